```python
import math
import jax, jax.numpy as jnp
from jax import lax
import numpy as np

D_MODEL = 1024
BATCH = 8
SEQ = 2048
DEPTH = 4
DEC_BATCH = 32
DEC_SEQ = 4
PAST_LEN = 8192
PAGE_SIZE = 128

N_A_LAYERS = DEPTH // 2
N_B_LAYERS = DEPTH - N_A_LAYERS
D_RNN = D_MODEL
N_RNN_BLOCKS = 16
RNN_BLOCK_W = D_RNN // N_RNN_BLOCKS
CONV_WIDTH = 4
LRU_C = 8.0
N_HEADS = 8
HEAD_DIM = D_MODEL // N_HEADS
N_KV_HEADS = N_HEADS
ROPE_DIM = HEAD_DIM // 4
ROPE_THETA = 500000.0
MOBA_BLOCK = 256
MOBA_TOP_K = 3
D_FF = 4 * D_MODEL
Q_CHUNK = 32
NORM_EPS = 1e-6

kernel_name = 'hawk_moba_yoco_decoder_step'


def round_up(n, m):
    return -(-n // m) * m


def rmsnorm(x, g):
    xf = x.astype(jnp.float32)
    y = xf * lax.rsqrt(jnp.mean(xf * xf, axis=-1, keepdims=True) + NORM_EPS)
    return (y * g.astype(jnp.float32)).astype(x.dtype)


def adaln(c, w, b, n):
    m = jax.nn.silu(c) @ w + b
    return jnp.split(m[:, None, :], n, axis=-1)


def modulate(x, g, shift, scale):
    return rmsnorm(x, g) * (1 + scale) + shift


def rope(x, pos):
    half = ROPE_DIM // 2
    inv = ROPE_THETA ** (-jnp.arange(half, dtype=jnp.float32) * 2.0 / ROPE_DIM)
    ang = pos.astype(jnp.float32)[:, None] * inv[None, :]
    cos, sin = jnp.cos(ang)[:, None, :], jnp.sin(ang)[:, None, :]
    xf = x.astype(jnp.float32)
    x1, x2, rest = xf[..., :half], xf[..., half:ROPE_DIM], xf[..., ROPE_DIM:]
    return jnp.concatenate([x1 * cos - x2 * sin, x2 * cos + x1 * sin, rest], -1).astype(x.dtype)


def causal_conv(x, state, w, b):
    s = x.shape[1]
    xp = jnp.concatenate([state.astype(x.dtype), x], axis=1)
    y = b + sum(xp[:, j:j + s] * w[j] for j in range(CONV_WIDTH))
    return y, xp[:, -(CONV_WIDTH - 1):]


def rglru(x, h0, w_gate, b_gate, lam):
    b, s, _ = x.shape
    xb = x.reshape(b, s, N_RNN_BLOCKS, RNN_BLOCK_W)
    g = jnp.einsum('bsnk,nkj->bsnj', xb, w_gate).reshape(b, s, N_RNN_BLOCKS, 2, RNN_BLOCK_W)
    g = g.transpose(0, 1, 3, 2, 4).reshape(b, s, 2, D_RNN) + b_gate
    g = g.astype(jnp.float32)
    r, i = jax.nn.sigmoid(g[:, :, 0]), jax.nn.sigmoid(g[:, :, 1])
    log_a = -LRU_C * r * jax.nn.softplus(-lam.astype(jnp.float32))
    a = jnp.exp(log_a)
    u = jnp.sqrt(-jnp.expm1(2.0 * log_a)) * (i * x.astype(jnp.float32))

    def combine(left, right):
        a1, b1 = left
        a2, b2 = right
        return a1 * a2, a2 * b1 + b2

    a_cum, u_cum = lax.associative_scan(combine, (a, u), axis=1)
    h = a_cum * h0.astype(jnp.float32)[:, None, :] + u_cum
    return h.astype(x.dtype), h[:, -1].astype(h0.dtype)


def recurrent_block(h, conv0, h0, w_in, conv_w, conv_b, w_gate, b_gate, lam, w_out):
    gb, xb = jnp.split(h @ w_in, 2, axis=-1)
    xc, conv_new = causal_conv(xb, conv0, conv_w, conv_b)
    y, h_last = rglru(xc, h0, w_gate, b_gate, lam)
    return (jax.nn.gelu(gb) * y) @ w_out, conv_new, h_last


def sq_relu_mlp(h, w_up, w_down):
    return jnp.square(jax.nn.relu(h @ w_up)) @ w_down


def shared_kv(x, c, pos, g_kv, w_ada_kv, b_ada_kv, w_kv, g_k):
    shift, scale = adaln(c, w_ada_kv, b_ada_kv, 2)
    h = modulate(x, g_kv, shift, scale)
    b, s, _ = x.shape
    kv = (h @ w_kv).reshape(b, s, 2, N_KV_HEADS, HEAD_DIM)
    k = rope(rmsnorm(kv[:, :, 0], g_k), pos)
    v = kv[:, :, 1]
    return k.transpose(0, 2, 1, 3), v.transpose(0, 2, 1, 3)


def to_pages(t, n_pages):
    b, h, s, d = t.shape
    t = jnp.pad(t, ((0, 0), (0, 0), (0, n_pages * PAGE_SIZE - s), (0, 0)))
    return t.reshape(b, h, n_pages, PAGE_SIZE, d).transpose(0, 2, 1, 3, 4)


def block_means(page_sums):
    ppb = MOBA_BLOCK // PAGE_SIZE
    b, n, h, d = page_sums.shape
    return page_sums.reshape(b, n // ppb, ppb, h, d).sum(2) / MOBA_BLOCK


def moba_chunk(q, pos, means, fetch):
    qc, nh, _ = q.shape
    nb = means.shape[0]
    ppb = MOBA_BLOCK // PAGE_SIZE
    cur = pos // MOBA_BLOCK
    gate = jnp.einsum('qhd,nhd->qhn', q.astype(jnp.float32), means.astype(jnp.float32))
    past = jnp.arange(nb)[None, None, :] < cur[:, None, None]
    gate = jnp.where(past, gate, -jnp.inf)
    _, top = lax.top_k(gate, min(MOBA_TOP_K, nb))
    own = jnp.broadcast_to(cur[:, None, None], (qc, nh, 1)).astype(top.dtype)
    blocks = jnp.concatenate([top, own], axis=-1)
    ok = jnp.concatenate([top < cur[:, None, None], jnp.ones((qc, nh, 1), bool)], axis=-1)
    lp = blocks[..., None] * ppb + jnp.arange(ppb)
    kg, vg = fetch(lp)
    kpos = lp[..., None] * PAGE_SIZE + jnp.arange(PAGE_SIZE)
    mask = ok[..., None, None] & (kpos <= pos[:, None, None, None, None])
    s = jnp.einsum('qhd,qhjpkd->qhjpk', q, kg).astype(jnp.float32) * (HEAD_DIM ** -0.5)
    s = jnp.where(mask, s, -jnp.inf)
    p = jax.nn.softmax(s.reshape(qc, nh, -1), axis=-1).reshape(s.shape)
    return jnp.einsum('qhjpk,qhjpkd->qhd', p.astype(vg.dtype), vg)


def moba_attend(q, pos, means, per_seq, fetch):
    b, sq, nh, hd = q.shape
    qc = math.gcd(sq, Q_CHUNK)
    nq = sq // qc
    pos_c = pos.reshape(nq, qc)

    def one_seq(args):
        q_b, means_b, seq_b = args

        def one_chunk(a):
            q_c, p_c = a
            return moba_chunk(q_c, p_c, means_b, lambda lp: fetch(seq_b, lp))

        return lax.map(one_chunk, (q_b.reshape(nq, qc, nh, hd), pos_c)).reshape(sq, nh, hd)

    return lax.map(one_seq, (q, means, per_seq))


def moba_layer(h, pos, means, per_seq, fetch, w_q, g_q, w_o):
    b, s, _ = h.shape
    q = rope(rmsnorm((h @ w_q).reshape(b, s, N_HEADS, HEAD_DIM), g_q), pos)
    o = moba_attend(q, pos, means, per_seq, fetch)
    return o.reshape(b, s, N_HEADS * HEAD_DIM) @ w_o


def fetch_local(seq, lp):
    kp, vp = seq
    hidx = jnp.arange(N_KV_HEADS)[None, :, None, None]
    return kp[lp, hidx], vp[lp, hidx]


def run_group(x, c, pos, conv0, h0, make_ctx, w):
    (g_norm, w_ada, b_ada, w_rec_in, conv_w, conv_b, w_gate, b_gate, lam, w_rec_out,
     g_kv, w_ada_kv, b_ada_kv, w_kv, g_k, w_q, g_q, w_o, w_up, w_down) = w
    conv_new, h_new = [], []
    for l in range(DEPTH):
        if l == N_A_LAYERS:
            k, v = shared_kv(x, c, pos, g_kv, w_ada_kv, b_ada_kv, w_kv, g_k)
            means, per_seq, fetch = make_ctx(k, v)
        shift, scale, gate = adaln(c, w_ada[l, 0], b_ada[l, 0], 3)
        h = modulate(x, g_norm[l, 0], shift, scale)
        if l < N_A_LAYERS:
            out, cs, hs = recurrent_block(h, conv0[l], h0[l], w_rec_in[l], conv_w[l], conv_b[l],
                                          w_gate[l], b_gate[l], lam[l], w_rec_out[l])
            conv_new.append(cs)
            h_new.append(hs)
        else:
            j = l - N_A_LAYERS
            out = moba_layer(h, pos, means, per_seq, fetch, w_q[j], g_q[j], w_o[j])
        x = x + gate * out
        shift, scale, gate = adaln(c, w_ada[l, 1], b_ada[l, 1], 3)
        h = modulate(x, g_norm[l, 1], shift, scale)
        x = x + gate * sq_relu_mlp(h, w_up[l], w_down[l])
    return x, jnp.stack(conv_new), jnp.stack(h_new), k, v


def setup_inputs(seed: int = 0) -> dict:
    key = jax.random.key(seed)
    ks = iter(jax.random.split(key, 40))
    nrm = lambda shape, s=1.0: jax.random.normal(next(ks), shape, jnp.float32) * s
    n_pages = PAST_LEN // PAGE_SIZE
    n_pool = (DEC_BATCH * n_pages * 5) // 4
    perm = jax.random.permutation(next(ks), n_pool)[:DEC_BATCH * n_pages]
    page_table = perm.reshape(DEC_BATCH, n_pages).astype(jnp.int32)
    u = jax.random.uniform(next(ks), (N_A_LAYERS, D_RNN), jnp.float32, 0.9, 0.999)
    a = u ** (1.0 / LRU_C)
    lam = jnp.log(a) - jnp.log1p(-a)
    return {
        'x_prompt': nrm((BATCH, SEQ, D_MODEL)),
        'x_sample': nrm((DEC_BATCH, DEC_SEQ, D_MODEL)),
        'c_prompt': nrm((BATCH, D_MODEL)),
        'c_sample': nrm((DEC_BATCH, D_MODEL)),
        'state_conv': nrm((N_A_LAYERS, DEC_BATCH, CONV_WIDTH - 1, D_RNN)),
        'state_rglru': nrm((N_A_LAYERS, DEC_BATCH, D_RNN), 0.5),
        'cache_k': nrm((n_pool, N_KV_HEADS, PAGE_SIZE, HEAD_DIM)),
        'cache_v': nrm((n_pool, N_KV_HEADS, PAGE_SIZE, HEAD_DIM)),
        'page_table': page_table,
        'g_norm': 1.0 + nrm((DEPTH, 2, D_MODEL), 0.02),
        'w_ada': nrm((DEPTH, 2, D_MODEL, 3 * D_MODEL), 0.5 * D_MODEL ** -0.5),
        'b_ada': nrm((DEPTH, 2, 3 * D_MODEL), 0.02),
        'w_rec_in': nrm((N_A_LAYERS, D_MODEL, 2 * D_RNN), D_MODEL ** -0.5),
        'conv_w': nrm((N_A_LAYERS, CONV_WIDTH, D_RNN), CONV_WIDTH ** -0.5),
        'conv_b': nrm((N_A_LAYERS, D_RNN), 0.02),
        'w_gate': nrm((N_A_LAYERS, N_RNN_BLOCKS, RNN_BLOCK_W, 2 * RNN_BLOCK_W), RNN_BLOCK_W ** -0.5),
        'b_gate': nrm((N_A_LAYERS, 2, D_RNN), 0.02),
        'lam': lam,
        'w_rec_out': nrm((N_A_LAYERS, D_RNN, D_MODEL), D_RNN ** -0.5),
        'g_kv': 1.0 + nrm((D_MODEL,), 0.02),
        'w_ada_kv': nrm((D_MODEL, 2 * D_MODEL), 0.5 * D_MODEL ** -0.5),
        'b_ada_kv': nrm((2 * D_MODEL,), 0.02),
        'w_kv': nrm((D_MODEL, 2 * N_KV_HEADS * HEAD_DIM), D_MODEL ** -0.5),
        'g_k': 1.0 + nrm((HEAD_DIM,), 0.02),
        'w_q': nrm((N_B_LAYERS, D_MODEL, N_HEADS * HEAD_DIM), D_MODEL ** -0.5),
        'g_q': 1.0 + nrm((N_B_LAYERS, HEAD_DIM), 0.02),
        'w_o': nrm((N_B_LAYERS, N_HEADS * HEAD_DIM, D_MODEL), (N_HEADS * HEAD_DIM) ** -0.5),
        'w_up': nrm((DEPTH, D_MODEL, D_FF), D_MODEL ** -0.5),
        'w_down': nrm((DEPTH, D_FF, D_MODEL), D_FF ** -0.5),
    }


def reference(x_prompt, x_sample, c_prompt, c_sample, state_conv, state_rglru, cache_k, cache_v,
              page_table, g_norm, w_ada, b_ada, w_rec_in, conv_w, conv_b, w_gate, b_gate, lam,
              w_rec_out, g_kv, w_ada_kv, b_ada_kv, w_kv, g_k, w_q, g_q, w_o, w_up, w_down):
    w = (g_norm, w_ada, b_ada, w_rec_in, conv_w, conv_b, w_gate, b_gate, lam, w_rec_out,
         g_kv, w_ada_kv, b_ada_kv, w_kv, g_k, w_q, g_q, w_o, w_up, w_down)
    ppb = MOBA_BLOCK // PAGE_SIZE
    n_past = page_table.shape[1]

    def prompt_ctx(k, v):
        n = round_up(-(-k.shape[2] // PAGE_SIZE), ppb)
        kp, vp = to_pages(k, n), to_pages(v, n)
        return block_means(kp.astype(jnp.float32).sum(3)), (kp, vp), fetch_local

    def sample_ctx(k, v):
        n_new = -(-k.shape[2] // PAGE_SIZE)
        kn, vn = to_pages(k, n_new), to_pages(v, n_new)
        past_sums = lax.map(lambda pt: cache_k[pt].astype(jnp.float32).sum(2), page_table)
        sums = jnp.concatenate([past_sums, kn.astype(jnp.float32).sum(3)], axis=1)
        n_tot = round_up(n_past + n_new, ppb)
        sums = jnp.pad(sums, ((0, 0), (0, n_tot - n_past - n_new), (0, 0), (0, 0)))

        def fetch_paged(seq, lp):
            pt_b, kn_b, vn_b = seq
            hidx = jnp.arange(N_KV_HEADS)[None, :, None, None]
            is_new = (lp >= n_past)[..., None, None]
            phys = pt_b[jnp.clip(lp, 0, n_past - 1)]
            newi = jnp.clip(lp - n_past, 0, n_new - 1)
            kg = jnp.where(is_new, kn_b[newi, hidx], cache_k[phys, hidx])
            vg = jnp.where(is_new, vn_b[newi, hidx], cache_v[phys, hidx])
            return kg, vg

        return block_means(sums), (page_table, kn, vn), fetch_paged

    bp, sp = x_prompt.shape[0], x_prompt.shape[1]
    pos_p = jnp.arange(sp, dtype=jnp.int32)
    conv0_p = jnp.zeros((N_A_LAYERS, bp, CONV_WIDTH - 1, D_RNN), x_prompt.dtype)
    h0_p = jnp.zeros((N_A_LAYERS, bp, D_RNN), x_prompt.dtype)
    y_prompt, conv_p, h_p, k_p, v_p = run_group(x_prompt, c_prompt, pos_p, conv0_p, h0_p, prompt_ctx, w)

    ss = x_sample.shape[1]
    pos_s = n_past * PAGE_SIZE + jnp.arange(ss, dtype=jnp.int32)
    y_sample, conv_s, h_s, k_s, v_s = run_group(x_sample, c_sample, pos_s, state_conv, state_rglru, sample_ctx, w)
    return (y_prompt, y_sample, conv_p, h_p, conv_s, h_s, k_p, v_p, k_s, v_s)
```

```python
import functools
import math

import jax
import jax.numpy as jnp
from jax import lax
from jax.experimental import pallas as pl
from jax.experimental.pallas import tpu as pltpu

F32 = jnp.float32
BF16 = jnp.bfloat16

NORM_EPS = 1e-6
LRU_C = 8.0
CONV_WIDTH = 4
PAGE_SIZE = 128
MOBA_BLOCK = 256
MOBA_TOP_K = 3
PAGES_PER_BLOCK = MOBA_BLOCK // PAGE_SIZE
N_HEADS = 8
ROPE_THETA = 500000.0
NEG = -1e30
LANES = 128
SUBLANES = 8
GATE_GROUP = 256
VMEM_LIMIT = 56 * 1024 * 1024


def _params(*sem):
    return pltpu.CompilerParams(dimension_semantics=sem, vmem_limit_bytes=VMEM_LIMIT)


def _resident(shape):
    nd = len(shape)
    return pl.BlockSpec(shape, lambda *_: (0,) * nd, pipeline_mode=pl.Buffered(1))


def _dot(a, b):
    return jnp.dot(a, b, preferred_element_type=F32)


def _dot_nt(a, b):
    return lax.dot_general(a, b, (((1,), (1,)), ((), ())), preferred_element_type=F32)


def _split(a):
    hi = a.astype(BF16)
    return hi, (a - hi.astype(F32)).astype(BF16)


def _dot3(a, b, dot=_dot):
    ah, al = _split(a)
    bh, bl = _split(b)
    return dot(ah, bh) + (dot(ah, bl) + dot(al, bh))


def _sigmoid(x):
    return 0.5 + 0.5 * jnp.tanh(0.5 * x)


def _gelu_tanh(x):
    return 0.5 * x * (1.0 + jnp.tanh(math.sqrt(2.0 / math.pi) * (x + 0.044715 * (x * x * x))))


def _softplus(x):
    return jnp.maximum(x, 0.0) + jnp.log1p(jnp.exp(-jnp.abs(x)))


def _modulate(x, g, mod, d):
    ms = jnp.mean(x * x, axis=-1, keepdims=True)
    y = x * lax.rsqrt(ms + NORM_EPS) * g
    return y * (1.0 + mod[:, d:2 * d]) + mod[:, :d]


def _adaln_kernel(c_ref, w_ref, b_ref, o_ref):
    c = c_ref[...]
    o_ref[0] = _dot3(c * _sigmoid(c), w_ref[0]) + b_ref[0]


def _adaln(c, w, b):
    r, d = c.shape
    m, _, n = w.shape
    tn = 1024
    return pl.pallas_call(
        _adaln_kernel,
        grid=(m, n // tn),
        in_specs=[
            pl.BlockSpec((r, d), lambda i, j: (0, 0)),
            pl.BlockSpec((1, d, tn), lambda i, j: (i, 0, j)),
            pl.BlockSpec((1, 1, tn), lambda i, j: (i, 0, j)),
        ],
        out_specs=pl.BlockSpec((1, r, tn), lambda i, j: (i, 0, j)),
        out_shape=jax.ShapeDtypeStruct((m, r, n), F32),
        compiler_params=_params("parallel", "parallel"),
    )(c, w, b)


def _mod_spec(mod, tm, tiles_per_seq):
    nb, rows, w = mod.shape
    if rows == 1:
        return pl.BlockSpec((1, 1, w), lambda i: (i // tiles_per_seq, 0, 0))
    return pl.BlockSpec((1, tm, w), lambda i: (0, i, 0))


def _rec_in_kernel(x_ref, mod_ref, g_ref, w_ref, ga_ref, xb_ref, *, d, dr):
    h = _modulate(x_ref[...], g_ref[...], mod_ref[0], d).astype(BF16)
    y = _dot(h, w_ref[...])
    ga_ref[...] = _gelu_tanh(y[:, :dr])
    xb_ref[...] = y[:, dr:]


def _rec_in(x, mod, g, w, tm, tiles_per_seq):
    t, d = x.shape
    dr = w.shape[1] // 2
    row = pl.BlockSpec((tm, dr), lambda i: (i, 0))
    return pl.pallas_call(
        functools.partial(_rec_in_kernel, d=d, dr=dr),
        grid=(t // tm,),
        in_specs=[
            pl.BlockSpec((tm, d), lambda i: (i, 0)),
            _mod_spec(mod, tm, tiles_per_seq),
            _resident((1, d)),
            _resident(w.shape),
        ],
        out_specs=[row, row],
        out_shape=[jax.ShapeDtypeStruct((t, dr), F32)] * 2,
        compiler_params=_params("parallel"),
    )(x, mod, g, w)


def _head_proj_kernel(x_ref, mod_ref, g_ref, w_ref, gh_ref, cos_ref, sa_ref, sb_ref, *out_refs,
                      d, hd, n_rope, n_plain, half):
    h = _modulate(x_ref[...], g_ref[...], mod_ref[0], d).astype(BF16)
    y = _dot(h, w_ref[...])
    cos, sa, sb = cos_ref[...], sa_ref[...], sb_ref[...]
    gh = gh_ref[...]
    for i in range(n_rope):
        yh = y[:, i * hd:(i + 1) * hd]
        yh = yh * lax.rsqrt(jnp.mean(yh * yh, axis=-1, keepdims=True) + NORM_EPS) * gh
        out_refs[0][0, i] = (yh * cos + pltpu.roll(yh, hd - half, 1) * sa
                             + pltpu.roll(yh, half, 1) * sb)
    for i in range(n_plain):
        out_refs[1][0, i] = y[:, (n_rope + i) * hd:(n_rope + i + 1) * hd]


def _head_proj(x, mod, g, w, gh, tables, nb, seq, tm, n_rope, n_plain, hd):
    t, d = x.shape
    tps = seq // tm
    tab = pl.BlockSpec((tm, hd), lambda i: (i % tps, 0))
    out_specs = [pl.BlockSpec((1, n_rope, tm, hd), lambda i: (i // tps, 0, i % tps, 0))]
    out_shape = [jax.ShapeDtypeStruct((nb, n_rope, seq, hd), F32)]
    if n_plain:
        out_specs.append(pl.BlockSpec((1, n_plain, tm, hd), lambda i: (i // tps, 0, i % tps, 0)))
        out_shape.append(jax.ShapeDtypeStruct((nb, n_plain, seq, hd), F32))
    return pl.pallas_call(
        functools.partial(_head_proj_kernel, d=d, hd=hd, n_rope=n_rope, n_plain=n_plain,
                          half=hd // 8),
        grid=(t // tm,),
        in_specs=[
            pl.BlockSpec((tm, d), lambda i: (i, 0)),
            _mod_spec(mod, tm, tps),
            _resident((1, d)),
            _resident(w.shape),
            _resident((1, hd)),
            tab, tab, tab,
        ],
        out_specs=out_specs,
        out_shape=out_shape,
        compiler_params=_params("parallel"),
    )(x, mod, g, w, gh, *tables)


def _out_res_kernel(x_ref, z_ref, mod_ref, w_ref, o_ref, *, d):
    gate = mod_ref[0][:, 2 * d:]
    o_ref[...] = x_ref[...] + gate * _dot(z_ref[...].astype(BF16), w_ref[...])


def _out_res(x, z, mod, w, tm, tiles_per_seq):
    t, d = x.shape
    k = z.shape[1]
    return pl.pallas_call(
        functools.partial(_out_res_kernel, d=d),
        grid=(t // tm,),
        in_specs=[
            pl.BlockSpec((tm, d), lambda i: (i, 0)),
            pl.BlockSpec((tm, k), lambda i: (i, 0)),
            _mod_spec(mod, tm, tiles_per_seq),
            _resident(w.shape),
        ],
        out_specs=pl.BlockSpec((tm, d), lambda i: (i, 0)),
        out_shape=jax.ShapeDtypeStruct((t, d), F32),
        compiler_params=_params("parallel"),
    )(x, z, mod, w)


def _mlp_kernel(x_ref, mod_ref, g_ref, wu_ref, wd_ref, o_ref, h_ref, acc_ref, *, d, n_chunks):
    x = x_ref[...]
    mod = mod_ref[0]
    h_ref[...] = _modulate(x, g_ref[...], mod, d).astype(BF16)
    for c in range(n_chunks):
        u = _dot(h_ref[...], wu_ref[c])
        a = jnp.square(jnp.maximum(u, 0.0)).astype(BF16)
        part = _dot(a, wd_ref[c])
        if c == 0:
            acc_ref[...] = part
        else:
            acc_ref[...] += part
    o_ref[...] = x + mod[:, 2 * d:] * acc_ref[...]


def _mlp(x, mod, g, wu, wd, tm, tiles_per_seq):
    t, d = x.shape
    return pl.pallas_call(
        functools.partial(_mlp_kernel, d=d, n_chunks=wu.shape[0]),
        grid=(t // tm,),
        in_specs=[
            pl.BlockSpec((tm, d), lambda i: (i, 0)),
            _mod_spec(mod, tm, tiles_per_seq),
            _resident((1, d)),
            _resident(wu.shape),
            _resident(wd.shape),
        ],
        out_specs=pl.BlockSpec((tm, d), lambda i: (i, 0)),
        out_shape=jax.ShapeDtypeStruct((t, d), F32),
        scratch_shapes=[pltpu.VMEM((tm, d), BF16), pltpu.VMEM((tm, d), F32)],
        compiler_params=_params("parallel"),
    )(x, mod, g, wu, wd)


def _lru_inputs(xc, wr_ref, wi_ref, bg_ref, sp):
    xh = xc.astype(BF16)
    gw = wr_ref.shape[1]
    r = jnp.concatenate([_dot(xh[:, j * gw:(j + 1) * gw], wr_ref[j])
                         for j in range(wr_ref.shape[0])], axis=1) + bg_ref[0:1, :]
    i = jnp.concatenate([_dot(xh[:, j * gw:(j + 1) * gw], wi_ref[j])
                         for j in range(wi_ref.shape[0])], axis=1) + bg_ref[1:2, :]
    log_a = -LRU_C * _sigmoid(r) * sp
    t = jnp.tanh(0.5 * log_a)
    rc = 1.0 / (1.0 - t)
    a = (1.0 + t) * rc
    u = (2.0 * rc) * jnp.sqrt(-t) * (_sigmoid(i) * xc)
    return a, u


def _scan_prompt_kernel(xb_ref, ga_ref, st_ref, h0_ref, cw_ref, cb_ref, wr_ref, wi_ref, bg_ref,
                        lam_ref, z_ref, tail_out_ref, hl_ref, tail_ref, hc_ref, a_ref, u_ref,
                        *, tt):
    ti = pl.program_id(1)
    dr = xb_ref.shape[1]

    @pl.when(ti == 0)
    def _():
        tail_ref[...] = st_ref[0]
        hc_ref[...] = h0_ref[0]

    x = xb_ref[...]
    tail = tail_ref[...]
    row8 = lax.broadcasted_iota(jnp.int32, (SUBLANES, dr), 0)
    xc = cb_ref[...] + cw_ref[CONV_WIDTH - 1:CONV_WIDTH, :] * x
    for k in range(1, CONV_WIDTH):
        xs = pltpu.roll(x, k, 0)
        first = jnp.where(row8 < k, pltpu.roll(tail, k, 0), xs[:SUBLANES])
        xs = jnp.concatenate([first, xs[SUBLANES:]], axis=0)
        xc = xc + cw_ref[CONV_WIDTH - 1 - k:CONV_WIDTH - k, :] * xs
    tail_ref[...] = x[tt - SUBLANES:]
    tail_out_ref[0] = x[tt - SUBLANES:]

    a, u = _lru_inputs(xc, wr_ref, wi_ref, bg_ref, _softplus(-lam_ref[...]))
    a_ref[...] = a
    u_ref[...] = u

    def group(gi, carry):
        r0 = pl.multiple_of(gi * SUBLANES, SUBLANES)
        a8 = a_ref[pl.ds(r0, SUBLANES), :]
        u8 = u_ref[pl.ds(r0, SUBLANES), :]
        for s in (1, 2, 4):
            keep = row8 >= s
            u8 = jnp.where(keep, a8 * pltpu.roll(u8, s, 0) + u8, u8)
            a8 = jnp.where(keep, a8 * pltpu.roll(a8, s, 0), a8)
        h8 = a8 * carry + u8
        z_ref[pl.ds(r0, SUBLANES), :] = h8 * ga_ref[pl.ds(r0, SUBLANES), :]
        return jnp.broadcast_to(h8[SUBLANES - 1:SUBLANES, :], (SUBLANES, dr))

    carry = lax.fori_loop(0, tt // SUBLANES, group,
                          jnp.broadcast_to(hc_ref[...], (SUBLANES, dr)), unroll=2)
    hc_ref[...] = carry[0:1, :]
    hl_ref[0] = carry[0:1, :]


def _scan_prompt(xb, ga, st8, h0, cw, cb, wr, wi, bg, lam, nb, seq, tt):
    t, dr = xb.shape
    nt = seq // tt
    row = pl.BlockSpec((tt, dr), lambda b, i: (b * nt + i, 0))
    per_seq8 = pl.BlockSpec((1, SUBLANES, dr), lambda b, i: (b, 0, 0))
    per_seq1 = pl.BlockSpec((1, 1, dr), lambda b, i: (b, 0, 0))
    return pl.pallas_call(
        functools.partial(_scan_prompt_kernel, tt=tt),
        grid=(nb, nt),
        in_specs=[row, row, per_seq8, per_seq1, _resident(cw.shape), _resident(cb.shape),
                  _resident(wr.shape), _resident(wi.shape), _resident(bg.shape),
                  _resident(lam.shape)],
        out_specs=[row, per_seq8, per_seq1],
        out_shape=[jax.ShapeDtypeStruct((t, dr), F32),
                   jax.ShapeDtypeStruct((nb, SUBLANES, dr), F32),
                   jax.ShapeDtypeStruct((nb, 1, dr), F32)],
        scratch_shapes=[pltpu.VMEM((SUBLANES, dr), F32), pltpu.VMEM((1, dr), F32),
                        pltpu.VMEM((tt, dr), F32), pltpu.VMEM((tt, dr), F32)],
        compiler_params=_params("parallel", "arbitrary"),
    )(xb, ga, st8, h0, cw, cb, wr, wi, bg, lam)


def _scan_steps_kernel(xb_ref, ga_ref, st_ref, h0_ref, cw_ref, cb_ref, wr_ref, wi_ref, bg_ref,
                       lam_ref, z_ref, hl_ref, *, steps):
    rows = [st_ref[j] for j in range(CONV_WIDTH - 1)] + [xb_ref[t] for t in range(steps)]
    sp = _softplus(-lam_ref[...])
    h = h0_ref[...]
    for t in range(steps):
        xc = cb_ref[...]
        for j in range(CONV_WIDTH):
            xc = xc + cw_ref[j:j + 1, :] * rows[t + j]
        a, u = _lru_inputs(xc, wr_ref, wi_ref, bg_ref, sp)
        h = a * h + u
        z_ref[t] = h * ga_ref[t]
    hl_ref[...] = h


def _scan_steps(xb, ga, st, h0, cw, cb, wr, wi, bg, lam):
    steps, nb, dr = xb.shape
    args = (xb, ga, st, h0, cw, cb, wr, wi, bg, lam)
    return pl.pallas_call(
        functools.partial(_scan_steps_kernel, steps=steps),
        grid=(1,),
        in_specs=[_resident(a.shape) for a in args],
        out_specs=[pl.BlockSpec((steps, nb, dr), lambda i: (0, 0, 0)),
                   pl.BlockSpec((nb, dr), lambda i: (0, 0))],
        out_shape=[jax.ShapeDtypeStruct((steps, nb, dr), F32),
                   jax.ShapeDtypeStruct((nb, dr), F32)],
        compiler_params=_params("arbitrary"),
    )(*args)


def _select_blocks(gate, n_valid):
    lane = lax.broadcasted_iota(jnp.int32, gate.shape, 1)
    lane_f = lane.astype(F32)
    g = jnp.where(lane < n_valid, gate, NEG)
    bias = jnp.full(gate.shape, NEG, F32)
    for _ in range(MOBA_TOP_K):
        m = jnp.max(g, axis=1, keepdims=True)
        idx = jnp.min(jnp.where(g == m, lane_f, float(LANES)), axis=1, keepdims=True)
        pick = (lane_f == idx) & (m > 0.5 * NEG)
        bias = jnp.where(pick, 0.0, bias)
        g = jnp.where(pick, NEG, g)
    return bias


def _online_update(m, l, acc, s, v):
    m_new = jnp.maximum(m, jnp.max(s, axis=1, keepdims=True))
    alpha = jnp.exp(m - m_new)
    p = jnp.exp(s - m_new)
    l = alpha * l + jnp.sum(p, axis=1, keepdims=True)
    acc = alpha * acc + _dot(p.astype(BF16), v)
    return m_new, l, acc


def _attn_prompt_kernel(q_ref, k_ref, v_ref, o_ref, means_ref, *, seq, hd, scale):
    qi = pl.program_id(2)
    nblk = seq // MOBA_BLOCK
    bq = MOBA_BLOCK

    @pl.when(qi == 0)
    def _():
        means_ref[...] = jnp.zeros(means_ref.shape, F32)
        for n in range(nblk):
            kb = k_ref[0, 0, n * bq:(n + 1) * bq, :]
            means_ref[n:n + 1, :] = jnp.sum(kb, axis=0, keepdims=True) * (1.0 / MOBA_BLOCK)

    q = q_ref[0, 0]
    bias = _select_blocks(_dot3(q, means_ref[...], _dot_nt), qi)
    qs = (q * scale).astype(BF16)
    q_aug = jnp.concatenate([qs, bias.astype(BF16)], axis=1)

    start = pl.multiple_of(qi * bq, bq)
    s = _dot_nt(qs, k_ref[0, 0, pl.ds(start, bq), :].astype(BF16))
    causal = (lax.broadcasted_iota(jnp.int32, (bq, bq), 1)
              <= lax.broadcasted_iota(jnp.int32, (bq, bq), 0))
    s = jnp.where(causal, s, NEG)
    m = jnp.max(s, axis=1, keepdims=True)
    p = jnp.exp(s - m)
    l = jnp.sum(p, axis=1, keepdims=True)
    acc = _dot(p.astype(BF16), v_ref[0, 0, pl.ds(start, bq), :].astype(BF16))

    lane = lax.broadcasted_iota(jnp.int32, (bq, hd), 1)

    def past(n, carry):
        st = pl.multiple_of(n * bq, bq)
        onehot = jnp.where(lane == n, 1.0, 0.0).astype(BF16)
        k_aug = jnp.concatenate([k_ref[0, 0, pl.ds(st, bq), :].astype(BF16), onehot], axis=1)
        return _online_update(*carry, _dot_nt(q_aug, k_aug),
                              v_ref[0, 0, pl.ds(st, bq), :].astype(BF16))

    m, l, acc = lax.fori_loop(0, qi, past, (m, l, acc))
    o_ref[0] = acc / l


def _attn_prompt(q, k, v):
    b, h, seq, hd = q.shape
    assert seq % MOBA_BLOCK == 0 and seq // MOBA_BLOCK <= LANES and hd == LANES
    full = pl.BlockSpec((1, 1, seq, hd), lambda bi, hi, qi: (bi, hi, 0, 0))
    return pl.pallas_call(
        functools.partial(_attn_prompt_kernel, seq=seq, hd=hd, scale=hd ** -0.5),
        grid=(b, h, seq // MOBA_BLOCK),
        in_specs=[pl.BlockSpec((1, 1, MOBA_BLOCK, hd), lambda bi, hi, qi: (bi, hi, qi, 0)),
                  full, full],
        out_specs=pl.BlockSpec((1, MOBA_BLOCK, hd), lambda bi, hi, qi: (bi, qi, hi)),
        out_shape=jax.ShapeDtypeStruct((b, seq, h * hd), F32),
        scratch_shapes=[pltpu.VMEM((LANES, hd), F32)],
        compiler_params=_params("parallel", "parallel", "arbitrary"),
    )(q, k, v)


def _page_means_kernel(pt_ref, *refs, n):
    k_refs, o_ref = refs[:n], refs[n]
    for h in range(o_ref.shape[1]):
        sums = [jnp.sum(k_refs[j][0, h], axis=0, keepdims=True) for j in range(n)]
        o_ref[0, h] = jnp.concatenate(
            [(sums[2 * i] + sums[2 * i + 1]) * (1.0 / MOBA_BLOCK) for i in range(n // 2)], axis=0)


def _page_spec(n_per_step, j, h, hd):
    return pl.BlockSpec((1, h, PAGE_SIZE, hd),
                        lambda b, g, pt: (pt[b, g * n_per_step + j], 0, 0, 0))


def _page_means(cache_k, page_table):
    db, n_pages = page_table.shape
    _, h, _, hd = cache_k.shape
    n = min(16, n_pages)
    return pl.pallas_call(
        functools.partial(_page_means_kernel, n=n),
        grid_spec=pltpu.PrefetchScalarGridSpec(
            num_scalar_prefetch=1,
            grid=(db, n_pages // n),
            in_specs=[_page_spec(n, j, h, hd) for j in range(n)],
            out_specs=pl.BlockSpec((1, h, n // 2, hd), lambda b, g, pt: (b, 0, g, 0)),
        ),
        out_shape=jax.ShapeDtypeStruct((db, h, n_pages // 2, hd), F32),
        compiler_params=_params("parallel", "arbitrary"),
    )(page_table, *([cache_k] * n))


def _attn_paged_kernel(pt_ref, q_ref, means_ref, kn_ref, vn_ref, *refs, n, sq, scale):
    k_refs, v_refs, o_ref = refs[:n], refs[n:2 * n], refs[2 * n]
    m_ref, l_ref, acc_ref, qaug_ref = refs[2 * n + 1:]
    g = pl.program_id(1)
    n_heads, rows, hd = q_ref.shape[1:]
    n_past_blocks = means_ref.shape[2]

    @pl.when(g == 0)
    def _():
        row = lax.broadcasted_iota(jnp.int32, (rows, 1), 0)
        for h in range(n_heads):
            q = q_ref[0, h]
            means = jnp.concatenate(
                [means_ref[0, h], jnp.zeros((LANES - n_past_blocks, hd), F32)], axis=0)
            bias = _select_blocks(_dot3(q, means, _dot_nt), n_past_blocks)
            qaug_ref[h] = jnp.concatenate([(q * scale).astype(BF16), bias.astype(BF16)], axis=1)
            kn, vn = kn_ref[0, h], vn_ref[0, h]
            s = [jnp.where(row >= j, jnp.sum(q * kn[j:j + 1, :], axis=1, keepdims=True) * scale,
                           NEG) for j in range(sq)]
            m = functools.reduce(jnp.maximum, s)
            p = [jnp.exp(sj - m) for sj in s]
            m_ref[h] = m
            l_ref[h] = functools.reduce(lambda x, y: x + y, p)
            acc_ref[h] = functools.reduce(lambda x, y: x + y,
                                          [p[j] * vn[j:j + 1, :] for j in range(sq)])

    keys = n * PAGE_SIZE
    block_of_key = (g * (n // PAGES_PER_BLOCK)
                    + lax.broadcasted_iota(jnp.int32, (keys, hd), 0) // MOBA_BLOCK)
    onehot = jnp.where(lax.broadcasted_iota(jnp.int32, (keys, hd), 1) == block_of_key,
                       1.0, 0.0).astype(BF16)
    for h in range(n_heads):
        kh = jnp.concatenate([r[0, h] for r in k_refs], axis=0).astype(BF16)
        vh = jnp.concatenate([r[0, h] for r in v_refs], axis=0).astype(BF16)
        s = _dot_nt(qaug_ref[h], jnp.concatenate([kh, onehot], axis=1))
        m_ref[h], l_ref[h], acc_ref[h] = _online_update(m_ref[h], l_ref[h], acc_ref[h], s, vh)

    @pl.when(g == pl.num_programs(1) - 1)
    def _():
        for h in range(n_heads):
            o_ref[0, h] = acc_ref[h] / l_ref[h]


def _attn_paged(q, means, k_new, v_new, cache_k, cache_v, page_table, sq):
    db, n_pages = page_table.shape
    _, h, rows, hd = q.shape
    n = min(8, n_pages)
    assert n_pages % n == 0 and n % PAGES_PER_BLOCK == 0 and n_pages // 2 <= LANES
    per_seq = pl.BlockSpec((1, h, rows, hd), lambda b, g, pt: (b, 0, 0, 0))
    return pl.pallas_call(
        functools.partial(_attn_paged_kernel, n=n, sq=sq, scale=hd ** -0.5),
        grid_spec=pltpu.PrefetchScalarGridSpec(
            num_scalar_prefetch=1,
            grid=(db, n_pages // n),
            in_specs=[per_seq,
                      pl.BlockSpec((1, h, n_pages // 2, hd), lambda b, g, pt: (b, 0, 0, 0)),
                      per_seq, per_seq]
            + [_page_spec(n, j, h, hd) for j in range(n)] * 2,
            out_specs=per_seq,
            scratch_shapes=[pltpu.VMEM((h, rows, 1), F32), pltpu.VMEM((h, rows, 1), F32),
                            pltpu.VMEM((h, rows, hd), F32), pltpu.VMEM((h, rows, 2 * hd), BF16)],
        ),
        out_shape=jax.ShapeDtypeStruct((db, h, rows, hd), F32),
        compiler_params=_params("parallel", "arbitrary"),
    )(page_table, q, means, k_new, v_new, *([cache_k] * n), *([cache_v] * n))


def _rope_tables(pos, hd):
    half = hd // 8
    inv = ROPE_THETA ** (-jnp.arange(half, dtype=F32) * 2.0 / (2 * half))
    ang = pos.astype(F32)[:, None] * inv[None, :]
    cos, sin = jnp.cos(ang), jnp.sin(ang)
    n = pos.shape[0]
    ones = jnp.ones((n, hd - 2 * half), F32)
    zeros = jnp.zeros((n, hd - 2 * half), F32)
    z = jnp.zeros_like(sin)
    return (jnp.concatenate([cos, cos, ones], axis=1),
            jnp.concatenate([-sin, z, zeros], axis=1),
            jnp.concatenate([z, sin, zeros], axis=1))


def _block_diag_gate(w_gate):
    n, k, _ = w_gate.shape
    per = GATE_GROUP // k
    eye = jnp.eye(per, dtype=w_gate.dtype)

    def build(w):
        w = w.reshape(n // per, per, k, k)
        return jnp.einsum("gmkj,mn->gmknj", w, eye).reshape(n // per, GATE_GROUP, GATE_GROUP).astype(BF16)

    return build(w_gate[:, :, :k]), build(w_gate[:, :, k:])


def kernel(x_prompt, x_sample, c_prompt, c_sample, state_conv, state_rglru, cache_k, cache_v, page_table, g_norm, w_ada, b_ada, w_rec_in, conv_w, conv_b, w_gate, b_gate, lam, w_rec_out, g_kv, w_ada_kv, b_ada_kv, w_kv, g_k, w_q, g_q, w_o, w_up, w_down):
    bp, sp, d = x_prompt.shape
    db, sq, _ = x_sample.shape
    depth = g_norm.shape[0]
    n_a = w_rec_in.shape[0]
    dr = w_rec_in.shape[2] // 2
    hd = g_k.shape[0]
    n_heads = w_q.shape[2] // hd
    n_pages = page_table.shape[1]
    d_ff = w_up.shape[2]
    assert hd == LANES and n_heads == N_HEADS and sq <= SUBLANES and n_pages % PAGES_PER_BLOCK == 0
    tp, ts = bp * sp, db * sq

    c_all = jnp.concatenate([c_prompt, c_sample], axis=0)
    pad = (-c_all.shape[0]) % SUBLANES
    c_all = jnp.pad(c_all, ((0, pad), (0, 0)))
    mods = _adaln(c_all, w_ada.reshape(depth * 2, d, 3 * d), b_ada.reshape(depth * 2, 1, 3 * d))
    mods_kv = _adaln(c_all, w_ada_kv[None], b_ada_kv[None, None])[0]

    def group_mods(m, prompt):
        if prompt:
            return m[:bp, None, :]
        return jnp.repeat(m[bp:bp + db], sq, axis=0)[None]

    w_in16 = w_rec_in.astype(BF16)
    w_out16 = w_rec_out.astype(BF16)
    w_kv16, w_q16, w_o16 = w_kv.astype(BF16), w_q.astype(BF16), w_o.astype(BF16)
    n_chunks = d_ff // 1024
    w_up16 = w_up.astype(BF16).reshape(depth, d, n_chunks, d_ff // n_chunks).transpose(0, 2, 1, 3)
    w_down16 = w_down.astype(BF16).reshape(depth, n_chunks, d_ff // n_chunks, d)
    gates = [_block_diag_gate(w_gate[l]) for l in range(n_a)]

    def run_group(x3, prompt):
        nb, seq, _ = x3.shape
        t = nb * seq
        x = x3.reshape(t, d)
        tm = 512 if prompt else t
        tps = seq // tm if prompt else 1
        if prompt:
            pos = jnp.arange(seq, dtype=jnp.int32)
            tables = _rope_tables(pos, hd)
            tab_nb, tab_seq = nb, seq
        else:
            pos = n_pages * PAGE_SIZE + jnp.arange(seq, dtype=jnp.int32)
            tables = tuple(jnp.tile(tb, (nb, 1)) for tb in _rope_tables(pos, hd))
            tab_nb, tab_seq = 1, t
        conv_new, h_new = [], []
        k = v = means = None
        for l in range(depth):
            mod = group_mods(mods[2 * l], prompt)
            g1 = g_norm[l, 0][None]
            if l == n_a:
                k, v = _head_proj(x, group_mods(mods_kv, prompt), g_kv[None], w_kv16, g_k[None],
                                  tables, tab_nb, tab_seq, tm, n_heads, n_heads, hd)
                if not prompt:
                    means = _page_means(cache_k, page_table)
            if l < n_a:
                ga, xb = _rec_in(x, mod, g1, w_in16[l], tm, tps)
                wr, wi = gates[l]
                cw, cb, bg, lm = conv_w[l], conv_b[l][None], b_gate[l], lam[l][None]
                if prompt:
                    st8 = jnp.zeros((nb, SUBLANES, dr), F32)
                    h0 = jnp.zeros((nb, 1, dr), F32)
                    z, tail, h_last = _scan_prompt(xb, ga, st8, h0, cw, cb, wr, wi, bg, lm,
                                                   nb, seq, 256)
                    conv_new.append(tail[:, SUBLANES - (CONV_WIDTH - 1):])
                    h_new.append(h_last[:, 0])
                else:
                    xb_t = xb.reshape(nb, seq, dr).transpose(1, 0, 2)
                    ga_t = ga.reshape(nb, seq, dr).transpose(1, 0, 2)
                    st = state_conv[l].transpose(1, 0, 2)
                    z_t, h_last = _scan_steps(xb_t, ga_t, st, state_rglru[l], cw, cb, wr, wi, bg, lm)
                    z = z_t.transpose(1, 0, 2).reshape(t, dr)
                    hist = jnp.concatenate([state_conv[l], xb.reshape(nb, seq, dr)], axis=1)
                    conv_new.append(hist[:, -(CONV_WIDTH - 1):])
                    h_new.append(h_last)
                x = _out_res(x, z, mod, w_out16[l], tm, tps)
            else:
                j = l - n_a
                (q,) = _head_proj(x, mod, g1, w_q16[j], g_q[j][None], tables, tab_nb, tab_seq, tm,
                                  n_heads, 0, hd)
                if prompt:
                    o = _attn_prompt(q, k, v)
                    o = o.reshape(t, n_heads * hd)
                else:
                    def pad_rows(a):
                        a = a.reshape(n_heads, nb, seq, hd).transpose(1, 0, 2, 3)
                        return jnp.pad(a, ((0, 0), (0, 0), (0, SUBLANES - seq), (0, 0)))
                    o = _attn_paged(pad_rows(q), means, pad_rows(k), pad_rows(v), cache_k, cache_v,
                                    page_table, seq)
                    o = o[:, :, :seq].transpose(0, 2, 1, 3).reshape(t, n_heads * hd)
                x = _out_res(x, o, mod, w_o16[j], tm, tps)
            mod2 = group_mods(mods[2 * l + 1], prompt)
            x = _mlp(x, mod2, g_norm[l, 1][None], w_up16[l], w_down16[l], tm, tps)
        if not prompt:
            k = k.reshape(n_heads, nb, seq, hd).transpose(1, 0, 2, 3)
            v = v.reshape(n_heads, nb, seq, hd).transpose(1, 0, 2, 3)
        return x.reshape(nb, seq, d), jnp.stack(conv_new), jnp.stack(h_new), k, v

    y_p, conv_p, h_p, k_p, v_p = run_group(x_prompt, True)
    y_s, conv_s, h_s, k_s, v_s = run_group(x_sample, False)
    return (y_p, y_s, conv_p, h_p, conv_s, h_s, k_p, v_p, k_s, v_s)
```

```python
import functools
import math

import jax
import jax.numpy as jnp
from jax import lax
from jax.experimental import pallas as pl
from jax.experimental.pallas import tpu as pltpu

F32 = jnp.float32
BF16 = jnp.bfloat16

NORM_EPS = 1e-6
LRU_C = 8.0
CONV_WIDTH = 4
PAGE_SIZE = 128
MOBA_BLOCK = 256
MOBA_TOP_K = 3
PAGES_PER_BLOCK = MOBA_BLOCK // PAGE_SIZE
N_HEADS = 8
ROPE_THETA = 500000.0
NEG = -1e30
LANES = 128
SUBLANES = 8
GATE_GROUP = 256
VMEM_LIMIT = 56 * 1024 * 1024


def _params(*sem):
    return pltpu.CompilerParams(dimension_semantics=sem, vmem_limit_bytes=VMEM_LIMIT)


def _resident(shape):
    nd = len(shape)
    return pl.BlockSpec(shape, lambda *_: (0,) * nd, pipeline_mode=pl.Buffered(1))


def _dot(a, b):
    return jnp.dot(a, b, preferred_element_type=F32)


def _dot_nt(a, b):
    return lax.dot_general(a, b, (((1,), (1,)), ((), ())), preferred_element_type=F32)


def _split(a):
    hi = a.astype(BF16)
    return hi, (a - hi.astype(F32)).astype(BF16)


def _dot3(a, b, dot=_dot):
    ah, al = _split(a)
    bh, bl = _split(b)
    return dot(ah, bh) + (dot(ah, bl) + dot(al, bh))


def _sigmoid(x):
    return 0.5 + 0.5 * jnp.tanh(0.5 * x)


def _gelu_tanh(x):
    return 0.5 * x * (1.0 + jnp.tanh(math.sqrt(2.0 / math.pi) * (x + 0.044715 * (x * x * x))))


def _softplus(x):
    return jnp.maximum(x, 0.0) + jnp.log1p(jnp.exp(-jnp.abs(x)))


def _modulate(x, g, mod, d):
    ms = jnp.mean(x * x, axis=-1, keepdims=True)
    y = x * lax.rsqrt(ms + NORM_EPS) * g
    return y * (1.0 + mod[:, d:2 * d]) + mod[:, :d]


def _adaln_kernel(c_ref, w_ref, b_ref, o_ref):
    c = c_ref[...]
    o_ref[0] = _dot3(c * _sigmoid(c), w_ref[0]) + b_ref[0]


def _adaln(c, w, b):
    r, d = c.shape
    m, _, n = w.shape
    tn = 1024
    return pl.pallas_call(
        _adaln_kernel,
        grid=(m, n // tn),
        in_specs=[
            pl.BlockSpec((r, d), lambda i, j: (0, 0)),
            pl.BlockSpec((1, d, tn), lambda i, j: (i, 0, j)),
            pl.BlockSpec((1, 1, tn), lambda i, j: (i, 0, j)),
        ],
        out_specs=pl.BlockSpec((1, r, tn), lambda i, j: (i, 0, j)),
        out_shape=jax.ShapeDtypeStruct((m, r, n), F32),
        compiler_params=_params("parallel", "parallel"),
        name="adaln",
    )(c, w, b)


def _mod_spec(mod, tm, tiles_per_seq):
    nb, rows, w = mod.shape
    if rows == 1:
        return pl.BlockSpec((1, 1, w), lambda i: (i // tiles_per_seq, 0, 0))
    return pl.BlockSpec((1, tm, w), lambda i: (0, i, 0))


def _rec_in_kernel(x_ref, mod_ref, g_ref, w_ref, ga_ref, xb_ref, *, d, dr):
    h = _modulate(x_ref[...], g_ref[...], mod_ref[0], d).astype(BF16)
    y = _dot(h, w_ref[...])
    ga_ref[...] = _gelu_tanh(y[:, :dr])
    xb_ref[...] = y[:, dr:]


def _rec_in(x, mod, g, w, tm, tiles_per_seq):
    t, d = x.shape
    dr = w.shape[1] // 2
    row = pl.BlockSpec((tm, dr), lambda i: (i, 0))
    return pl.pallas_call(
        functools.partial(_rec_in_kernel, d=d, dr=dr),
        grid=(t // tm,),
        in_specs=[
            pl.BlockSpec((tm, d), lambda i: (i, 0)),
            _mod_spec(mod, tm, tiles_per_seq),
            _resident((1, d)),
            _resident(w.shape),
        ],
        out_specs=[row, row],
        out_shape=[jax.ShapeDtypeStruct((t, dr), F32)] * 2,
        compiler_params=_params("parallel"),
        name="rec_in",
    )(x, mod, g, w)


def _head_proj_kernel(x_ref, mod_ref, g_ref, w_ref, gh_ref, cos_ref, sa_ref, sb_ref, *out_refs,
                      d, hd, n_rope, n_plain, half):
    h = _modulate(x_ref[...], g_ref[...], mod_ref[0], d).astype(BF16)
    y = _dot(h, w_ref[...])
    cos, sa, sb = cos_ref[...], sa_ref[...], sb_ref[...]
    gh = gh_ref[...]
    for i in range(n_rope):
        yh = y[:, i * hd:(i + 1) * hd]
        yh = yh * lax.rsqrt(jnp.mean(yh * yh, axis=-1, keepdims=True) + NORM_EPS) * gh
        out_refs[0][0, i] = (yh * cos + pltpu.roll(yh, hd - half, 1) * sa
                             + pltpu.roll(yh, half, 1) * sb)
    for i in range(n_plain):
        out_refs[1][0, i] = y[:, (n_rope + i) * hd:(n_rope + i + 1) * hd]


def _head_proj(x, mod, g, w, gh, tables, nb, seq, tm, n_rope, n_plain, hd):
    t, d = x.shape
    tps = seq // tm
    tab = pl.BlockSpec((tm, hd), lambda i: (i % tps, 0))
    out_specs = [pl.BlockSpec((1, n_rope, tm, hd), lambda i: (i // tps, 0, i % tps, 0))]
    out_shape = [jax.ShapeDtypeStruct((nb, n_rope, seq, hd), F32)]
    if n_plain:
        out_specs.append(pl.BlockSpec((1, n_plain, tm, hd), lambda i: (i // tps, 0, i % tps, 0)))
        out_shape.append(jax.ShapeDtypeStruct((nb, n_plain, seq, hd), F32))
    return pl.pallas_call(
        functools.partial(_head_proj_kernel, d=d, hd=hd, n_rope=n_rope, n_plain=n_plain,
                          half=hd // 8),
        grid=(t // tm,),
        in_specs=[
            pl.BlockSpec((tm, d), lambda i: (i, 0)),
            _mod_spec(mod, tm, tps),
            _resident((1, d)),
            _resident(w.shape),
            _resident((1, hd)),
            tab, tab, tab,
        ],
        out_specs=out_specs,
        out_shape=out_shape,
        compiler_params=_params("parallel"),
        name="head_proj",
    )(x, mod, g, w, gh, *tables)


def _out_res_kernel(x_ref, z_ref, mod_ref, w_ref, o_ref, *, d):
    gate = mod_ref[0][:, 2 * d:]
    o_ref[...] = x_ref[...] + gate * _dot(z_ref[...].astype(BF16), w_ref[...])


def _out_res(x, z, mod, w, tm, tiles_per_seq):
    t, d = x.shape
    k = z.shape[1]
    return pl.pallas_call(
        functools.partial(_out_res_kernel, d=d),
        grid=(t // tm,),
        in_specs=[
            pl.BlockSpec((tm, d), lambda i: (i, 0)),
            pl.BlockSpec((tm, k), lambda i: (i, 0)),
            _mod_spec(mod, tm, tiles_per_seq),
            _resident(w.shape),
        ],
        out_specs=pl.BlockSpec((tm, d), lambda i: (i, 0)),
        out_shape=jax.ShapeDtypeStruct((t, d), F32),
        compiler_params=_params("parallel"),
        name="out_res",
    )(x, z, mod, w)


def _mlp_kernel(x_ref, mod_ref, g_ref, wu_ref, wd_ref, o_ref, h_ref, acc_ref, *, d, n_chunks):
    x = x_ref[...]
    mod = mod_ref[0]
    h_ref[...] = _modulate(x, g_ref[...], mod, d).astype(BF16)
    for c in range(n_chunks):
        u = _dot(h_ref[...], wu_ref[c])
        a = jnp.square(jnp.maximum(u, 0.0)).astype(BF16)
        part = _dot(a, wd_ref[c])
        if c == 0:
            acc_ref[...] = part
        else:
            acc_ref[...] += part
    o_ref[...] = x + mod[:, 2 * d:] * acc_ref[...]


def _mlp(x, mod, g, wu, wd, tm, tiles_per_seq):
    t, d = x.shape
    return pl.pallas_call(
        functools.partial(_mlp_kernel, d=d, n_chunks=wu.shape[0]),
        grid=(t // tm,),
        in_specs=[
            pl.BlockSpec((tm, d), lambda i: (i, 0)),
            _mod_spec(mod, tm, tiles_per_seq),
            _resident((1, d)),
            _resident(wu.shape),
            _resident(wd.shape),
        ],
        out_specs=pl.BlockSpec((tm, d), lambda i: (i, 0)),
        out_shape=jax.ShapeDtypeStruct((t, d), F32),
        scratch_shapes=[pltpu.VMEM((tm, d), BF16), pltpu.VMEM((tm, d), F32)],
        compiler_params=_params("parallel"),
        name="mlp",
    )(x, mod, g, wu, wd)


def _lru_inputs(xc, wr_ref, wi_ref, bg_ref, sp):
    xh = xc.astype(BF16)
    gw = wr_ref.shape[1]
    r = jnp.concatenate([_dot(xh[:, j * gw:(j + 1) * gw], wr_ref[j])
                         for j in range(wr_ref.shape[0])], axis=1) + bg_ref[0:1, :]
    i = jnp.concatenate([_dot(xh[:, j * gw:(j + 1) * gw], wi_ref[j])
                         for j in range(wi_ref.shape[0])], axis=1) + bg_ref[1:2, :]
    log_a = -LRU_C * _sigmoid(r) * sp
    t = jnp.tanh(0.5 * log_a)
    rc = 1.0 / (1.0 - t)
    a = (1.0 + t) * rc
    u = (2.0 * rc) * jnp.sqrt(-t) * (_sigmoid(i) * xc)
    return a, u


def _scan_prompt_kernel(xb_ref, ga_ref, st_ref, h0_ref, cw_ref, cb_ref, wr_ref, wi_ref, bg_ref,
                        lam_ref, z_ref, tail_out_ref, hl_ref, tail_ref, hc_ref, a_ref, u_ref,
                        *, tt):
    ti = pl.program_id(1)
    dr = xb_ref.shape[1]

    @pl.when(ti == 0)
    def _():
        tail_ref[...] = st_ref[0]
        hc_ref[...] = h0_ref[0]

    x = xb_ref[...]
    tail = tail_ref[...]
    row8 = lax.broadcasted_iota(jnp.int32, (SUBLANES, dr), 0)
    xc = cb_ref[...] + cw_ref[CONV_WIDTH - 1:CONV_WIDTH, :] * x
    for k in range(1, CONV_WIDTH):
        xs = pltpu.roll(x, k, 0)
        first = jnp.where(row8 < k, pltpu.roll(tail, k, 0), xs[:SUBLANES])
        xs = jnp.concatenate([first, xs[SUBLANES:]], axis=0)
        xc = xc + cw_ref[CONV_WIDTH - 1 - k:CONV_WIDTH - k, :] * xs
    tail_ref[...] = x[tt - SUBLANES:]
    tail_out_ref[0] = x[tt - SUBLANES:]

    a, u = _lru_inputs(xc, wr_ref, wi_ref, bg_ref, _softplus(-lam_ref[...]))
    a_ref[...] = a
    u_ref[...] = u

    def group(gi, carry):
        r0 = pl.multiple_of(gi * SUBLANES, SUBLANES)
        a8 = a_ref[pl.ds(r0, SUBLANES), :]
        u8 = u_ref[pl.ds(r0, SUBLANES), :]
        for s in (1, 2, 4):
            keep = row8 >= s
            u8 = jnp.where(keep, a8 * pltpu.roll(u8, s, 0) + u8, u8)
            a8 = jnp.where(keep, a8 * pltpu.roll(a8, s, 0), a8)
        h8 = a8 * carry + u8
        z_ref[pl.ds(r0, SUBLANES), :] = h8 * ga_ref[pl.ds(r0, SUBLANES), :]
        return jnp.broadcast_to(h8[SUBLANES - 1:SUBLANES, :], (SUBLANES, dr))

    carry = lax.fori_loop(0, tt // SUBLANES, group,
                          jnp.broadcast_to(hc_ref[...], (SUBLANES, dr)), unroll=2)
    hc_ref[...] = carry[0:1, :]
    hl_ref[0] = carry[0:1, :]


def _scan_prompt(xb, ga, st8, h0, cw, cb, wr, wi, bg, lam, nb, seq, tt):
    t, dr = xb.shape
    nt = seq // tt
    row = pl.BlockSpec((tt, dr), lambda b, i: (b * nt + i, 0))
    per_seq8 = pl.BlockSpec((1, SUBLANES, dr), lambda b, i: (b, 0, 0))
    per_seq1 = pl.BlockSpec((1, 1, dr), lambda b, i: (b, 0, 0))
    return pl.pallas_call(
        functools.partial(_scan_prompt_kernel, tt=tt),
        grid=(nb, nt),
        in_specs=[row, row, per_seq8, per_seq1, _resident(cw.shape), _resident(cb.shape),
                  _resident(wr.shape), _resident(wi.shape), _resident(bg.shape),
                  _resident(lam.shape)],
        out_specs=[row, per_seq8, per_seq1],
        out_shape=[jax.ShapeDtypeStruct((t, dr), F32),
                   jax.ShapeDtypeStruct((nb, SUBLANES, dr), F32),
                   jax.ShapeDtypeStruct((nb, 1, dr), F32)],
        scratch_shapes=[pltpu.VMEM((SUBLANES, dr), F32), pltpu.VMEM((1, dr), F32),
                        pltpu.VMEM((tt, dr), F32), pltpu.VMEM((tt, dr), F32)],
        compiler_params=_params("parallel", "arbitrary"),
        name="scan_prompt",
    )(xb, ga, st8, h0, cw, cb, wr, wi, bg, lam)


def _scan_steps_kernel(xb_ref, ga_ref, st_ref, h0_ref, cw_ref, cb_ref, wr_ref, wi_ref, bg_ref,
                       lam_ref, z_ref, hl_ref, *, steps):
    rows = [st_ref[j] for j in range(CONV_WIDTH - 1)] + [xb_ref[t] for t in range(steps)]
    sp = _softplus(-lam_ref[...])
    h = h0_ref[...]
    for t in range(steps):
        xc = cb_ref[...]
        for j in range(CONV_WIDTH):
            xc = xc + cw_ref[j:j + 1, :] * rows[t + j]
        a, u = _lru_inputs(xc, wr_ref, wi_ref, bg_ref, sp)
        h = a * h + u
        z_ref[t] = h * ga_ref[t]
    hl_ref[...] = h


def _scan_steps(xb, ga, st, h0, cw, cb, wr, wi, bg, lam):
    steps, nb, dr = xb.shape
    args = (xb, ga, st, h0, cw, cb, wr, wi, bg, lam)
    return pl.pallas_call(
        functools.partial(_scan_steps_kernel, steps=steps),
        grid=(1,),
        in_specs=[_resident(a.shape) for a in args],
        out_specs=[pl.BlockSpec((steps, nb, dr), lambda i: (0, 0, 0)),
                   pl.BlockSpec((nb, dr), lambda i: (0, 0))],
        out_shape=[jax.ShapeDtypeStruct((steps, nb, dr), F32),
                   jax.ShapeDtypeStruct((nb, dr), F32)],
        compiler_params=_params("arbitrary"),
        name="scan_steps",
    )(*args)


def _block_bias_t(gate_t, cur):
    row = lax.broadcasted_iota(jnp.int32, gate_t.shape, 0)
    row_f = row.astype(F32)
    g = jnp.where(row < cur, gate_t, NEG)
    bias = jnp.where(row == cur, 0.0, NEG)
    for _ in range(min(MOBA_TOP_K, cur)):
        m = jnp.max(g, axis=0, keepdims=True)
        idx = jnp.min(jnp.where(g == m, row_f, float(LANES)), axis=0, keepdims=True)
        pick = row_f == idx
        bias = jnp.where(pick, 0.0, bias)
        g = jnp.where(pick, NEG, g)
    return bias


def _attn_prompt_kernel(q_ref, k_ref, v_ref, o_ref, kaug_ref, vt_ref, means_ref, *, seq, hd, scale):
    qi = pl.program_id(2)
    bq = MOBA_BLOCK
    nblk = seq // bq
    nbp = means_ref.shape[0]

    @pl.when(qi == 0)
    def _():
        lane = lax.broadcasted_iota(jnp.int32, (bq, hd), 1)
        means_ref[...] = jnp.zeros(means_ref.shape, F32)
        for n in range(nblk):
            kb = k_ref[0, 0, n * bq:(n + 1) * bq, :]
            means_ref[n:n + 1, :] = jnp.sum(kb, axis=0, keepdims=True) * (1.0 / MOBA_BLOCK)
            onehot = jnp.where(lane == n, 1.0, 0.0).astype(BF16)
            kaug_ref[n * bq:(n + 1) * bq, :] = jnp.concatenate([kb.astype(BF16), onehot], axis=1)
            vt_ref[:, n * bq:(n + 1) * bq] = v_ref[0, 0, n * bq:(n + 1) * bq, :].T.astype(BF16)

    qt = q_ref[0, 0].T
    gate_t = _dot3(means_ref[...], qt)
    causal = (lax.broadcasted_iota(jnp.int32, (bq, bq), 0)
              <= lax.broadcasted_iota(jnp.int32, (bq, bq), 1))

    for c in range(nblk):
        @pl.when(qi == c)
        def _(c=c):
            qt_aug = jnp.concatenate(
                [qt * scale, _block_bias_t(gate_t, c), jnp.zeros((hd - nbp, bq), F32)],
                axis=0).astype(BF16)
            s = _dot(kaug_ref[0:(c + 1) * bq, :], qt_aug)
            s_own = jnp.where(causal, s[c * bq:], NEG)
            m = jnp.max(s_own, axis=0, keepdims=True)
            if c:
                m = jnp.maximum(m, jnp.max(s[:c * bq], axis=0, keepdims=True))
            p = jnp.exp(s_own - m)
            if c:
                p = jnp.concatenate([jnp.exp(s[:c * bq] - m), p], axis=0)
            l = jnp.sum(p, axis=0, keepdims=True)
            ot = _dot(vt_ref[:, 0:(c + 1) * bq], p.astype(BF16))
            o_ref[0] = (ot * (1.0 / l)).T


def _attn_prompt(q, k, v):
    b, h, seq, hd = q.shape
    nblk = seq // MOBA_BLOCK
    assert seq % MOBA_BLOCK == 0 and nblk <= hd and hd == LANES
    nbp = -(-nblk // SUBLANES) * SUBLANES
    full = pl.BlockSpec((1, 1, seq, hd), lambda bi, hi, qi: (bi, hi, 0, 0))
    return pl.pallas_call(
        functools.partial(_attn_prompt_kernel, seq=seq, hd=hd, scale=hd ** -0.5),
        grid=(b, h, nblk),
        in_specs=[pl.BlockSpec((1, 1, MOBA_BLOCK, hd), lambda bi, hi, qi: (bi, hi, qi, 0)),
                  full, full],
        out_specs=pl.BlockSpec((1, MOBA_BLOCK, hd), lambda bi, hi, qi: (bi, qi, hi)),
        out_shape=jax.ShapeDtypeStruct((b, seq, h * hd), F32),
        scratch_shapes=[pltpu.VMEM((seq, 2 * hd), BF16), pltpu.VMEM((hd, seq), BF16),
                        pltpu.VMEM((nbp, hd), F32)],
        compiler_params=_params("parallel", "parallel", "arbitrary"),
        name="attn_prompt",
    )(q, k, v)


def _page_means_kernel(pt_ref, *refs, n):
    k_refs, o_ref = refs[:n], refs[n]
    n_heads, _, hd = k_refs[0].shape[1:]
    for h in range(n_heads):
        sums = [jnp.sum(k_refs[j][0, h], axis=0, keepdims=True) for j in range(n)]
        o_ref[0, :, h * hd:(h + 1) * hd] = jnp.concatenate(
            [(sums[2 * i] + sums[2 * i + 1]) * (1.0 / MOBA_BLOCK) for i in range(n // 2)], axis=0)


def _page_means(cache_k, page_table):
    db, n_pages = page_table.shape
    _, h, _, hd = cache_k.shape
    n = min(16, n_pages)

    def page(j):
        return pl.BlockSpec((1, h, PAGE_SIZE, hd), lambda b, g, pt: (pt[b, g * n + j], 0, 0, 0))

    return pl.pallas_call(
        functools.partial(_page_means_kernel, n=n),
        grid_spec=pltpu.PrefetchScalarGridSpec(
            num_scalar_prefetch=1,
            grid=(db, n_pages // n),
            in_specs=[page(j) for j in range(n)],
            out_specs=pl.BlockSpec((1, n // 2, h * hd), lambda b, g, pt: (b, g, 0)),
        ),
        out_shape=jax.ShapeDtypeStruct((db, n_pages // 2, h * hd), F32),
        compiler_params=_params("parallel", "arbitrary"),
        name="page_means",
    )(page_table, *([cache_k] * n))


def _select_kernel(q_ref, means_ref, idx_ref):
    n_blocks = means_ref.shape[1]
    means = jnp.concatenate(
        [means_ref[0], jnp.zeros((LANES - n_blocks, means_ref.shape[2]), F32)], axis=0)
    gate = _dot3(q_ref[0], means, _dot_nt)
    lane = lax.broadcasted_iota(jnp.int32, gate.shape, 1)
    lane_f = lane.astype(F32)
    g = jnp.where(lane < n_blocks, gate, NEG)
    out = jnp.zeros(gate.shape, F32)
    for j in range(MOBA_TOP_K):
        m = jnp.max(g, axis=1, keepdims=True)
        idx = jnp.min(jnp.where(g == m, lane_f, float(LANES)), axis=1, keepdims=True)
        out = jnp.where(lane == j, idx, out)
        g = jnp.where(lane_f == idx, NEG, g)
    idx_ref[0] = out.astype(jnp.int32)


def _select(q_bd, means):
    db, r, w = q_bd.shape
    nb = means.shape[1]
    assert MOBA_TOP_K <= nb <= LANES
    return pl.pallas_call(
        _select_kernel,
        grid=(db,),
        in_specs=[pl.BlockSpec((1, r, w), lambda b: (b, 0, 0)),
                  pl.BlockSpec((1, nb, w), lambda b: (b, 0, 0))],
        out_specs=pl.BlockSpec((1, r, LANES), lambda b: (b, 0, 0)),
        out_shape=jax.ShapeDtypeStruct((db, r, LANES), jnp.int32),
        compiler_params=_params("parallel"),
        name="select_blocks",
    )(q_bd, means)


def _attn_paged_kernel(sel_ref, q_ref, kn_ref, vn_ref, *refs, n_slots, sq, scale):
    k_refs, v_refs, o_ref = refs[:n_slots], refs[n_slots:2 * n_slots], refs[2 * n_slots]
    q = q_ref[0, 0]
    kn, vn = kn_ref[0, 0], vn_ref[0, 0]
    k_all = jnp.concatenate([r[0, 0] for r in k_refs], axis=0).astype(BF16)
    v_all = jnp.concatenate([r[0, 0] for r in v_refs], axis=0).astype(BF16)
    s = _dot_nt((q * scale).astype(BF16), k_all)
    keys_per_q = MOBA_TOP_K * MOBA_BLOCK
    row = lax.broadcasted_iota(jnp.int32, s.shape, 0)
    col = lax.broadcasted_iota(jnp.int32, s.shape, 1)
    s = jnp.where((col >= row * keys_per_q) & (col < (row + 1) * keys_per_q), s, NEG)
    row1 = lax.broadcasted_iota(jnp.int32, (q.shape[0], 1), 0)
    s_new = [jnp.where(row1 >= j, jnp.sum(q * kn[j:j + 1, :], axis=1, keepdims=True) * scale, NEG)
             for j in range(sq)]
    m = functools.reduce(jnp.maximum, s_new + [jnp.max(s, axis=1, keepdims=True)])
    p = jnp.exp(s - m)
    p_new = [jnp.exp(sj - m) for sj in s_new]
    l = functools.reduce(lambda x, y: x + y, p_new + [jnp.sum(p, axis=1, keepdims=True)])
    acc = functools.reduce(lambda x, y: x + y,
                           [p_new[j] * vn[j:j + 1, :] for j in range(sq)]
                           + [_dot(p.astype(BF16), v_all)])
    o_ref[0, 0] = acc / l


def _attn_paged(q, k_new, v_new, cache_k, cache_v, sel, sq):
    db, h, rows, hd = q.shape
    n_slots = sq * MOBA_TOP_K * PAGES_PER_BLOCK
    per_head = pl.BlockSpec((1, 1, rows, hd), lambda b, hi, sel: (b, hi, 0, 0))

    def page(s):
        return pl.BlockSpec((1, 1, PAGE_SIZE, hd),
                            lambda b, hi, sel: (sel[b, hi * n_slots + s], hi, 0, 0))

    return pl.pallas_call(
        functools.partial(_attn_paged_kernel, n_slots=n_slots, sq=sq, scale=hd ** -0.5),
        grid_spec=pltpu.PrefetchScalarGridSpec(
            num_scalar_prefetch=1,
            grid=(db, h),
            in_specs=[per_head, per_head, per_head]
            + [page(s) for s in range(n_slots)] + [page(s) for s in range(n_slots)],
            out_specs=per_head,
        ),
        out_shape=jax.ShapeDtypeStruct((db, h, rows, hd), F32),
        compiler_params=_params("parallel", "parallel"),
        name="attn_paged",
    )(sel, q, k_new, v_new, *([cache_k] * n_slots), *([cache_v] * n_slots))


def _rope_tables(pos, hd):
    half = hd // 8
    inv = ROPE_THETA ** (-jnp.arange(half, dtype=F32) * 2.0 / (2 * half))
    ang = pos.astype(F32)[:, None] * inv[None, :]
    cos, sin = jnp.cos(ang), jnp.sin(ang)
    n = pos.shape[0]
    ones = jnp.ones((n, hd - 2 * half), F32)
    zeros = jnp.zeros((n, hd - 2 * half), F32)
    z = jnp.zeros_like(sin)
    return (jnp.concatenate([cos, cos, ones], axis=1),
            jnp.concatenate([-sin, z, zeros], axis=1),
            jnp.concatenate([z, sin, zeros], axis=1))


def _block_diag_gate(w_gate):
    n, k, _ = w_gate.shape
    per = GATE_GROUP // k
    eye = jnp.eye(per, dtype=w_gate.dtype)

    def build(w):
        w = w.reshape(n // per, per, k, k)
        return jnp.einsum("gmkj,mn->gmknj", w, eye).reshape(n // per, GATE_GROUP, GATE_GROUP).astype(BF16)

    return build(w_gate[:, :, :k]), build(w_gate[:, :, k:])


def kernel(x_prompt, x_sample, c_prompt, c_sample, state_conv, state_rglru, cache_k, cache_v, page_table, g_norm, w_ada, b_ada, w_rec_in, conv_w, conv_b, w_gate, b_gate, lam, w_rec_out, g_kv, w_ada_kv, b_ada_kv, w_kv, g_k, w_q, g_q, w_o, w_up, w_down):
    bp, sp, d = x_prompt.shape
    db, sq, _ = x_sample.shape
    depth = g_norm.shape[0]
    n_a = w_rec_in.shape[0]
    dr = w_rec_in.shape[2] // 2
    hd = g_k.shape[0]
    n_heads = w_q.shape[2] // hd
    n_pages = page_table.shape[1]
    d_ff = w_up.shape[2]
    assert hd == LANES and n_heads == N_HEADS and sq <= SUBLANES and n_pages % PAGES_PER_BLOCK == 0

    c_all = jnp.concatenate([c_prompt, c_sample], axis=0)
    pad = (-c_all.shape[0]) % SUBLANES
    c_all = jnp.pad(c_all, ((0, pad), (0, 0)))
    mods = _adaln(c_all, w_ada.reshape(depth * 2, d, 3 * d), b_ada.reshape(depth * 2, 1, 3 * d))
    mods_kv = _adaln(c_all, w_ada_kv[None], b_ada_kv[None, None])[0]

    def group_mods(m, prompt):
        if prompt:
            return m[:bp, None, :]
        return jnp.repeat(m[bp:bp + db], sq, axis=0)[None]

    w_in16 = w_rec_in.astype(BF16)
    w_out16 = w_rec_out.astype(BF16)
    w_kv16, w_q16, w_o16 = w_kv.astype(BF16), w_q.astype(BF16), w_o.astype(BF16)
    n_chunks = d_ff // 1024
    w_up16 = w_up.astype(BF16).reshape(depth, d, n_chunks, d_ff // n_chunks).transpose(0, 2, 1, 3)
    w_down16 = w_down.astype(BF16).reshape(depth, n_chunks, d_ff // n_chunks, d)
    gates = [_block_diag_gate(w_gate[l]) for l in range(n_a)]

    def run_group(x3, prompt):
        nb, seq, _ = x3.shape
        t = nb * seq
        x = x3.reshape(t, d)
        tm = 512 if prompt else t
        tps = seq // tm if prompt else 1
        if prompt:
            pos = jnp.arange(seq, dtype=jnp.int32)
            tables = _rope_tables(pos, hd)
            tab_nb, tab_seq = nb, seq
        else:
            pos = n_pages * PAGE_SIZE + jnp.arange(seq, dtype=jnp.int32)
            tables = tuple(jnp.tile(tb, (nb, 1)) for tb in _rope_tables(pos, hd))
            tab_nb, tab_seq = 1, t

        def pad_rows(a):
            a = a.reshape(n_heads, nb, seq, hd).transpose(1, 0, 2, 3)
            return jnp.pad(a, ((0, 0), (0, 0), (0, SUBLANES - seq), (0, 0)))

        conv_new, h_new = [], []
        k = v = means = k8 = v8 = None
        for l in range(depth):
            mod = group_mods(mods[2 * l], prompt)
            g1 = g_norm[l, 0][None]
            if l == n_a:
                k, v = _head_proj(x, group_mods(mods_kv, prompt), g_kv[None], w_kv16, g_k[None],
                                  tables, tab_nb, tab_seq, tm, n_heads, n_heads, hd)
                if not prompt:
                    means = _page_means(cache_k, page_table)
                    k8, v8 = pad_rows(k), pad_rows(v)
            if l < n_a:
                ga, xb = _rec_in(x, mod, g1, w_in16[l], tm, tps)
                wr, wi = gates[l]
                cw, cb, bg, lm = conv_w[l], conv_b[l][None], b_gate[l], lam[l][None]
                if prompt:
                    st8 = jnp.zeros((nb, SUBLANES, dr), F32)
                    h0 = jnp.zeros((nb, 1, dr), F32)
                    z, tail, h_last = _scan_prompt(xb, ga, st8, h0, cw, cb, wr, wi, bg, lm,
                                                   nb, seq, 256)
                    conv_new.append(tail[:, SUBLANES - (CONV_WIDTH - 1):])
                    h_new.append(h_last[:, 0])
                else:
                    xb_t = xb.reshape(nb, seq, dr).transpose(1, 0, 2)
                    ga_t = ga.reshape(nb, seq, dr).transpose(1, 0, 2)
                    st = state_conv[l].transpose(1, 0, 2)
                    z_t, h_last = _scan_steps(xb_t, ga_t, st, state_rglru[l], cw, cb, wr, wi, bg, lm)
                    z = z_t.transpose(1, 0, 2).reshape(t, dr)
                    hist = jnp.concatenate([state_conv[l], xb.reshape(nb, seq, dr)], axis=1)
                    conv_new.append(hist[:, -(CONV_WIDTH - 1):])
                    h_new.append(h_last)
                x = _out_res(x, z, mod, w_out16[l], tm, tps)
            else:
                j = l - n_a
                (q,) = _head_proj(x, mod, g1, w_q16[j], g_q[j][None], tables, tab_nb, tab_seq, tm,
                                  n_heads, 0, hd)
                if prompt:
                    o = _attn_prompt(q, k, v).reshape(t, n_heads * hd)
                else:
                    q8 = pad_rows(q)
                    q_bd = jnp.einsum("bhqd,hg->bhqgd", q8, jnp.eye(n_heads, dtype=F32))
                    picks = _select(q_bd.reshape(nb, n_heads * SUBLANES, n_heads * hd), means)
                    picks = picks.reshape(nb, n_heads, SUBLANES, LANES)[:, :, :seq, :MOBA_TOP_K]
                    pages = (picks[..., None] * PAGES_PER_BLOCK
                             + jnp.arange(PAGES_PER_BLOCK, dtype=jnp.int32))
                    sel = jnp.take_along_axis(page_table, pages.reshape(nb, -1), axis=1)
                    o = _attn_paged(q8, k8, v8, cache_k, cache_v, sel, seq)
                    o = o[:, :, :seq].transpose(0, 2, 1, 3).reshape(t, n_heads * hd)
                x = _out_res(x, o, mod, w_o16[j], tm, tps)
            mod2 = group_mods(mods[2 * l + 1], prompt)
            x = _mlp(x, mod2, g_norm[l, 1][None], w_up16[l], w_down16[l], tm, tps)
        if not prompt:
            k = k.reshape(n_heads, nb, seq, hd).transpose(1, 0, 2, 3)
            v = v.reshape(n_heads, nb, seq, hd).transpose(1, 0, 2, 3)
        return x.reshape(nb, seq, d), jnp.stack(conv_new), jnp.stack(h_new), k, v

    y_p, conv_p, h_p, k_p, v_p = run_group(x_prompt, True)
    y_s, conv_s, h_s, k_s, v_s = run_group(x_sample, False)
    return (y_p, y_s, conv_p, h_p, conv_s, h_s, k_p, v_p, k_s, v_s)
```

```python
import functools
import math

import jax
import jax.numpy as jnp
from jax import lax
from jax.experimental import pallas as pl
from jax.experimental.pallas import tpu as pltpu

F32 = jnp.float32
BF16 = jnp.bfloat16

NORM_EPS = 1e-6
LRU_C = 8.0
CONV_WIDTH = 4
PAGE_SIZE = 128
MOBA_BLOCK = 256
MOBA_TOP_K = 3
PAGES_PER_BLOCK = MOBA_BLOCK // PAGE_SIZE
N_HEADS = 8
ROPE_THETA = 500000.0
NEG = -1e30
LANES = 128
SUBLANES = 8
GATE_GROUP = 256
VMEM_LIMIT = 56 * 1024 * 1024


def _params(*sem):
    return pltpu.CompilerParams(dimension_semantics=sem, vmem_limit_bytes=VMEM_LIMIT)


def _resident(shape):
    nd = len(shape)
    return pl.BlockSpec(shape, lambda *_: (0,) * nd, pipeline_mode=pl.Buffered(1))


def _dot(a, b):
    return jnp.dot(a, b, preferred_element_type=F32)


def _dot_nt(a, b):
    return lax.dot_general(a, b, (((1,), (1,)), ((), ())), preferred_element_type=F32)


def _split(a):
    hi = a.astype(BF16)
    return hi, (a - hi.astype(F32)).astype(BF16)


def _dot3(a, b, dot=_dot):
    ah, al = _split(a)
    bh, bl = _split(b)
    return dot(ah, bh) + (dot(ah, bl) + dot(al, bh))


def _sigmoid(x):
    return 0.5 + 0.5 * jnp.tanh(0.5 * x)


def _gelu_tanh(x):
    return 0.5 * x * (1.0 + jnp.tanh(math.sqrt(2.0 / math.pi) * (x + 0.044715 * (x * x * x))))


def _softplus(x):
    return jnp.maximum(x, 0.0) + jnp.log1p(jnp.exp(-jnp.abs(x)))


def _modulate(x, g, mod, d):
    ms = jnp.mean(x * x, axis=-1, keepdims=True)
    y = x * lax.rsqrt(ms + NORM_EPS) * g
    return y * (1.0 + mod[:, d:2 * d]) + mod[:, :d]


def _adaln_kernel(c_ref, w_ref, b_ref, o_ref):
    c = c_ref[...]
    o_ref[0] = _dot3(c * _sigmoid(c), w_ref[0]) + b_ref[0]


def _adaln(c, w, b):
    r, d = c.shape
    m, _, n = w.shape
    tn = 1024
    return pl.pallas_call(
        _adaln_kernel,
        grid=(m, n // tn),
        in_specs=[
            pl.BlockSpec((r, d), lambda i, j: (0, 0)),
            pl.BlockSpec((1, d, tn), lambda i, j: (i, 0, j)),
            pl.BlockSpec((1, 1, tn), lambda i, j: (i, 0, j)),
        ],
        out_specs=pl.BlockSpec((1, r, tn), lambda i, j: (i, 0, j)),
        out_shape=jax.ShapeDtypeStruct((m, r, n), F32),
        compiler_params=_params("parallel", "parallel"),
        name="adaln",
    )(c, w, b)


def _mod_spec(mod, tm, tiles_per_seq):
    nb, rows, w = mod.shape
    if rows == 1:
        return pl.BlockSpec((1, 1, w), lambda i: (i // tiles_per_seq, 0, 0))
    return pl.BlockSpec((1, tm, w), lambda i: (0, i, 0))


def _rec_in_kernel(x_ref, mod_ref, g_ref, w_ref, ga_ref, xb_ref, *, d, dr):
    h = _modulate(x_ref[...], g_ref[...], mod_ref[0], d).astype(BF16)
    y = _dot(h, w_ref[...])
    ga_ref[...] = _gelu_tanh(y[:, :dr])
    xb_ref[...] = y[:, dr:]


def _rec_in(x, mod, g, w, tm, tiles_per_seq):
    t, d = x.shape
    dr = w.shape[1] // 2
    row = pl.BlockSpec((tm, dr), lambda i: (i, 0))
    return pl.pallas_call(
        functools.partial(_rec_in_kernel, d=d, dr=dr),
        grid=(t // tm,),
        in_specs=[
            pl.BlockSpec((tm, d), lambda i: (i, 0)),
            _mod_spec(mod, tm, tiles_per_seq),
            _resident((1, d)),
            _resident(w.shape),
        ],
        out_specs=[row, row],
        out_shape=[jax.ShapeDtypeStruct((t, dr), F32)] * 2,
        compiler_params=_params("parallel"),
        name="rec_in",
    )(x, mod, g, w)


def _head_proj_kernel(x_ref, mod_ref, g_ref, w_ref, gh_ref, cos_ref, sa_ref, sb_ref, *out_refs,
                      d, hd, n_rope, n_plain, half):
    h = _modulate(x_ref[...], g_ref[...], mod_ref[0], d).astype(BF16)
    y = _dot(h, w_ref[...])
    cos, sa, sb = cos_ref[...], sa_ref[...], sb_ref[...]
    gh = gh_ref[...]
    for i in range(n_rope):
        yh = y[:, i * hd:(i + 1) * hd]
        yh = yh * lax.rsqrt(jnp.mean(yh * yh, axis=-1, keepdims=True) + NORM_EPS) * gh
        out_refs[0][0, i] = (yh * cos + pltpu.roll(yh, hd - half, 1) * sa
                             + pltpu.roll(yh, half, 1) * sb)
    for i in range(n_plain):
        out_refs[1][0, i] = y[:, (n_rope + i) * hd:(n_rope + i + 1) * hd]


def _head_proj(x, mod, g, w, gh, tables, nb, seq, tm, n_rope, n_plain, hd):
    t, d = x.shape
    tps = seq // tm
    tab = pl.BlockSpec((tm, hd), lambda i: (i % tps, 0))
    out_specs = [pl.BlockSpec((1, n_rope, tm, hd), lambda i: (i // tps, 0, i % tps, 0))]
    out_shape = [jax.ShapeDtypeStruct((nb, n_rope, seq, hd), F32)]
    if n_plain:
        out_specs.append(pl.BlockSpec((1, n_plain, tm, hd), lambda i: (i // tps, 0, i % tps, 0)))
        out_shape.append(jax.ShapeDtypeStruct((nb, n_plain, seq, hd), F32))
    return pl.pallas_call(
        functools.partial(_head_proj_kernel, d=d, hd=hd, n_rope=n_rope, n_plain=n_plain,
                          half=hd // 8),
        grid=(t // tm,),
        in_specs=[
            pl.BlockSpec((tm, d), lambda i: (i, 0)),
            _mod_spec(mod, tm, tps),
            _resident((1, d)),
            _resident(w.shape),
            _resident((1, hd)),
            tab, tab, tab,
        ],
        out_specs=out_specs,
        out_shape=out_shape,
        compiler_params=_params("parallel"),
        name="head_proj",
    )(x, mod, g, w, gh, *tables)


FF_CHUNK = 1024


def _mix_mlp_kernel(x_ref, z_ref, mod1_ref, wo_ref, mod2_ref, g_ref, wu_ref, wd_ref, o_ref,
                    x1_ref, h_ref, acc_ref, *, d):
    x1 = x_ref[...] + mod1_ref[0][:, 2 * d:] * _dot(z_ref[...].astype(BF16), wo_ref[...])
    x1_ref[...] = x1
    mod2 = mod2_ref[0]
    h_ref[...] = _modulate(x1, g_ref[...], mod2, d).astype(BF16)
    for c in range(wu_ref.shape[1] // FF_CHUNK):
        cols = slice(c * FF_CHUNK, (c + 1) * FF_CHUNK)
        u = _dot(h_ref[...], wu_ref[:, cols])
        a = jnp.square(jnp.maximum(u, 0.0)).astype(BF16)
        part = _dot(a, wd_ref[cols, :])
        if c == 0:
            acc_ref[...] = part
        else:
            acc_ref[...] += part
    o_ref[...] = x1_ref[...] + mod2[:, 2 * d:] * acc_ref[...]


def _mix_mlp(x, z, mod1, wo, mod2, g, wu, wd, tm, tiles_per_seq):
    t, d = x.shape
    assert wu.shape[1] % FF_CHUNK == 0
    row = pl.BlockSpec((tm, d), lambda i: (i, 0))
    return pl.pallas_call(
        functools.partial(_mix_mlp_kernel, d=d),
        grid=(t // tm,),
        in_specs=[
            row,
            pl.BlockSpec((tm, z.shape[1]), lambda i: (i, 0)),
            _mod_spec(mod1, tm, tiles_per_seq),
            _resident(wo.shape),
            _mod_spec(mod2, tm, tiles_per_seq),
            _resident((1, d)),
            _resident(wu.shape),
            _resident(wd.shape),
        ],
        out_specs=row,
        out_shape=jax.ShapeDtypeStruct((t, d), F32),
        scratch_shapes=[pltpu.VMEM((tm, d), F32), pltpu.VMEM((tm, d), BF16),
                        pltpu.VMEM((tm, d), F32)],
        compiler_params=_params("parallel"),
        name="mix_mlp",
    )(x, z, mod1, wo, mod2, g, wu, wd)


def _lru_inputs(xc, wr_ref, wi_ref, bg_ref, sp):
    xh = xc.astype(BF16)
    gw = wr_ref.shape[1]
    r = jnp.concatenate([_dot(xh[:, j * gw:(j + 1) * gw], wr_ref[j])
                         for j in range(wr_ref.shape[0])], axis=1) + bg_ref[0:1, :]
    i = jnp.concatenate([_dot(xh[:, j * gw:(j + 1) * gw], wi_ref[j])
                         for j in range(wi_ref.shape[0])], axis=1) + bg_ref[1:2, :]
    log_a = -LRU_C * _sigmoid(r) * sp
    t = jnp.tanh(0.5 * log_a)
    rc = 1.0 / (1.0 - t)
    a = (1.0 + t) * rc
    u = (2.0 * rc) * jnp.sqrt(-t) * (_sigmoid(i) * xc)
    return a, u


def _scan_prompt_kernel(xb_ref, ga_ref, st_ref, h0_ref, cw_ref, cb_ref, wr_ref, wi_ref, bg_ref,
                        lam_ref, z_ref, tail_out_ref, hl_ref, tail_ref, hc_ref, a_ref, u_ref,
                        *, tt):
    ti = pl.program_id(1)
    dr = xb_ref.shape[1]

    @pl.when(ti == 0)
    def _():
        tail_ref[...] = st_ref[0]
        hc_ref[...] = h0_ref[0]

    x = xb_ref[...]
    tail = tail_ref[...]
    row8 = lax.broadcasted_iota(jnp.int32, (SUBLANES, dr), 0)
    xc = cb_ref[...] + cw_ref[CONV_WIDTH - 1:CONV_WIDTH, :] * x
    for k in range(1, CONV_WIDTH):
        xs = pltpu.roll(x, k, 0)
        first = jnp.where(row8 < k, pltpu.roll(tail, k, 0), xs[:SUBLANES])
        xs = jnp.concatenate([first, xs[SUBLANES:]], axis=0)
        xc = xc + cw_ref[CONV_WIDTH - 1 - k:CONV_WIDTH - k, :] * xs
    tail_ref[...] = x[tt - SUBLANES:]
    tail_out_ref[0] = x[tt - SUBLANES:]

    a, u = _lru_inputs(xc, wr_ref, wi_ref, bg_ref, _softplus(-lam_ref[...]))
    a_ref[...] = a
    u_ref[...] = u

    def group(gi, carry):
        r0 = pl.multiple_of(gi * SUBLANES, SUBLANES)
        a8 = a_ref[pl.ds(r0, SUBLANES), :]
        u8 = u_ref[pl.ds(r0, SUBLANES), :]
        for s in (1, 2, 4):
            keep = row8 >= s
            u8 = jnp.where(keep, a8 * pltpu.roll(u8, s, 0) + u8, u8)
            a8 = jnp.where(keep, a8 * pltpu.roll(a8, s, 0), a8)
        h8 = a8 * carry + u8
        z_ref[pl.ds(r0, SUBLANES), :] = h8 * ga_ref[pl.ds(r0, SUBLANES), :]
        return jnp.broadcast_to(h8[SUBLANES - 1:SUBLANES, :], (SUBLANES, dr))

    carry = lax.fori_loop(0, tt // SUBLANES, group,
                          jnp.broadcast_to(hc_ref[...], (SUBLANES, dr)), unroll=2)
    hc_ref[...] = carry[0:1, :]
    hl_ref[0] = carry[0:1, :]


def _scan_prompt(xb, ga, st8, h0, cw, cb, wr, wi, bg, lam, nb, seq, tt):
    t, dr = xb.shape
    nt = seq // tt
    row = pl.BlockSpec((tt, dr), lambda b, i: (b * nt + i, 0))
    per_seq8 = pl.BlockSpec((1, SUBLANES, dr), lambda b, i: (b, 0, 0))
    per_seq1 = pl.BlockSpec((1, 1, dr), lambda b, i: (b, 0, 0))
    return pl.pallas_call(
        functools.partial(_scan_prompt_kernel, tt=tt),
        grid=(nb, nt),
        in_specs=[row, row, per_seq8, per_seq1, _resident(cw.shape), _resident(cb.shape),
                  _resident(wr.shape), _resident(wi.shape), _resident(bg.shape),
                  _resident(lam.shape)],
        out_specs=[row, per_seq8, per_seq1],
        out_shape=[jax.ShapeDtypeStruct((t, dr), F32),
                   jax.ShapeDtypeStruct((nb, SUBLANES, dr), F32),
                   jax.ShapeDtypeStruct((nb, 1, dr), F32)],
        scratch_shapes=[pltpu.VMEM((SUBLANES, dr), F32), pltpu.VMEM((1, dr), F32),
                        pltpu.VMEM((tt, dr), F32), pltpu.VMEM((tt, dr), F32)],
        compiler_params=_params("parallel", "arbitrary"),
        name="scan_prompt",
    )(xb, ga, st8, h0, cw, cb, wr, wi, bg, lam)


def _scan_steps_kernel(xb_ref, ga_ref, st_ref, h0_ref, cw_ref, cb_ref, wr_ref, wi_ref, bg_ref,
                       lam_ref, z_ref, hl_ref, *, steps):
    rows = [st_ref[j] for j in range(CONV_WIDTH - 1)] + [xb_ref[t] for t in range(steps)]
    sp = _softplus(-lam_ref[...])
    h = h0_ref[...]
    for t in range(steps):
        xc = cb_ref[...]
        for j in range(CONV_WIDTH):
            xc = xc + cw_ref[j:j + 1, :] * rows[t + j]
        a, u = _lru_inputs(xc, wr_ref, wi_ref, bg_ref, sp)
        h = a * h + u
        z_ref[t] = h * ga_ref[t]
    hl_ref[...] = h


def _scan_steps(xb, ga, st, h0, cw, cb, wr, wi, bg, lam):
    steps, nb, dr = xb.shape
    args = (xb, ga, st, h0, cw, cb, wr, wi, bg, lam)
    return pl.pallas_call(
        functools.partial(_scan_steps_kernel, steps=steps),
        grid=(1,),
        in_specs=[_resident(a.shape) for a in args],
        out_specs=[pl.BlockSpec((steps, nb, dr), lambda i: (0, 0, 0)),
                   pl.BlockSpec((nb, dr), lambda i: (0, 0))],
        out_shape=[jax.ShapeDtypeStruct((steps, nb, dr), F32),
                   jax.ShapeDtypeStruct((nb, dr), F32)],
        compiler_params=_params("arbitrary"),
        name="scan_steps",
    )(*args)


def _block_bias_t(gate_t, cur):
    row = lax.broadcasted_iota(jnp.int32, gate_t.shape, 0)
    row_f = row.astype(F32)
    g = jnp.where(row < cur, gate_t, NEG)
    bias = jnp.where(row == cur, 0.0, NEG)
    for _ in range(min(MOBA_TOP_K, cur)):
        m = jnp.max(g, axis=0, keepdims=True)
        idx = jnp.min(jnp.where(g == m, row_f, float(LANES)), axis=0, keepdims=True)
        pick = row_f == idx
        bias = jnp.where(pick, 0.0, bias)
        g = jnp.where(pick, NEG, g)
    return bias


def _attn_prompt_kernel(q_ref, k_ref, v_ref, o_ref, kaug_ref, vt_ref, means_ref, *, seq, hd, scale):
    bq = MOBA_BLOCK
    nblk = seq // bq
    nbp = means_ref.shape[0]

    lane = lax.broadcasted_iota(jnp.int32, (bq, hd), 1)
    means_ref[...] = jnp.zeros(means_ref.shape, F32)
    for n in range(nblk):
        kb = k_ref[0, 0, n * bq:(n + 1) * bq, :]
        means_ref[n:n + 1, :] = jnp.sum(kb, axis=0, keepdims=True) * (1.0 / MOBA_BLOCK)
        onehot = jnp.where(lane == n, 1.0, 0.0).astype(BF16)
        kaug_ref[n * bq:(n + 1) * bq, :] = jnp.concatenate([kb.astype(BF16), onehot], axis=1)
        vt_ref[:, n * bq:(n + 1) * bq] = v_ref[0, 0, n * bq:(n + 1) * bq, :].T.astype(BF16)

    causal = (lax.broadcasted_iota(jnp.int32, (bq, bq), 0)
              <= lax.broadcasted_iota(jnp.int32, (bq, bq), 1))

    for c in range(nblk):
        qt = q_ref[0, 0, c * bq:(c + 1) * bq, :].T
        gate_t = _dot3(means_ref[...], qt)
        qt_aug = jnp.concatenate(
            [qt * scale, _block_bias_t(gate_t, c), jnp.zeros((hd - nbp, bq), F32)],
            axis=0).astype(BF16)
        s = _dot(kaug_ref[0:(c + 1) * bq, :], qt_aug)
        s_own = jnp.where(causal, s[c * bq:], NEG)
        m = jnp.max(s_own, axis=0, keepdims=True)
        if c:
            m = jnp.maximum(m, jnp.max(s[:c * bq], axis=0, keepdims=True))
        p = jnp.exp(s_own - m)
        if c:
            p = jnp.concatenate([jnp.exp(s[:c * bq] - m), p], axis=0)
        l = jnp.sum(p, axis=0, keepdims=True)
        ot = _dot(vt_ref[:, 0:(c + 1) * bq], p.astype(BF16))
        o_ref[0, c * bq:(c + 1) * bq, :] = (ot * (1.0 / l)).T


def _attn_prompt(q, k, v):
    b, h, seq, hd = q.shape
    nblk = seq // MOBA_BLOCK
    assert seq % MOBA_BLOCK == 0 and nblk <= hd and hd == LANES
    nbp = -(-nblk // SUBLANES) * SUBLANES
    full = pl.BlockSpec((1, 1, seq, hd), lambda bi, hi: (bi, hi, 0, 0))
    return pl.pallas_call(
        functools.partial(_attn_prompt_kernel, seq=seq, hd=hd, scale=hd ** -0.5),
        grid=(b, h),
        in_specs=[full, full, full],
        out_specs=pl.BlockSpec((1, seq, hd), lambda bi, hi: (bi, 0, hi)),
        out_shape=jax.ShapeDtypeStruct((b, seq, h * hd), F32),
        scratch_shapes=[pltpu.VMEM((seq, 2 * hd), BF16), pltpu.VMEM((hd, seq), BF16),
                        pltpu.VMEM((nbp, hd), F32)],
        compiler_params=_params("parallel", "parallel"),
        name="attn_prompt",
    )(q, k, v)


def _page_means_kernel(pt_ref, *refs, n):
    k_refs, o_ref = refs[:n], refs[n]
    n_heads, _, hd = k_refs[0].shape[1:]
    for h in range(n_heads):
        sums = [jnp.sum(k_refs[j][0, h], axis=0, keepdims=True) for j in range(n)]
        o_ref[0, :, h * hd:(h + 1) * hd] = jnp.concatenate(
            [(sums[2 * i] + sums[2 * i + 1]) * (1.0 / MOBA_BLOCK) for i in range(n // 2)], axis=0)


def _page_means(cache_k, page_table):
    db, n_pages = page_table.shape
    _, h, _, hd = cache_k.shape
    n = min(16, n_pages)

    def page(j):
        return pl.BlockSpec((1, h, PAGE_SIZE, hd), lambda b, g, pt: (pt[b, g * n + j], 0, 0, 0))

    return pl.pallas_call(
        functools.partial(_page_means_kernel, n=n),
        grid_spec=pltpu.PrefetchScalarGridSpec(
            num_scalar_prefetch=1,
            grid=(db, n_pages // n),
            in_specs=[page(j) for j in range(n)],
            out_specs=pl.BlockSpec((1, n // 2, h * hd), lambda b, g, pt: (b, g, 0)),
        ),
        out_shape=jax.ShapeDtypeStruct((db, n_pages // 2, h * hd), F32),
        compiler_params=_params("parallel", "arbitrary"),
        name="page_means",
    )(page_table, *([cache_k] * n))


def _select_kernel(q_ref, means_ref, pt_ref, sel_ref):
    n_blocks = means_ref.shape[1]
    means = jnp.concatenate(
        [means_ref[0], jnp.zeros((LANES - n_blocks, means_ref.shape[2]), F32)], axis=0)
    gate = _dot3(q_ref[0], means, _dot_nt)
    lane = lax.broadcasted_iota(jnp.int32, gate.shape, 1)
    lane_f = lane.astype(F32)
    pages = pt_ref[0]
    g = jnp.where(lane < n_blocks, gate, NEG)
    out = jnp.zeros(gate.shape, F32)
    for j in range(MOBA_TOP_K):
        m = jnp.max(g, axis=1, keepdims=True)
        idx = jnp.min(jnp.where(g == m, lane_f, float(LANES)), axis=1, keepdims=True)
        for p in range(PAGES_PER_BLOCK):
            phys = jnp.sum(jnp.where(lane_f == idx * PAGES_PER_BLOCK + p, pages, 0.0),
                           axis=1, keepdims=True)
            out = jnp.where(lane == j * PAGES_PER_BLOCK + p, phys, out)
        g = jnp.where(lane_f == idx, NEG, g)
    sel_ref[0] = out.astype(jnp.int32)


def _select(q_bd, means, pages_f):
    db, r, w = q_bd.shape
    nb = means.shape[1]
    assert MOBA_TOP_K <= nb and nb * PAGES_PER_BLOCK <= LANES
    return pl.pallas_call(
        _select_kernel,
        grid=(db,),
        in_specs=[pl.BlockSpec((1, r, w), lambda b: (b, 0, 0)),
                  pl.BlockSpec((1, nb, w), lambda b: (b, 0, 0)),
                  pl.BlockSpec((1, 1, LANES), lambda b: (b, 0, 0))],
        out_specs=pl.BlockSpec((1, r, LANES), lambda b: (b, 0, 0)),
        out_shape=jax.ShapeDtypeStruct((db, r, LANES), jnp.int32),
        compiler_params=_params("parallel"),
        name="select_blocks",
    )(q_bd, means, pages_f)


def _attn_paged_kernel(sel_ref, q_ref, kn_ref, vn_ref, ck_ref, cv_ref, o_ref, kbuf, vbuf, sems,
                       *, n_slots, sq, scale):
    n_heads = pl.num_programs(1)
    step = pl.program_id(0) * n_heads + pl.program_id(1)
    cur = step % 2

    def page_copies(b, h, buf, page_of_slot):
        out = []
        for s in range(n_slots):
            page = page_of_slot(b, h, s)
            out.append(pltpu.make_async_copy(ck_ref.at[page, h], kbuf.at[buf, s], sems.at[0, buf]))
            out.append(pltpu.make_async_copy(cv_ref.at[page, h], vbuf.at[buf, s], sems.at[1, buf]))
        return out

    def selected(b, h, s):
        return sel_ref[b, h * n_slots + s]

    @pl.when(step == 0)
    def _():
        for c in page_copies(pl.program_id(0), pl.program_id(1), cur, selected):
            c.start()

    @pl.when(step + 1 < pl.num_programs(0) * n_heads)
    def _():
        nxt = step + 1
        for c in page_copies(nxt // n_heads, nxt % n_heads, 1 - cur, selected):
            c.start()

    for c in page_copies(0, 0, cur, lambda b, h, s: 0):
        c.wait()

    q = q_ref[0, 0]
    kn, vn = kn_ref[0, 0], vn_ref[0, 0]
    hd = q.shape[1]
    k_all = kbuf[cur].reshape(n_slots * PAGE_SIZE, hd).astype(BF16)
    v_all = vbuf[cur].reshape(n_slots * PAGE_SIZE, hd).astype(BF16)
    s = _dot_nt((q * scale).astype(BF16), k_all)
    keys_per_q = MOBA_TOP_K * MOBA_BLOCK
    row = lax.broadcasted_iota(jnp.int32, s.shape, 0)
    col = lax.broadcasted_iota(jnp.int32, s.shape, 1)
    s = jnp.where((col >= row * keys_per_q) & (col < (row + 1) * keys_per_q), s, NEG)
    row1 = lax.broadcasted_iota(jnp.int32, (q.shape[0], 1), 0)
    s_new = [jnp.where(row1 >= j, jnp.sum(q * kn[j:j + 1, :], axis=1, keepdims=True) * scale, NEG)
             for j in range(sq)]
    m = functools.reduce(jnp.maximum, s_new + [jnp.max(s, axis=1, keepdims=True)])
    p = jnp.exp(s - m)
    p_new = [jnp.exp(sj - m) for sj in s_new]
    l = functools.reduce(lambda x, y: x + y, p_new + [jnp.sum(p, axis=1, keepdims=True)])
    acc = functools.reduce(lambda x, y: x + y,
                           [p_new[j] * vn[j:j + 1, :] for j in range(sq)]
                           + [_dot(p.astype(BF16), v_all)])
    o_ref[0, 0] = acc / l


def _attn_paged(q, k_new, v_new, cache_k, cache_v, sel, sq):
    db, h, rows, hd = q.shape
    n_slots = sq * MOBA_TOP_K * PAGES_PER_BLOCK
    per_head = pl.BlockSpec((1, 1, rows, hd), lambda b, hi, sel: (b, hi, 0, 0))
    hbm = pl.BlockSpec(memory_space=pl.ANY)
    return pl.pallas_call(
        functools.partial(_attn_paged_kernel, n_slots=n_slots, sq=sq, scale=hd ** -0.5),
        grid_spec=pltpu.PrefetchScalarGridSpec(
            num_scalar_prefetch=1,
            grid=(db, h),
            in_specs=[per_head, per_head, per_head, hbm, hbm],
            out_specs=per_head,
            scratch_shapes=[pltpu.VMEM((2, n_slots, PAGE_SIZE, hd), F32),
                            pltpu.VMEM((2, n_slots, PAGE_SIZE, hd), F32),
                            pltpu.SemaphoreType.DMA((2, 2))],
        ),
        out_shape=jax.ShapeDtypeStruct((db, h, rows, hd), F32),
        compiler_params=_params("arbitrary", "arbitrary"),
        name="attn_paged",
    )(sel, q, k_new, v_new, cache_k, cache_v)


def _rope_tables(pos, hd):
    half = hd // 8
    inv = ROPE_THETA ** (-jnp.arange(half, dtype=F32) * 2.0 / (2 * half))
    ang = pos.astype(F32)[:, None] * inv[None, :]
    cos, sin = jnp.cos(ang), jnp.sin(ang)
    n = pos.shape[0]
    ones = jnp.ones((n, hd - 2 * half), F32)
    zeros = jnp.zeros((n, hd - 2 * half), F32)
    z = jnp.zeros_like(sin)
    return (jnp.concatenate([cos, cos, ones], axis=1),
            jnp.concatenate([-sin, z, zeros], axis=1),
            jnp.concatenate([z, sin, zeros], axis=1))


def _block_diag_gate(w_gate):
    n, k, _ = w_gate.shape
    per = GATE_GROUP // k
    eye = jnp.eye(per, dtype=w_gate.dtype)

    def build(w):
        w = w.reshape(n // per, per, k, k)
        return jnp.einsum("gmkj,mn->gmknj", w, eye).reshape(n // per, GATE_GROUP, GATE_GROUP).astype(BF16)

    return build(w_gate[:, :, :k]), build(w_gate[:, :, k:])


def kernel(x_prompt, x_sample, c_prompt, c_sample, state_conv, state_rglru, cache_k, cache_v, page_table, g_norm, w_ada, b_ada, w_rec_in, conv_w, conv_b, w_gate, b_gate, lam, w_rec_out, g_kv, w_ada_kv, b_ada_kv, w_kv, g_k, w_q, g_q, w_o, w_up, w_down):
    bp, sp, d = x_prompt.shape
    db, sq, _ = x_sample.shape
    depth = g_norm.shape[0]
    n_a = w_rec_in.shape[0]
    dr = w_rec_in.shape[2] // 2
    hd = g_k.shape[0]
    n_heads = w_q.shape[2] // hd
    n_pages = page_table.shape[1]
    d_ff = w_up.shape[2]
    assert hd == LANES and n_heads == N_HEADS and sq <= SUBLANES and n_pages % PAGES_PER_BLOCK == 0

    c_all = jnp.concatenate([c_prompt, c_sample], axis=0)
    pad = (-c_all.shape[0]) % SUBLANES
    c_all = jnp.pad(c_all, ((0, pad), (0, 0)))
    mods = _adaln(c_all, w_ada.reshape(depth * 2, d, 3 * d), b_ada.reshape(depth * 2, 1, 3 * d))
    mods_kv = _adaln(c_all, w_ada_kv[None], b_ada_kv[None, None])[0]

    def group_mods(m, prompt):
        if prompt:
            return m[:bp, None, :]
        return jnp.repeat(m[bp:bp + db], sq, axis=0)[None]

    w_in16 = w_rec_in.astype(BF16)
    w_out16 = w_rec_out.astype(BF16)
    w_kv16, w_q16, w_o16 = w_kv.astype(BF16), w_q.astype(BF16), w_o.astype(BF16)
    w_up16, w_down16 = w_up.astype(BF16), w_down.astype(BF16)
    gates = [_block_diag_gate(w_gate[l]) for l in range(n_a)]
    pages_f = jnp.pad(page_table.astype(F32), ((0, 0), (0, LANES - n_pages)))[:, None, :]

    def run_group(x3, prompt):
        nb, seq, _ = x3.shape
        t = nb * seq
        x = x3.reshape(t, d)
        tm = 512 if prompt else t
        tps = seq // tm if prompt else 1
        if prompt:
            pos = jnp.arange(seq, dtype=jnp.int32)
            tables = _rope_tables(pos, hd)
            tab_nb, tab_seq = nb, seq
        else:
            pos = n_pages * PAGE_SIZE + jnp.arange(seq, dtype=jnp.int32)
            tables = tuple(jnp.tile(tb, (nb, 1)) for tb in _rope_tables(pos, hd))
            tab_nb, tab_seq = 1, t

        def pad_rows(a):
            a = a.reshape(n_heads, nb, seq, hd).transpose(1, 0, 2, 3)
            return jnp.pad(a, ((0, 0), (0, 0), (0, SUBLANES - seq), (0, 0)))

        conv_new, h_new = [], []
        k = v = means = k8 = v8 = None
        for l in range(depth):
            mod = group_mods(mods[2 * l], prompt)
            g1 = g_norm[l, 0][None]
            if l == n_a:
                k, v = _head_proj(x, group_mods(mods_kv, prompt), g_kv[None], w_kv16, g_k[None],
                                  tables, tab_nb, tab_seq, tm, n_heads, n_heads, hd)
                if not prompt:
                    means = _page_means(cache_k, page_table)
                    k8, v8 = pad_rows(k), pad_rows(v)
            if l < n_a:
                ga, xb = _rec_in(x, mod, g1, w_in16[l], tm, tps)
                wr, wi = gates[l]
                cw, cb, bg, lm = conv_w[l], conv_b[l][None], b_gate[l], lam[l][None]
                if prompt:
                    st8 = jnp.zeros((nb, SUBLANES, dr), F32)
                    h0 = jnp.zeros((nb, 1, dr), F32)
                    z, tail, h_last = _scan_prompt(xb, ga, st8, h0, cw, cb, wr, wi, bg, lm,
                                                   nb, seq, 256)
                    conv_new.append(tail[:, SUBLANES - (CONV_WIDTH - 1):])
                    h_new.append(h_last[:, 0])
                else:
                    xb_t = xb.reshape(nb, seq, dr).transpose(1, 0, 2)
                    ga_t = ga.reshape(nb, seq, dr).transpose(1, 0, 2)
                    st = state_conv[l].transpose(1, 0, 2)
                    z_t, h_last = _scan_steps(xb_t, ga_t, st, state_rglru[l], cw, cb, wr, wi, bg, lm)
                    z = z_t.transpose(1, 0, 2).reshape(t, dr)
                    hist = jnp.concatenate([state_conv[l], xb.reshape(nb, seq, dr)], axis=1)
                    conv_new.append(hist[:, -(CONV_WIDTH - 1):])
                    h_new.append(h_last)
                w_mix = w_out16[l]
            else:
                j = l - n_a
                (q,) = _head_proj(x, mod, g1, w_q16[j], g_q[j][None], tables, tab_nb, tab_seq, tm,
                                  n_heads, 0, hd)
                if prompt:
                    z = _attn_prompt(q, k, v).reshape(t, n_heads * hd)
                else:
                    q8 = pad_rows(q)
                    q_bd = jnp.einsum("bhqd,hg->bhqgd", q8, jnp.eye(n_heads, dtype=F32))
                    sel = _select(q_bd.reshape(nb, n_heads * SUBLANES, n_heads * hd), means, pages_f)
                    sel = sel.reshape(nb, n_heads, SUBLANES, LANES)
                    sel = sel[:, :, :seq, :MOBA_TOP_K * PAGES_PER_BLOCK].reshape(nb, -1)
                    z = _attn_paged(q8, k8, v8, cache_k, cache_v, sel, seq)
                    z = z[:, :, :seq].transpose(0, 2, 1, 3).reshape(t, n_heads * hd)
                w_mix = w_o16[j]
            mod2 = group_mods(mods[2 * l + 1], prompt)
            x = _mix_mlp(x, z, mod, w_mix, mod2, g_norm[l, 1][None], w_up16[l], w_down16[l],
                         tm, tps)
        if not prompt:
            k = k.reshape(n_heads, nb, seq, hd).transpose(1, 0, 2, 3)
            v = v.reshape(n_heads, nb, seq, hd).transpose(1, 0, 2, 3)
        return x.reshape(nb, seq, d), jnp.stack(conv_new), jnp.stack(h_new), k, v

    y_p, conv_p, h_p, k_p, v_p = run_group(x_prompt, True)
    y_s, conv_s, h_s, k_s, v_s = run_group(x_sample, False)
    return (y_p, y_s, conv_p, h_p, conv_s, h_s, k_p, v_p, k_s, v_s)
```

```python
import functools
import math

import jax
import jax.numpy as jnp
from jax import lax
from jax.experimental import pallas as pl
from jax.experimental.pallas import tpu as pltpu

F32 = jnp.float32
BF16 = jnp.bfloat16

NORM_EPS = 1e-6
LRU_C = 8.0
CONV_WIDTH = 4
PAGE_SIZE = 128
MOBA_BLOCK = 256
MOBA_TOP_K = 3
PAGES_PER_BLOCK = MOBA_BLOCK // PAGE_SIZE
N_HEADS = 8
ROPE_THETA = 500000.0
NEG = -1e30
LANES = 128
SUBLANES = 8
GATE_GROUP = 256
VMEM_LIMIT = 56 * 1024 * 1024


def _params(*sem):
    return pltpu.CompilerParams(dimension_semantics=sem, vmem_limit_bytes=VMEM_LIMIT)


def _resident(shape):
    nd = len(shape)
    return pl.BlockSpec(shape, lambda *_: (0,) * nd, pipeline_mode=pl.Buffered(1))


def _dot(a, b):
    return jnp.dot(a, b, preferred_element_type=F32)


def _dot_nt(a, b):
    return lax.dot_general(a, b, (((1,), (1,)), ((), ())), preferred_element_type=F32)


def _split(a):
    hi = a.astype(BF16)
    return hi, (a - hi.astype(F32)).astype(BF16)


def _dot3(a, b, dot=_dot):
    ah, al = _split(a)
    bh, bl = _split(b)
    return dot(ah, bh) + (dot(ah, bl) + dot(al, bh))


def _sigmoid(x):
    return 0.5 + 0.5 * jnp.tanh(0.5 * x)


def _gelu_tanh(x):
    return 0.5 * x * (1.0 + jnp.tanh(math.sqrt(2.0 / math.pi) * (x + 0.044715 * (x * x * x))))


def _softplus(x):
    return jnp.maximum(x, 0.0) + jnp.log1p(jnp.exp(-jnp.abs(x)))


def _modulate(x, g, mod, d):
    ms = jnp.mean(x * x, axis=-1, keepdims=True)
    y = x * lax.rsqrt(ms + NORM_EPS) * g
    return y * (1.0 + mod[:, d:2 * d]) + mod[:, :d]


def _adaln_kernel(c_ref, w_ref, b_ref, o_ref):
    c = c_ref[...]
    o_ref[0] = _dot3(c * _sigmoid(c), w_ref[0]) + b_ref[0]


def _adaln(c, w, b):
    r, d = c.shape
    m, _, n = w.shape
    tn = 1024
    return pl.pallas_call(
        _adaln_kernel,
        grid=(m, n // tn),
        in_specs=[
            pl.BlockSpec((r, d), lambda i, j: (0, 0)),
            pl.BlockSpec((1, d, tn), lambda i, j: (i, 0, j)),
            pl.BlockSpec((1, 1, tn), lambda i, j: (i, 0, j)),
        ],
        out_specs=pl.BlockSpec((1, r, tn), lambda i, j: (i, 0, j)),
        out_shape=jax.ShapeDtypeStruct((m, r, n), F32),
        compiler_params=_params("parallel", "parallel"),
        name="adaln",
    )(c, w, b)


def _mod_spec(mod, tm, tiles_per_seq):
    nb, rows, w = mod.shape
    if rows == 1:
        return pl.BlockSpec((1, 1, w), lambda i, *_: (i // tiles_per_seq, 0, 0))
    return pl.BlockSpec((1, tm, w), lambda i, *_: (0, i, 0))


def _rec_in_kernel(x_ref, mod_ref, g_ref, w_ref, ga_ref, xb_ref, *, d, dr):
    h = _modulate(x_ref[...], g_ref[...], mod_ref[0], d).astype(BF16)
    y = _dot(h, w_ref[...])
    ga_ref[...] = _gelu_tanh(y[:, :dr])
    xb_ref[...] = y[:, dr:]


def _rec_in(x, mod, g, w, tm, tiles_per_seq):
    t, d = x.shape
    dr = w.shape[1] // 2
    row = pl.BlockSpec((tm, dr), lambda i: (i, 0))
    return pl.pallas_call(
        functools.partial(_rec_in_kernel, d=d, dr=dr),
        grid=(t // tm,),
        in_specs=[
            pl.BlockSpec((tm, d), lambda i: (i, 0)),
            _mod_spec(mod, tm, tiles_per_seq),
            _resident((1, d)),
            _resident(w.shape),
        ],
        out_specs=[row, row],
        out_shape=[jax.ShapeDtypeStruct((t, dr), F32)] * 2,
        compiler_params=_params("parallel"),
        name="rec_in",
    )(x, mod, g, w)


def _rope_partner(hd):
    half = hd // 8
    i = jnp.arange(hd)
    return jnp.where(i < half, i + half, jnp.where(i < 2 * half, i - half, i))


def _with_partner_columns(w, n_rope, hd):
    d = w.shape[0]
    wr = w[:, :n_rope * hd].reshape(d, n_rope, hd)[:, :, _rope_partner(hd)]
    return jnp.concatenate([w, wr.reshape(d, n_rope * hd)], axis=1)


def _head_proj_kernel(x_ref, mod_ref, g_ref, w_ref, gh_ref, ghp_ref, cos_ref, sin_ref, *out_refs,
                      d, hd, n_rope, n_plain):
    h_ref = out_refs[-1]
    h_ref[...] = _modulate(x_ref[...], g_ref[...], mod_ref[0], d).astype(BF16)
    cg = cos_ref[...] * gh_ref[...]
    sg = sin_ref[...] * ghp_ref[...]
    partner0 = n_rope + n_plain
    roped, plain = [], []
    for i0 in range(0, n_rope + n_plain, 2):
        y = _dot(h_ref[...], w_ref[:, i0 * hd:(i0 + 2) * hd])
        if i0 < n_rope:
            yp = _dot(h_ref[...], w_ref[:, (partner0 + i0) * hd:(partner0 + i0 + 2) * hd])
        for i in (i0, i0 + 1):
            cols = slice((i - i0) * hd, (i - i0 + 1) * hd)
            if i >= n_rope:
                plain.append(y[:, cols])
                continue
            yh = y[:, cols]
            rs = lax.rsqrt(jnp.mean(yh * yh, axis=-1, keepdims=True) + NORM_EPS)
            roped.append(rs * (yh * cg + yp[:, cols] * sg))
    out_refs[0][0] = jnp.stack(roped, axis=0)
    if plain:
        out_refs[1][0] = jnp.stack(plain, axis=0)


def _head_proj(x, mod, g, w, gh, tables, nb, seq, tm, n_rope, n_plain, hd):
    t, d = x.shape
    assert n_rope % 2 == 0 and n_plain % 2 == 0
    tps = seq // tm
    ghp = gh[:, _rope_partner(hd)]
    tab = pl.BlockSpec((tm, hd), lambda i: (i % tps, 0))
    out_specs = [pl.BlockSpec((1, n_rope, tm, hd), lambda i: (i // tps, 0, i % tps, 0))]
    out_shape = [jax.ShapeDtypeStruct((nb, n_rope, seq, hd), F32)]
    if n_plain:
        out_specs.append(pl.BlockSpec((1, n_plain, tm, hd), lambda i: (i // tps, 0, i % tps, 0)))
        out_shape.append(jax.ShapeDtypeStruct((nb, n_plain, seq, hd), F32))
    return pl.pallas_call(
        functools.partial(_head_proj_kernel, d=d, hd=hd, n_rope=n_rope, n_plain=n_plain),
        grid=(t // tm,),
        in_specs=[
            pl.BlockSpec((tm, d), lambda i: (i, 0)),
            _mod_spec(mod, tm, tps),
            _resident((1, d)),
            _resident(w.shape),
            _resident((1, hd)),
            _resident((1, hd)),
            tab, tab,
        ],
        out_specs=out_specs,
        out_shape=out_shape,
        scratch_shapes=[pltpu.VMEM((tm, d), BF16)],
        compiler_params=_params("parallel"),
        name="head_proj",
    )(x, mod, g, w, gh, ghp, *tables)


FF_CHUNK = 1024


PAGES_PER_MEANS_STEP = 16


def _page_block_means(k_refs, o_ref):
    n = len(k_refs)
    n_heads, _, hd = k_refs[0].shape[1:]
    for h in range(n_heads):
        sums = [jnp.sum(k_refs[j][0, h], axis=0, keepdims=True) for j in range(n)]
        o_ref[0, :, h * hd:(h + 1) * hd] = jnp.concatenate(
            [(sums[2 * i] + sums[2 * i + 1]) * (1.0 / MOBA_BLOCK) for i in range(n // 2)], axis=0)


def _mix_mlp_kernel(*refs, d, n_side):
    if n_side:
        refs = refs[1:]
    x_ref, z_ref, mod1_ref, wo_ref, mod2_ref, g_ref, wu_ref, wd_ref = refs[:8]
    o_ref = refs[8 + n_side]
    x1_ref, h_ref, acc_ref = refs[-3:]
    if n_side:
        _page_block_means(refs[8:8 + n_side], refs[9 + n_side])
    x1 = x_ref[...] + mod1_ref[0][:, 2 * d:] * _dot(z_ref[...].astype(BF16), wo_ref[...])
    x1_ref[...] = x1
    mod2 = mod2_ref[0]
    h_ref[...] = _modulate(x1, g_ref[...], mod2, d).astype(BF16)
    for c in range(wu_ref.shape[1] // FF_CHUNK):
        cols = slice(c * FF_CHUNK, (c + 1) * FF_CHUNK)
        u = _dot(h_ref[...], wu_ref[:, cols])
        a = jnp.square(jnp.maximum(u, 0.0)).astype(BF16)
        part = _dot(a, wd_ref[cols, :])
        if c == 0:
            acc_ref[...] = part
        else:
            acc_ref[...] += part
    o_ref[...] = x1_ref[...] + mod2[:, 2 * d:] * acc_ref[...]


def _mix_mlp(x, z, mod1, wo, mod2, g, wu, wd, tm, tiles_per_seq, side=None):
    t, d = x.shape
    assert wu.shape[1] % FF_CHUNK == 0
    steps = t // tm
    row = pl.BlockSpec((tm, d), lambda i, *_: (i, 0))
    in_specs = [
        row,
        pl.BlockSpec((tm, z.shape[1]), lambda i, *_: (i, 0)),
        _mod_spec(mod1, tm, tiles_per_seq),
        _resident(wo.shape),
        _mod_spec(mod2, tm, tiles_per_seq),
        _resident((1, d)),
        _resident(wu.shape),
        _resident(wd.shape),
    ]
    scratch = [pltpu.VMEM((tm, d), F32), pltpu.VMEM((tm, d), BF16), pltpu.VMEM((tm, d), F32)]
    out_specs, out_shape = row, jax.ShapeDtypeStruct((t, d), F32)
    args = (x, z, mod1, wo, mod2, g, wu, wd)
    if side is None:
        return pl.pallas_call(
            functools.partial(_mix_mlp_kernel, d=d, n_side=0),
            grid=(steps,), in_specs=in_specs, out_specs=out_specs, out_shape=out_shape,
            scratch_shapes=scratch, compiler_params=_params("parallel"), name="mix_mlp",
        )(*args)
    cache_k, page_table, first_seq = side
    n = PAGES_PER_MEANS_STEP
    n_pages = page_table.shape[1]
    _, h, _, hd = cache_k.shape
    gps = n_pages // n

    def page(j):
        return pl.BlockSpec((1, h, PAGE_SIZE, hd),
                            lambda i, pt: (pt[first_seq + i // gps, (i % gps) * n + j], 0, 0, 0))

    return pl.pallas_call(
        functools.partial(_mix_mlp_kernel, d=d, n_side=n),
        grid_spec=pltpu.PrefetchScalarGridSpec(
            num_scalar_prefetch=1,
            grid=(steps,),
            in_specs=in_specs + [page(j) for j in range(n)],
            out_specs=[out_specs,
                       pl.BlockSpec((1, n // PAGES_PER_BLOCK, h * hd),
                                    lambda i, pt: (i // gps, i % gps, 0))],
            scratch_shapes=scratch,
        ),
        out_shape=[out_shape,
                   jax.ShapeDtypeStruct((steps // gps, n_pages // PAGES_PER_BLOCK, h * hd), F32)],
        compiler_params=_params("parallel"),
        name="mix_mlp_means",
    )(page_table, *args, *([cache_k] * n))


def _lru_inputs(xc, wr_ref, wi_ref, bg_ref, sp):
    xh = xc.astype(BF16)
    gw = wr_ref.shape[1]
    r = jnp.concatenate([_dot(xh[:, j * gw:(j + 1) * gw], wr_ref[j])
                         for j in range(wr_ref.shape[0])], axis=1) + bg_ref[0:1, :]
    i = jnp.concatenate([_dot(xh[:, j * gw:(j + 1) * gw], wi_ref[j])
                         for j in range(wi_ref.shape[0])], axis=1) + bg_ref[1:2, :]
    log_a = -LRU_C * _sigmoid(r) * sp
    t = jnp.tanh(0.5 * log_a)
    rc = 1.0 / (1.0 - t)
    a = (1.0 + t) * rc
    u = (2.0 * rc) * jnp.sqrt(-t) * (_sigmoid(i) * xc)
    return a, u


def _scan_prompt_kernel(xb_ref, ga_ref, st_ref, h0_ref, cw_ref, cb_ref, wr_ref, wi_ref, bg_ref,
                        lam_ref, z_ref, tail_out_ref, hl_ref, tail_ref, hc_ref, a_ref, u_ref,
                        *, tt):
    ti = pl.program_id(1)
    dr = xb_ref.shape[1]

    @pl.when(ti == 0)
    def _():
        tail_ref[...] = st_ref[0]
        hc_ref[...] = h0_ref[0]

    x = xb_ref[...]
    tail = tail_ref[...]
    row8 = lax.broadcasted_iota(jnp.int32, (SUBLANES, dr), 0)
    xc = cb_ref[...] + cw_ref[CONV_WIDTH - 1:CONV_WIDTH, :] * x
    for k in range(1, CONV_WIDTH):
        xs = pltpu.roll(x, k, 0)
        first = jnp.where(row8 < k, pltpu.roll(tail, k, 0), xs[:SUBLANES])
        xs = jnp.concatenate([first, xs[SUBLANES:]], axis=0)
        xc = xc + cw_ref[CONV_WIDTH - 1 - k:CONV_WIDTH - k, :] * xs
    tail_ref[...] = x[tt - SUBLANES:]
    tail_out_ref[0] = x[tt - SUBLANES:]

    a, u = _lru_inputs(xc, wr_ref, wi_ref, bg_ref, _softplus(-lam_ref[...]))
    a_ref[...] = a
    u_ref[...] = u

    def group(gi, carry):
        r0 = pl.multiple_of(gi * SUBLANES, SUBLANES)
        a8 = a_ref[pl.ds(r0, SUBLANES), :]
        u8 = u_ref[pl.ds(r0, SUBLANES), :]
        for s in (1, 2, 4):
            keep = row8 >= s
            u8 = jnp.where(keep, a8 * pltpu.roll(u8, s, 0) + u8, u8)
            a8 = jnp.where(keep, a8 * pltpu.roll(a8, s, 0), a8)
        h8 = a8 * carry + u8
        z_ref[pl.ds(r0, SUBLANES), :] = h8 * ga_ref[pl.ds(r0, SUBLANES), :]
        return jnp.broadcast_to(h8[SUBLANES - 1:SUBLANES, :], (SUBLANES, dr))

    carry = lax.fori_loop(0, tt // SUBLANES, group,
                          jnp.broadcast_to(hc_ref[...], (SUBLANES, dr)), unroll=2)
    hc_ref[...] = carry[0:1, :]
    hl_ref[0] = carry[0:1, :]


def _scan_prompt(xb, ga, st8, h0, cw, cb, wr, wi, bg, lam, nb, seq, tt):
    t, dr = xb.shape
    nt = seq // tt
    row = pl.BlockSpec((tt, dr), lambda b, i: (b * nt + i, 0))
    per_seq8 = pl.BlockSpec((1, SUBLANES, dr), lambda b, i: (b, 0, 0))
    per_seq1 = pl.BlockSpec((1, 1, dr), lambda b, i: (b, 0, 0))
    return pl.pallas_call(
        functools.partial(_scan_prompt_kernel, tt=tt),
        grid=(nb, nt),
        in_specs=[row, row, per_seq8, per_seq1, _resident(cw.shape), _resident(cb.shape),
                  _resident(wr.shape), _resident(wi.shape), _resident(bg.shape),
                  _resident(lam.shape)],
        out_specs=[row, per_seq8, per_seq1],
        out_shape=[jax.ShapeDtypeStruct((t, dr), F32),
                   jax.ShapeDtypeStruct((nb, SUBLANES, dr), F32),
                   jax.ShapeDtypeStruct((nb, 1, dr), F32)],
        scratch_shapes=[pltpu.VMEM((SUBLANES, dr), F32), pltpu.VMEM((1, dr), F32),
                        pltpu.VMEM((tt, dr), F32), pltpu.VMEM((tt, dr), F32)],
        compiler_params=_params("parallel", "arbitrary"),
        name="scan_prompt",
    )(xb, ga, st8, h0, cw, cb, wr, wi, bg, lam)


def _scan_steps_kernel(xb_ref, ga_ref, st_ref, h0_ref, cw_ref, cb_ref, wr_ref, wi_ref, bg_ref,
                       lam_ref, z_ref, hl_ref, *, steps):
    rows = [st_ref[j] for j in range(CONV_WIDTH - 1)] + [xb_ref[t] for t in range(steps)]
    sp = _softplus(-lam_ref[...])
    h = h0_ref[...]
    for t in range(steps):
        xc = cb_ref[...]
        for j in range(CONV_WIDTH):
            xc = xc + cw_ref[j:j + 1, :] * rows[t + j]
        a, u = _lru_inputs(xc, wr_ref, wi_ref, bg_ref, sp)
        h = a * h + u
        z_ref[t] = h * ga_ref[t]
    hl_ref[...] = h


def _scan_steps(xb, ga, st, h0, cw, cb, wr, wi, bg, lam):
    steps, nb, dr = xb.shape
    args = (xb, ga, st, h0, cw, cb, wr, wi, bg, lam)
    return pl.pallas_call(
        functools.partial(_scan_steps_kernel, steps=steps),
        grid=(1,),
        in_specs=[_resident(a.shape) for a in args],
        out_specs=[pl.BlockSpec((steps, nb, dr), lambda i: (0, 0, 0)),
                   pl.BlockSpec((nb, dr), lambda i: (0, 0))],
        out_shape=[jax.ShapeDtypeStruct((steps, nb, dr), F32),
                   jax.ShapeDtypeStruct((nb, dr), F32)],
        compiler_params=_params("arbitrary"),
        name="scan_steps",
    )(*args)


ATTN_KEY_CHUNK = 128


def _block_bias_t(gate_t, cur):
    row = lax.broadcasted_iota(jnp.int32, gate_t.shape, 0)
    row_f = row.astype(F32)
    g = jnp.where(row < cur, gate_t, NEG)
    bias = jnp.where(row == cur, 0.0, NEG)
    for _ in range(min(MOBA_TOP_K, cur)):
        m = jnp.max(g, axis=0, keepdims=True)
        idx = jnp.min(jnp.where(g == m, row_f, float(LANES)), axis=0, keepdims=True)
        pick = row_f == idx
        bias = jnp.where(pick, 0.0, bias)
        g = jnp.where(pick, NEG, g)
    return bias


def _attn_prompt_kernel(q_ref, k_ref, v_ref, o_ref, kaug_ref, vt_ref, means_ref, *, seq, hd, scale):
    bq = MOBA_BLOCK
    nblk = seq // bq
    nbp = means_ref.shape[0]

    lane = lax.broadcasted_iota(jnp.int32, (bq, hd), 1)
    means_ref[...] = jnp.zeros(means_ref.shape, F32)
    for n in range(nblk):
        kb = k_ref[0, 0, n * bq:(n + 1) * bq, :]
        means_ref[n:n + 1, :] = jnp.sum(kb, axis=0, keepdims=True) * (1.0 / MOBA_BLOCK)
        onehot = jnp.where(lane == n, 1.0, 0.0).astype(BF16)
        kaug_ref[n * bq:(n + 1) * bq, :] = jnp.concatenate([kb.astype(BF16), onehot], axis=1)
        vt_ref[:, n * bq:(n + 1) * bq] = v_ref[0, 0, n * bq:(n + 1) * bq, :].T.astype(BF16)

    kc = ATTN_KEY_CHUNK
    key_idx = lax.broadcasted_iota(jnp.int32, (kc, bq), 0)
    query_idx = lax.broadcasted_iota(jnp.int32, (kc, bq), 1)

    outs = []
    for c in range(nblk):
        qt = q_ref[0, 0, c * bq:(c + 1) * bq, :].T
        gate_t = _dot3(means_ref[...], qt)
        qt_aug = jnp.concatenate(
            [qt * scale, _block_bias_t(gate_t, c), jnp.zeros((hd - nbp, bq), F32)],
            axis=0).astype(BF16)
        m = l = acc = None
        own = list(range(c * bq // kc, (c + 1) * bq // kc))
        for j in own + list(range(c * bq // kc)):
            s = _dot(kaug_ref[j * kc:(j + 1) * kc, :], qt_aug)
            if j in own:
                s = jnp.where(key_idx + (j * kc - c * bq) <= query_idx, s, NEG)
            mn = jnp.max(s, axis=0, keepdims=True)
            p = jnp.exp(s - mn)
            ln = jnp.sum(p, axis=0, keepdims=True)
            on = _dot(vt_ref[:, j * kc:(j + 1) * kc], p.astype(BF16))
            if m is None:
                m, l, acc = mn, ln, on
            else:
                m_new = jnp.maximum(m, mn)
                wa, wb = jnp.exp(m - m_new), jnp.exp(mn - m_new)
                m, l, acc = m_new, l * wa + ln * wb, acc * wa + on * wb
        outs.append((acc * (1.0 / l)).T)
    o_ref[0] = jnp.concatenate(outs, axis=0)


def _attn_prompt(q, k, v):
    b, h, seq, hd = q.shape
    nblk = seq // MOBA_BLOCK
    assert seq % MOBA_BLOCK == 0 and nblk <= hd and hd == LANES
    nbp = -(-nblk // SUBLANES) * SUBLANES
    full = pl.BlockSpec((1, 1, seq, hd), lambda bi, hi: (bi, hi, 0, 0))
    return pl.pallas_call(
        functools.partial(_attn_prompt_kernel, seq=seq, hd=hd, scale=hd ** -0.5),
        grid=(b, h),
        in_specs=[full, full, full],
        out_specs=pl.BlockSpec((1, seq, hd), lambda bi, hi: (bi, 0, hi)),
        out_shape=jax.ShapeDtypeStruct((b, seq, h * hd), F32),
        scratch_shapes=[pltpu.VMEM((seq, 2 * hd), BF16), pltpu.VMEM((hd, seq), BF16),
                        pltpu.VMEM((nbp, hd), F32)],
        compiler_params=_params("parallel", "parallel"),
        name="attn_prompt",
    )(q, k, v)


def _page_means_kernel(pt_ref, *refs, n):
    _page_block_means(refs[:n], refs[n])


def _page_means(cache_k, page_table):
    db, n_pages = page_table.shape
    _, h, _, hd = cache_k.shape
    n = min(PAGES_PER_MEANS_STEP, n_pages)

    def page(j):
        return pl.BlockSpec((1, h, PAGE_SIZE, hd), lambda b, g, pt: (pt[b, g * n + j], 0, 0, 0))

    return pl.pallas_call(
        functools.partial(_page_means_kernel, n=n),
        grid_spec=pltpu.PrefetchScalarGridSpec(
            num_scalar_prefetch=1,
            grid=(db, n_pages // n),
            in_specs=[page(j) for j in range(n)],
            out_specs=pl.BlockSpec((1, n // 2, h * hd), lambda b, g, pt: (b, g, 0)),
        ),
        out_shape=jax.ShapeDtypeStruct((db, n_pages // 2, h * hd), F32),
        compiler_params=_params("parallel", "arbitrary"),
        name="page_means",
    )(page_table, *([cache_k] * n))


def _select_kernel(q_ref, means_ref, pt_ref, sel_ref):
    n_blocks = means_ref.shape[1]
    means = jnp.concatenate(
        [means_ref[0], jnp.zeros((LANES - n_blocks, means_ref.shape[2]), F32)], axis=0)
    gate = _dot3(q_ref[0], means, _dot_nt)
    lane = lax.broadcasted_iota(jnp.int32, gate.shape, 1)
    lane_f = lane.astype(F32)
    pages = pt_ref[0]
    g = jnp.where(lane < n_blocks, gate, NEG)
    out = jnp.zeros(gate.shape, F32)
    for j in range(MOBA_TOP_K):
        m = jnp.max(g, axis=1, keepdims=True)
        idx = jnp.min(jnp.where(g == m, lane_f, float(LANES)), axis=1, keepdims=True)
        for p in range(PAGES_PER_BLOCK):
            phys = jnp.sum(jnp.where(lane_f == idx * PAGES_PER_BLOCK + p, pages, 0.0),
                           axis=1, keepdims=True)
            out = jnp.where(lane == j * PAGES_PER_BLOCK + p, phys, out)
        g = jnp.where(lane_f == idx, NEG, g)
    sel_ref[0] = out.astype(jnp.int32)


def _select(q_bd, means, pages_f):
    db, r, w = q_bd.shape
    nb = means.shape[1]
    assert MOBA_TOP_K <= nb and nb * PAGES_PER_BLOCK <= LANES
    return pl.pallas_call(
        _select_kernel,
        grid=(db,),
        in_specs=[pl.BlockSpec((1, r, w), lambda b: (b, 0, 0)),
                  pl.BlockSpec((1, nb, w), lambda b: (b, 0, 0)),
                  pl.BlockSpec((1, 1, LANES), lambda b: (b, 0, 0))],
        out_specs=pl.BlockSpec((1, r, LANES), lambda b: (b, 0, 0)),
        out_shape=jax.ShapeDtypeStruct((db, r, LANES), jnp.int32),
        compiler_params=_params("parallel"),
        name="select_blocks",
    )(q_bd, means, pages_f)


def _attn_paged_kernel(sel_ref, q_ref, kn_ref, vn_ref, ck_ref, cv_ref, o_ref, kbuf, vbuf, sems,
                       *, n_slots, sq, scale):
    n_heads = pl.num_programs(1)
    step = pl.program_id(0) * n_heads + pl.program_id(1)
    cur = step % 2

    def page_copies(b, h, buf, page_of_slot):
        out = []
        for s in range(n_slots):
            page = page_of_slot(b, h, s)
            out.append(pltpu.make_async_copy(ck_ref.at[page, h], kbuf.at[buf, s], sems.at[0, buf]))
            out.append(pltpu.make_async_copy(cv_ref.at[page, h], vbuf.at[buf, s], sems.at[1, buf]))
        return out

    def selected(b, h, s):
        return sel_ref[b, h * n_slots + s]

    def start_all(copies):
        for i, c in enumerate(copies):
            c.start(priority=i % 2)

    @pl.when(step == 0)
    def _():
        start_all(page_copies(pl.program_id(0), pl.program_id(1), cur, selected))

    @pl.when(step + 1 < pl.num_programs(0) * n_heads)
    def _():
        nxt = step + 1
        start_all(page_copies(nxt // n_heads, nxt % n_heads, 1 - cur, selected))

    for c in page_copies(0, 0, cur, lambda b, h, s: 0):
        c.wait()

    q = q_ref[0, 0]
    kn, vn = kn_ref[0, 0], vn_ref[0, 0]
    hd = q.shape[1]
    k_all = kbuf[cur].reshape(n_slots * PAGE_SIZE, hd).astype(BF16)
    v_all = vbuf[cur].reshape(n_slots * PAGE_SIZE, hd).astype(BF16)
    s = _dot_nt((q * scale).astype(BF16), k_all)
    keys_per_q = MOBA_TOP_K * MOBA_BLOCK
    row = lax.broadcasted_iota(jnp.int32, s.shape, 0)
    col = lax.broadcasted_iota(jnp.int32, s.shape, 1)
    s = jnp.where((col >= row * keys_per_q) & (col < (row + 1) * keys_per_q), s, NEG)
    row1 = lax.broadcasted_iota(jnp.int32, (q.shape[0], 1), 0)
    s_new = [jnp.where(row1 >= j, jnp.sum(q * kn[j:j + 1, :], axis=1, keepdims=True) * scale, NEG)
             for j in range(sq)]
    m = functools.reduce(jnp.maximum, s_new + [jnp.max(s, axis=1, keepdims=True)])
    p = jnp.exp(s - m)
    p_new = [jnp.exp(sj - m) for sj in s_new]
    l = functools.reduce(lambda x, y: x + y, p_new + [jnp.sum(p, axis=1, keepdims=True)])
    acc = functools.reduce(lambda x, y: x + y,
                           [p_new[j] * vn[j:j + 1, :] for j in range(sq)]
                           + [_dot(p.astype(BF16), v_all)])
    o_ref[0, 0] = acc / l


def _attn_paged(q, k_new, v_new, cache_k, cache_v, sel, sq):
    db, h, rows, hd = q.shape
    n_slots = sq * MOBA_TOP_K * PAGES_PER_BLOCK
    per_head = pl.BlockSpec((1, 1, rows, hd), lambda b, hi, sel: (b, hi, 0, 0))
    hbm = pl.BlockSpec(memory_space=pl.ANY)
    return pl.pallas_call(
        functools.partial(_attn_paged_kernel, n_slots=n_slots, sq=sq, scale=hd ** -0.5),
        grid_spec=pltpu.PrefetchScalarGridSpec(
            num_scalar_prefetch=1,
            grid=(db, h),
            in_specs=[per_head, per_head, per_head, hbm, hbm],
            out_specs=per_head,
            scratch_shapes=[pltpu.VMEM((2, n_slots, PAGE_SIZE, hd), F32),
                            pltpu.VMEM((2, n_slots, PAGE_SIZE, hd), F32),
                            pltpu.SemaphoreType.DMA((2, 2))],
        ),
        out_shape=jax.ShapeDtypeStruct((db, h, rows, hd), F32),
        compiler_params=_params("arbitrary", "arbitrary"),
        name="attn_paged",
    )(sel, q, k_new, v_new, cache_k, cache_v)


def _rope_tables(pos, hd):
    half = hd // 8
    inv = ROPE_THETA ** (-jnp.arange(half, dtype=F32) * 2.0 / (2 * half))
    ang = pos.astype(F32)[:, None] * inv[None, :]
    cos, sin = jnp.cos(ang), jnp.sin(ang)
    n = pos.shape[0]
    ones = jnp.ones((n, hd - 2 * half), F32)
    zeros = jnp.zeros((n, hd - 2 * half), F32)
    return (jnp.concatenate([cos, cos, ones], axis=1),
            jnp.concatenate([-sin, sin, zeros], axis=1))


def _block_diag_gate(w_gate):
    n, k, _ = w_gate.shape
    per = GATE_GROUP // k
    eye = jnp.eye(per, dtype=w_gate.dtype)

    def build(w):
        w = w.reshape(n // per, per, k, k)
        return jnp.einsum("gmkj,mn->gmknj", w, eye).reshape(n // per, GATE_GROUP, GATE_GROUP).astype(BF16)

    return build(w_gate[:, :, :k]), build(w_gate[:, :, k:])


def kernel(x_prompt, x_sample, c_prompt, c_sample, state_conv, state_rglru, cache_k, cache_v, page_table, g_norm, w_ada, b_ada, w_rec_in, conv_w, conv_b, w_gate, b_gate, lam, w_rec_out, g_kv, w_ada_kv, b_ada_kv, w_kv, g_k, w_q, g_q, w_o, w_up, w_down):
    bp, sp, d = x_prompt.shape
    db, sq, _ = x_sample.shape
    depth = g_norm.shape[0]
    n_a = w_rec_in.shape[0]
    dr = w_rec_in.shape[2] // 2
    hd = g_k.shape[0]
    n_heads = w_q.shape[2] // hd
    n_pages = page_table.shape[1]
    d_ff = w_up.shape[2]
    assert hd == LANES and n_heads == N_HEADS and sq <= SUBLANES and n_pages % PAGES_PER_BLOCK == 0

    c_all = jnp.concatenate([c_prompt, c_sample], axis=0)
    pad = (-c_all.shape[0]) % SUBLANES
    c_all = jnp.pad(c_all, ((0, pad), (0, 0)))
    mods = _adaln(c_all, w_ada.reshape(depth * 2, d, 3 * d), b_ada.reshape(depth * 2, 1, 3 * d))
    mods_kv = _adaln(c_all, w_ada_kv[None], b_ada_kv[None, None])[0]

    def group_mods(m, prompt):
        if prompt:
            return m[:bp, None, :]
        return jnp.repeat(m[bp:bp + db], sq, axis=0)[None]

    w_in16 = w_rec_in.astype(BF16)
    w_out16 = w_rec_out.astype(BF16)
    w_kv16 = _with_partner_columns(w_kv.astype(BF16), n_heads, hd)
    w_q16 = [_with_partner_columns(w_q[j].astype(BF16), n_heads, hd) for j in range(depth - n_a)]
    w_o16 = w_o.astype(BF16)
    w_up16, w_down16 = w_up.astype(BF16), w_down.astype(BF16)
    gates = [_block_diag_gate(w_gate[l]) for l in range(n_a)]
    pages_f = jnp.pad(page_table.astype(F32), ((0, 0), (0, LANES - n_pages)))[:, None, :]

    prompt_tm = 512
    steps = bp * sp // prompt_tm
    gps = n_pages // PAGES_PER_MEANS_STEP
    seqs_per_call = steps // gps if gps and steps % gps == 0 else 0
    fuse_means = (n_pages % PAGES_PER_MEANS_STEP == 0 and seqs_per_call > 0
                  and db % seqs_per_call == 0 and db // seqs_per_call <= depth)
    means_parts = []

    def run_group(x3, prompt):
        nb, seq, _ = x3.shape
        t = nb * seq
        x = x3.reshape(t, d)
        tm = prompt_tm if prompt else t
        tps = seq // tm if prompt else 1
        if prompt:
            pos = jnp.arange(seq, dtype=jnp.int32)
            tables = _rope_tables(pos, hd)
            tab_nb, tab_seq = nb, seq
        else:
            pos = n_pages * PAGE_SIZE + jnp.arange(seq, dtype=jnp.int32)
            tables = tuple(jnp.tile(tb, (nb, 1)) for tb in _rope_tables(pos, hd))
            tab_nb, tab_seq = 1, t

        def pad_rows(a):
            a = a.reshape(n_heads, nb, seq, hd).transpose(1, 0, 2, 3)
            return jnp.pad(a, ((0, 0), (0, 0), (0, SUBLANES - seq), (0, 0)))

        conv_new, h_new = [], []
        k = v = means = k8 = v8 = None
        for l in range(depth):
            mod = group_mods(mods[2 * l], prompt)
            g1 = g_norm[l, 0][None]
            if l == n_a:
                k, v = _head_proj(x, group_mods(mods_kv, prompt), g_kv[None], w_kv16, g_k[None],
                                  tables, tab_nb, tab_seq, tm, n_heads, n_heads, hd)
                if not prompt:
                    means = (jnp.concatenate(means_parts, axis=0) if fuse_means
                             else _page_means(cache_k, page_table))
                    k8, v8 = pad_rows(k), pad_rows(v)
            if l < n_a:
                ga, xb = _rec_in(x, mod, g1, w_in16[l], tm, tps)
                wr, wi = gates[l]
                cw, cb, bg, lm = conv_w[l], conv_b[l][None], b_gate[l], lam[l][None]
                if prompt:
                    st8 = jnp.zeros((nb, SUBLANES, dr), F32)
                    h0 = jnp.zeros((nb, 1, dr), F32)
                    z, tail, h_last = _scan_prompt(xb, ga, st8, h0, cw, cb, wr, wi, bg, lm,
                                                   nb, seq, 256)
                    conv_new.append(tail[:, SUBLANES - (CONV_WIDTH - 1):])
                    h_new.append(h_last[:, 0])
                else:
                    xb_t = xb.reshape(nb, seq, dr).transpose(1, 0, 2)
                    ga_t = ga.reshape(nb, seq, dr).transpose(1, 0, 2)
                    st = state_conv[l].transpose(1, 0, 2)
                    z_t, h_last = _scan_steps(xb_t, ga_t, st, state_rglru[l], cw, cb, wr, wi, bg, lm)
                    z = z_t.transpose(1, 0, 2).reshape(t, dr)
                    hist = jnp.concatenate([state_conv[l], xb.reshape(nb, seq, dr)], axis=1)
                    conv_new.append(hist[:, -(CONV_WIDTH - 1):])
                    h_new.append(h_last)
                w_mix = w_out16[l]
            else:
                j = l - n_a
                (q,) = _head_proj(x, mod, g1, w_q16[j], g_q[j][None], tables, tab_nb, tab_seq, tm,
                                  n_heads, 0, hd)
                if prompt:
                    z = _attn_prompt(q, k, v).reshape(t, n_heads * hd)
                else:
                    q8 = pad_rows(q)
                    q_bd = jnp.einsum("bhqd,hg->bhqgd", q8, jnp.eye(n_heads, dtype=F32))
                    sel = _select(q_bd.reshape(nb, n_heads * SUBLANES, n_heads * hd), means, pages_f)
                    sel = sel.reshape(nb, n_heads, SUBLANES, LANES)
                    sel = sel[:, :, :seq, :MOBA_TOP_K * PAGES_PER_BLOCK].reshape(nb, -1)
                    z = _attn_paged(q8, k8, v8, cache_k, cache_v, sel, seq)
                    z = z[:, :, :seq].transpose(0, 2, 1, 3).reshape(t, n_heads * hd)
                w_mix = w_o16[j]
            mod2 = group_mods(mods[2 * l + 1], prompt)
            side = None
            if prompt and fuse_means and l * seqs_per_call < db:
                side = (cache_k, page_table, l * seqs_per_call)
            x = _mix_mlp(x, z, mod, w_mix, mod2, g_norm[l, 1][None], w_up16[l], w_down16[l],
                         tm, tps, side)
            if side is not None:
                x, part = x
                means_parts.append(part)
        if not prompt:
            k = k.reshape(n_heads, nb, seq, hd).transpose(1, 0, 2, 3)
            v = v.reshape(n_heads, nb, seq, hd).transpose(1, 0, 2, 3)
        return x.reshape(nb, seq, d), jnp.stack(conv_new), jnp.stack(h_new), k, v

    y_p, conv_p, h_p, k_p, v_p = run_group(x_prompt, True)
    y_s, conv_s, h_s, k_s, v_s = run_group(x_sample, False)
    return (y_p, y_s, conv_p, h_p, conv_s, h_s, k_p, v_p, k_s, v_s)
```

```python
import functools
import math

import jax
import jax.numpy as jnp
from jax import lax
from jax.experimental import pallas as pl
from jax.experimental.pallas import tpu as pltpu

F32 = jnp.float32
BF16 = jnp.bfloat16

NORM_EPS = 1e-6
LRU_C = 8.0
CONV_WIDTH = 4
PAGE_SIZE = 128
MOBA_BLOCK = 256
MOBA_TOP_K = 3
PAGES_PER_BLOCK = MOBA_BLOCK // PAGE_SIZE
N_HEADS = 8
ROPE_THETA = 500000.0
NEG = -1e30
LOG2_E = math.log2(math.e)
LANES = 128
SUBLANES = 8
GATE_GROUP = 256
VMEM_LIMIT = 56 * 1024 * 1024


def _params(*sem):
    return pltpu.CompilerParams(dimension_semantics=sem, vmem_limit_bytes=VMEM_LIMIT)


def _resident(shape):
    nd = len(shape)
    return pl.BlockSpec(shape, lambda *_: (0,) * nd, pipeline_mode=pl.Buffered(1))


def _dot(a, b):
    return jnp.dot(a, b, preferred_element_type=F32)


def _dot_nt(a, b):
    return lax.dot_general(a, b, (((1,), (1,)), ((), ())), preferred_element_type=F32)


def _split(a):
    hi = a.astype(BF16)
    return hi, (a - hi.astype(F32)).astype(BF16)


def _dot3(a, b, dot=_dot):
    ah, al = _split(a)
    bh, bl = _split(b)
    return dot(ah, bh) + (dot(ah, bl) + dot(al, bh))


def _sigmoid(x):
    return 0.5 + 0.5 * jnp.tanh(0.5 * x)


def _gelu_tanh(x):
    return 0.5 * x * (1.0 + jnp.tanh(math.sqrt(2.0 / math.pi) * (x + 0.044715 * (x * x * x))))


def _softplus(x):
    return jnp.maximum(x, 0.0) + jnp.log1p(jnp.exp(-jnp.abs(x)))


def _modulate(x, g, mod, d):
    ms = jnp.mean(x * x, axis=-1, keepdims=True)
    y = x * lax.rsqrt(ms + NORM_EPS) * g
    return y * (1.0 + mod[:, d:2 * d]) + mod[:, :d]


def _adaln_kernel(c_ref, w_ref, b_ref, o_ref):
    c = c_ref[...]
    a = c * _sigmoid(c)
    r = a.shape[0]
    a_hi = a.astype(BF16).astype(F32)
    w_hi, w_lo = _split(w_ref[0])
    stacked = _dot(jnp.concatenate([a_hi, a - a_hi], axis=0).astype(BF16), w_hi)
    o_ref[0] = stacked[:r] + (stacked[r:] + _dot(a_hi.astype(BF16), w_lo)) + b_ref[0]


def _adaln(c, w, b):
    r, d = c.shape
    m, _, n = w.shape
    tn = 1024
    return pl.pallas_call(
        _adaln_kernel,
        grid=(m, n // tn),
        in_specs=[
            pl.BlockSpec((r, d), lambda i, j: (0, 0)),
            pl.BlockSpec((1, d, tn), lambda i, j: (i, 0, j)),
            pl.BlockSpec((1, 1, tn), lambda i, j: (i, 0, j)),
        ],
        out_specs=pl.BlockSpec((1, r, tn), lambda i, j: (i, 0, j)),
        out_shape=jax.ShapeDtypeStruct((m, r, n), F32),
        compiler_params=_params("parallel", "parallel"),
        name="adaln",
    )(c, w, b)


def _mod_spec(mod, tm, tiles_per_seq):
    nb, rows, w = mod.shape
    if rows == 1:
        return pl.BlockSpec((1, 1, w), lambda i, *_: (i // tiles_per_seq, 0, 0))
    return pl.BlockSpec((1, tm, w), lambda i, *_: (0, i, 0))


def _rec_in_kernel(x_ref, mod_ref, g_ref, w_ref, ga_ref, xb_ref, *, d, dr):
    h = _modulate(x_ref[...], g_ref[...], mod_ref[0], d).astype(BF16)
    y = _dot(h, w_ref[...])
    ga_ref[...] = _gelu_tanh(y[:, :dr])
    xb_ref[...] = y[:, dr:]


def _rec_in(x, mod, g, w, tm, tiles_per_seq):
    t, d = x.shape
    dr = w.shape[1] // 2
    row = pl.BlockSpec((tm, dr), lambda i: (i, 0))
    return pl.pallas_call(
        functools.partial(_rec_in_kernel, d=d, dr=dr),
        grid=(t // tm,),
        in_specs=[
            pl.BlockSpec((tm, d), lambda i: (i, 0)),
            _mod_spec(mod, tm, tiles_per_seq),
            _resident((1, d)),
            _resident(w.shape),
        ],
        out_specs=[row, row],
        out_shape=[jax.ShapeDtypeStruct((t, dr), F32)] * 2,
        compiler_params=_params("parallel"),
        name="rec_in",
    )(x, mod, g, w)


def _rope_partner(hd):
    half = hd // 8
    i = jnp.arange(hd)
    return jnp.where(i < half, i + half, jnp.where(i < 2 * half, i - half, i))


def _with_partner_columns(w, n_rope, hd):
    d = w.shape[0]
    wr = w[:, :n_rope * hd].reshape(d, n_rope, hd)[:, :, _rope_partner(hd)]
    return jnp.concatenate([w, wr.reshape(d, n_rope * hd)], axis=1)


def _head_proj_kernel(x_ref, mod_ref, g_ref, w_ref, gh_ref, ghp_ref, cos_ref, sin_ref, *out_refs,
                      d, hd, n_rope, n_plain):
    h_ref = out_refs[-1]
    h_ref[...] = _modulate(x_ref[...], g_ref[...], mod_ref[0], d).astype(BF16)
    cg = cos_ref[...] * gh_ref[...]
    sg = sin_ref[...] * ghp_ref[...]
    partner0 = n_rope + n_plain
    roped, plain = [], []
    for i0 in range(0, n_rope + n_plain, 2):
        y = _dot(h_ref[...], w_ref[:, i0 * hd:(i0 + 2) * hd])
        if i0 < n_rope:
            yp = _dot(h_ref[...], w_ref[:, (partner0 + i0) * hd:(partner0 + i0 + 2) * hd])
        for i in (i0, i0 + 1):
            cols = slice((i - i0) * hd, (i - i0 + 1) * hd)
            if i >= n_rope:
                plain.append(y[:, cols])
                continue
            yh = y[:, cols]
            rs = lax.rsqrt(jnp.mean(yh * yh, axis=-1, keepdims=True) + NORM_EPS)
            roped.append(rs * (yh * cg + yp[:, cols] * sg))
    out_refs[0][0] = jnp.stack(roped, axis=0)
    if plain:
        out_refs[1][0] = jnp.stack(plain, axis=0)


def _head_proj(x, mod, g, w, gh, tables, nb, seq, tm, n_rope, n_plain, hd):
    t, d = x.shape
    assert n_rope % 2 == 0 and n_plain % 2 == 0
    tps = seq // tm
    ghp = gh[:, _rope_partner(hd)]
    tab = pl.BlockSpec((tm, hd), lambda i: (i % tps, 0))
    out_specs = [pl.BlockSpec((1, n_rope, tm, hd), lambda i: (i // tps, 0, i % tps, 0))]
    out_shape = [jax.ShapeDtypeStruct((nb, n_rope, seq, hd), F32)]
    if n_plain:
        out_specs.append(pl.BlockSpec((1, n_plain, tm, hd), lambda i: (i // tps, 0, i % tps, 0)))
        out_shape.append(jax.ShapeDtypeStruct((nb, n_plain, seq, hd), F32))
    return pl.pallas_call(
        functools.partial(_head_proj_kernel, d=d, hd=hd, n_rope=n_rope, n_plain=n_plain),
        grid=(t // tm,),
        in_specs=[
            pl.BlockSpec((tm, d), lambda i: (i, 0)),
            _mod_spec(mod, tm, tps),
            _resident((1, d)),
            _resident(w.shape),
            _resident((1, hd)),
            _resident((1, hd)),
            tab, tab,
        ],
        out_specs=out_specs,
        out_shape=out_shape,
        scratch_shapes=[pltpu.VMEM((tm, d), BF16)],
        compiler_params=_params("parallel"),
        name="head_proj",
    )(x, mod, g, w, gh, ghp, *tables)


FF_CHUNK = 1024


PAGES_PER_MEANS_STEP = 16


def _page_block_means(k_refs, o_ref):
    n = len(k_refs)
    n_heads, _, hd = k_refs[0].shape[1:]
    for h in range(n_heads):
        sums = [jnp.sum(k_refs[j][0, h], axis=0, keepdims=True) for j in range(n)]
        o_ref[0, :, h * hd:(h + 1) * hd] = jnp.concatenate(
            [(sums[2 * i] + sums[2 * i + 1]) * (1.0 / MOBA_BLOCK) for i in range(n // 2)], axis=0)


def _mix_mlp_kernel(*refs, d, n_side):
    if n_side:
        refs = refs[1:]
    x_ref, z_ref, mod1_ref, wo_ref, mod2_ref, g_ref, wu_ref, wd_ref = refs[:8]
    o_ref = refs[8 + n_side]
    x1_ref, h_ref, acc_ref = refs[-3:]
    if n_side:
        _page_block_means(refs[8:8 + n_side], refs[9 + n_side])
    x1 = x_ref[...] + mod1_ref[0][:, 2 * d:] * _dot(z_ref[...].astype(BF16), wo_ref[...])
    x1_ref[...] = x1
    mod2 = mod2_ref[0]
    h_ref[...] = _modulate(x1, g_ref[...], mod2, d).astype(BF16)
    for c in range(wu_ref.shape[1] // FF_CHUNK):
        cols = slice(c * FF_CHUNK, (c + 1) * FF_CHUNK)
        u = _dot(h_ref[...], wu_ref[:, cols])
        a = jnp.square(jnp.maximum(u, 0.0)).astype(BF16)
        part = _dot(a, wd_ref[cols, :])
        if c == 0:
            acc_ref[...] = part
        else:
            acc_ref[...] += part
    o_ref[...] = x1_ref[...] + mod2[:, 2 * d:] * acc_ref[...]


def _mix_mlp(x, z, mod1, wo, mod2, g, wu, wd, tm, tiles_per_seq, side=None):
    t, d = x.shape
    assert wu.shape[1] % FF_CHUNK == 0
    steps = t // tm
    row = pl.BlockSpec((tm, d), lambda i, *_: (i, 0))
    in_specs = [
        row,
        pl.BlockSpec((tm, z.shape[1]), lambda i, *_: (i, 0)),
        _mod_spec(mod1, tm, tiles_per_seq),
        _resident(wo.shape),
        _mod_spec(mod2, tm, tiles_per_seq),
        _resident((1, d)),
        _resident(wu.shape),
        _resident(wd.shape),
    ]
    scratch = [pltpu.VMEM((tm, d), F32), pltpu.VMEM((tm, d), BF16), pltpu.VMEM((tm, d), F32)]
    out_specs, out_shape = row, jax.ShapeDtypeStruct((t, d), F32)
    args = (x, z, mod1, wo, mod2, g, wu, wd)
    if side is None:
        return pl.pallas_call(
            functools.partial(_mix_mlp_kernel, d=d, n_side=0),
            grid=(steps,), in_specs=in_specs, out_specs=out_specs, out_shape=out_shape,
            scratch_shapes=scratch, compiler_params=_params("parallel"), name="mix_mlp",
        )(*args)
    cache_k, page_table, first_seq = side
    n = PAGES_PER_MEANS_STEP
    n_pages = page_table.shape[1]
    _, h, _, hd = cache_k.shape
    gps = n_pages // n

    def page(j):
        return pl.BlockSpec((1, h, PAGE_SIZE, hd),
                            lambda i, pt: (pt[first_seq + i // gps, (i % gps) * n + j], 0, 0, 0))

    return pl.pallas_call(
        functools.partial(_mix_mlp_kernel, d=d, n_side=n),
        grid_spec=pltpu.PrefetchScalarGridSpec(
            num_scalar_prefetch=1,
            grid=(steps,),
            in_specs=in_specs + [page(j) for j in range(n)],
            out_specs=[out_specs,
                       pl.BlockSpec((1, n // PAGES_PER_BLOCK, h * hd),
                                    lambda i, pt: (i // gps, i % gps, 0))],
            scratch_shapes=scratch,
        ),
        out_shape=[out_shape,
                   jax.ShapeDtypeStruct((steps // gps, n_pages // PAGES_PER_BLOCK, h * hd), F32)],
        compiler_params=_params("parallel"),
        name="mix_mlp_means",
    )(page_table, *args, *([cache_k] * n))


def _lru_inputs(xc, wr_ref, wi_ref, bg_ref, sp):
    xh = xc.astype(BF16)
    gw = wr_ref.shape[1]
    r = jnp.concatenate([_dot(xh[:, j * gw:(j + 1) * gw], wr_ref[j])
                         for j in range(wr_ref.shape[0])], axis=1) + bg_ref[0:1, :]
    i = jnp.concatenate([_dot(xh[:, j * gw:(j + 1) * gw], wi_ref[j])
                         for j in range(wi_ref.shape[0])], axis=1) + bg_ref[1:2, :]
    log_a = -LRU_C * _sigmoid(r) * sp
    t = jnp.tanh(0.5 * log_a)
    rc = 1.0 / (1.0 - t)
    a = (1.0 + t) * rc
    u = (2.0 * rc) * jnp.sqrt(-t) * (_sigmoid(i) * xc)
    return a, u


def _scan_prompt_kernel(xb_ref, ga_ref, st_ref, h0_ref, cw_ref, cb_ref, wr_ref, wi_ref, bg_ref,
                        lam_ref, z_ref, tail_out_ref, hl_ref, tail_ref, hc_ref, a_ref, u_ref,
                        *, tt):
    ti = pl.program_id(1)
    dr = xb_ref.shape[1]

    @pl.when(ti == 0)
    def _():
        tail_ref[...] = st_ref[0]
        hc_ref[...] = h0_ref[0]

    x = xb_ref[...]
    tail = tail_ref[...]
    row8 = lax.broadcasted_iota(jnp.int32, (SUBLANES, dr), 0)
    xc = cb_ref[...] + cw_ref[CONV_WIDTH - 1:CONV_WIDTH, :] * x
    for k in range(1, CONV_WIDTH):
        xs = pltpu.roll(x, k, 0)
        first = jnp.where(row8 < k, pltpu.roll(tail, k, 0), xs[:SUBLANES])
        xs = jnp.concatenate([first, xs[SUBLANES:]], axis=0)
        xc = xc + cw_ref[CONV_WIDTH - 1 - k:CONV_WIDTH - k, :] * xs
    tail_ref[...] = x[tt - SUBLANES:]
    tail_out_ref[0] = x[tt - SUBLANES:]

    a, u = _lru_inputs(xc, wr_ref, wi_ref, bg_ref, _softplus(-lam_ref[...]))
    a_ref[...] = a
    u_ref[...] = u

    def group(gi, carry):
        r0 = pl.multiple_of(gi * SUBLANES, SUBLANES)
        a8 = a_ref[pl.ds(r0, SUBLANES), :]
        u8 = u_ref[pl.ds(r0, SUBLANES), :]
        for s in (1, 2, 4):
            keep = row8 >= s
            u8 = jnp.where(keep, a8 * pltpu.roll(u8, s, 0) + u8, u8)
            a8 = jnp.where(keep, a8 * pltpu.roll(a8, s, 0), a8)
        h8 = a8 * carry + u8
        z_ref[pl.ds(r0, SUBLANES), :] = h8 * ga_ref[pl.ds(r0, SUBLANES), :]
        return jnp.broadcast_to(h8[SUBLANES - 1:SUBLANES, :], (SUBLANES, dr))

    carry = lax.fori_loop(0, tt // SUBLANES, group,
                          jnp.broadcast_to(hc_ref[...], (SUBLANES, dr)), unroll=4)
    hc_ref[...] = carry[0:1, :]
    hl_ref[0] = carry[0:1, :]


def _scan_prompt(xb, ga, st8, h0, cw, cb, wr, wi, bg, lam, nb, seq, tt):
    t, dr = xb.shape
    nt = seq // tt
    row = pl.BlockSpec((tt, dr), lambda b, i: (b * nt + i, 0))
    per_seq8 = pl.BlockSpec((1, SUBLANES, dr), lambda b, i: (b, 0, 0))
    per_seq1 = pl.BlockSpec((1, 1, dr), lambda b, i: (b, 0, 0))
    return pl.pallas_call(
        functools.partial(_scan_prompt_kernel, tt=tt),
        grid=(nb, nt),
        in_specs=[row, row, per_seq8, per_seq1, _resident(cw.shape), _resident(cb.shape),
                  _resident(wr.shape), _resident(wi.shape), _resident(bg.shape),
                  _resident(lam.shape)],
        out_specs=[row, per_seq8, per_seq1],
        out_shape=[jax.ShapeDtypeStruct((t, dr), F32),
                   jax.ShapeDtypeStruct((nb, SUBLANES, dr), F32),
                   jax.ShapeDtypeStruct((nb, 1, dr), F32)],
        scratch_shapes=[pltpu.VMEM((SUBLANES, dr), F32), pltpu.VMEM((1, dr), F32),
                        pltpu.VMEM((tt, dr), F32), pltpu.VMEM((tt, dr), F32)],
        compiler_params=_params("parallel", "arbitrary"),
        name="scan_prompt",
    )(xb, ga, st8, h0, cw, cb, wr, wi, bg, lam)


def _scan_steps_kernel(xb_ref, ga_ref, st_ref, h0_ref, cw_ref, cb_ref, wr_ref, wi_ref, bg_ref,
                       lam_ref, z_ref, hl_ref, *, steps):
    rows = [st_ref[j] for j in range(CONV_WIDTH - 1)] + [xb_ref[t] for t in range(steps)]
    sp = _softplus(-lam_ref[...])
    h = h0_ref[...]
    for t in range(steps):
        xc = cb_ref[...]
        for j in range(CONV_WIDTH):
            xc = xc + cw_ref[j:j + 1, :] * rows[t + j]
        a, u = _lru_inputs(xc, wr_ref, wi_ref, bg_ref, sp)
        h = a * h + u
        z_ref[t] = h * ga_ref[t]
    hl_ref[...] = h


def _scan_steps(xb, ga, st, h0, cw, cb, wr, wi, bg, lam):
    steps, nb, dr = xb.shape
    args = (xb, ga, st, h0, cw, cb, wr, wi, bg, lam)
    return pl.pallas_call(
        functools.partial(_scan_steps_kernel, steps=steps),
        grid=(1,),
        in_specs=[_resident(a.shape) for a in args],
        out_specs=[pl.BlockSpec((steps, nb, dr), lambda i: (0, 0, 0)),
                   pl.BlockSpec((nb, dr), lambda i: (0, 0))],
        out_shape=[jax.ShapeDtypeStruct((steps, nb, dr), F32),
                   jax.ShapeDtypeStruct((nb, dr), F32)],
        compiler_params=_params("arbitrary"),
        name="scan_steps",
    )(*args)


ATTN_KEY_CHUNK = 128


def _block_bias_t(gate_t, cur):
    row = lax.broadcasted_iota(jnp.int32, gate_t.shape, 0)
    row_f = row.astype(F32)
    g = jnp.where(row < cur, gate_t, NEG)
    bias = jnp.where(row == cur, 0.0, NEG)
    for _ in range(min(MOBA_TOP_K, cur)):
        m = jnp.max(g, axis=0, keepdims=True)
        idx = jnp.min(jnp.where(g == m, row_f, float(LANES)), axis=0, keepdims=True)
        pick = row_f == idx
        bias = jnp.where(pick, 0.0, bias)
        g = jnp.where(pick, NEG, g)
    return bias


def _attn_prompt_kernel(q_ref, k_ref, v_ref, o_ref, kaug_ref, vt_ref, means_ref, *, seq, hd, scale):
    bq = MOBA_BLOCK
    nblk = seq // bq
    nbp = means_ref.shape[0]

    lane = lax.broadcasted_iota(jnp.int32, (bq, hd), 1)
    means_ref[...] = jnp.zeros(means_ref.shape, F32)
    for n in range(nblk):
        kb = k_ref[0, 0, n * bq:(n + 1) * bq, :]
        means_ref[n:n + 1, :] = jnp.sum(kb, axis=0, keepdims=True) * (1.0 / MOBA_BLOCK)
        onehot = jnp.where(lane == n, 1.0, 0.0).astype(BF16)
        kaug_ref[n * bq:(n + 1) * bq, :] = jnp.concatenate([kb.astype(BF16), onehot], axis=1)
        vt_ref[:, n * bq:(n + 1) * bq] = v_ref[0, 0, n * bq:(n + 1) * bq, :].T.astype(BF16)

    kc = ATTN_KEY_CHUNK
    key_idx = lax.broadcasted_iota(jnp.int32, (kc, bq), 0)
    query_idx = lax.broadcasted_iota(jnp.int32, (kc, bq), 1)

    outs = []
    for c in range(nblk):
        qt = q_ref[0, 0, c * bq:(c + 1) * bq, :].T
        gate_t = _dot3(means_ref[...], qt)
        qt_aug = jnp.concatenate(
            [qt * (scale * LOG2_E), _block_bias_t(gate_t, c), jnp.zeros((hd - nbp, bq), F32)],
            axis=0).astype(BF16)
        m = l = acc = None
        own = list(range(c * bq // kc, (c + 1) * bq // kc))
        for j in own + list(range(c * bq // kc)):
            s = _dot(kaug_ref[j * kc:(j + 1) * kc, :], qt_aug)
            if j in own:
                s = jnp.where(key_idx + (j * kc - c * bq) <= query_idx, s, NEG)
            mn = jnp.max(s, axis=0, keepdims=True)
            p = jnp.exp2(s - mn)
            ln = jnp.sum(p, axis=0, keepdims=True)
            on = _dot(vt_ref[:, j * kc:(j + 1) * kc], p.astype(BF16))
            if m is None:
                m, l, acc = mn, ln, on
            else:
                m_new = jnp.maximum(m, mn)
                wa, wb = jnp.exp2(m - m_new), jnp.exp2(mn - m_new)
                m, l, acc = m_new, l * wa + ln * wb, acc * wa + on * wb
        outs.append((acc * (1.0 / l)).T)
    o_ref[0] = jnp.concatenate(outs, axis=0)


def _attn_prompt(q, k, v):
    b, h, seq, hd = q.shape
    nblk = seq // MOBA_BLOCK
    assert seq % MOBA_BLOCK == 0 and nblk <= hd and hd == LANES
    nbp = -(-nblk // SUBLANES) * SUBLANES
    full = pl.BlockSpec((1, 1, seq, hd), lambda bi, hi: (bi, hi, 0, 0))
    return pl.pallas_call(
        functools.partial(_attn_prompt_kernel, seq=seq, hd=hd, scale=hd ** -0.5),
        grid=(b, h),
        in_specs=[full, full, full],
        out_specs=pl.BlockSpec((1, seq, hd), lambda bi, hi: (bi, 0, hi)),
        out_shape=jax.ShapeDtypeStruct((b, seq, h * hd), F32),
        scratch_shapes=[pltpu.VMEM((seq, 2 * hd), BF16), pltpu.VMEM((hd, seq), BF16),
                        pltpu.VMEM((nbp, hd), F32)],
        compiler_params=_params("parallel", "parallel"),
        name="attn_prompt",
    )(q, k, v)


def _page_means_kernel(pt_ref, *refs, n):
    _page_block_means(refs[:n], refs[n])


def _page_means(cache_k, page_table):
    db, n_pages = page_table.shape
    _, h, _, hd = cache_k.shape
    n = min(PAGES_PER_MEANS_STEP, n_pages)

    def page(j):
        return pl.BlockSpec((1, h, PAGE_SIZE, hd), lambda b, g, pt: (pt[b, g * n + j], 0, 0, 0))

    return pl.pallas_call(
        functools.partial(_page_means_kernel, n=n),
        grid_spec=pltpu.PrefetchScalarGridSpec(
            num_scalar_prefetch=1,
            grid=(db, n_pages // n),
            in_specs=[page(j) for j in range(n)],
            out_specs=pl.BlockSpec((1, n // 2, h * hd), lambda b, g, pt: (b, g, 0)),
        ),
        out_shape=jax.ShapeDtypeStruct((db, n_pages // 2, h * hd), F32),
        compiler_params=_params("parallel", "arbitrary"),
        name="page_means",
    )(page_table, *([cache_k] * n))


def _select_kernel(q_ref, means_ref, pt_ref, sel_ref):
    n_blocks = means_ref.shape[1]
    outs = []
    for i in range(q_ref.shape[0]):
        means = jnp.concatenate(
            [means_ref[i], jnp.zeros((LANES - n_blocks, means_ref.shape[2]), F32)], axis=0)
        gate = _dot3(q_ref[i], means, _dot_nt)
        lane = lax.broadcasted_iota(jnp.int32, gate.shape, 1)
        lane_f = lane.astype(F32)
        pages = pt_ref[i]
        g = jnp.where(lane < n_blocks, gate, NEG)
        out = jnp.zeros(gate.shape, F32)
        for j in range(MOBA_TOP_K):
            m = jnp.max(g, axis=1, keepdims=True)
            idx = jnp.min(jnp.where(g == m, lane_f, float(LANES)), axis=1, keepdims=True)
            for p in range(PAGES_PER_BLOCK):
                phys = jnp.sum(jnp.where(lane_f == idx * PAGES_PER_BLOCK + p, pages, 0.0),
                               axis=1, keepdims=True)
                out = jnp.where(lane == j * PAGES_PER_BLOCK + p, phys, out)
            g = jnp.where(lane_f == idx, NEG, g)
        outs.append(out.astype(jnp.int32))
    sel_ref[...] = jnp.stack(outs, axis=0)


def _select(q_bd, means, pages_f):
    db, r, w = q_bd.shape
    nb = means.shape[1]
    assert MOBA_TOP_K <= nb and nb * PAGES_PER_BLOCK <= LANES
    g = math.gcd(db, 8)
    return pl.pallas_call(
        _select_kernel,
        grid=(db // g,),
        in_specs=[pl.BlockSpec((g, r, w), lambda b: (b, 0, 0)),
                  pl.BlockSpec((g, nb, w), lambda b: (b, 0, 0)),
                  pl.BlockSpec((g, 1, LANES), lambda b: (b, 0, 0))],
        out_specs=pl.BlockSpec((g, r, LANES), lambda b: (b, 0, 0)),
        out_shape=jax.ShapeDtypeStruct((db, r, LANES), jnp.int32),
        compiler_params=_params("parallel"),
        name="select_blocks",
    )(q_bd, means, pages_f)


def _attn_paged_kernel(sel_ref, q_ref, kn_ref, vn_ref, ck_ref, cv_ref, o_ref, kbuf, vbuf, sems,
                       *, n_slots, sq, scale):
    b = pl.program_id(0)
    n_heads = q_ref.shape[1]
    assert n_heads % 2 == 0

    def page_copies(seq, h, page_of_slot):
        out = []
        for s in range(n_slots):
            row = page_of_slot(seq, h * n_slots + s)
            out.append(pltpu.make_async_copy(ck_ref.at[row], kbuf.at[h % 2, s], sems.at[0, h % 2]))
            out.append(pltpu.make_async_copy(cv_ref.at[row], vbuf.at[h % 2, s], sems.at[1, h % 2]))
        return out

    def start_all(copies):
        for i, c in enumerate(copies):
            c.start(priority=i % 2)

    @pl.when(b == 0)
    def _():
        start_all(page_copies(b, 0, lambda seq, i: sel_ref[seq, i]))

    for h in range(n_heads):
        if h + 1 < n_heads:
            start_all(page_copies(b, h + 1, lambda seq, i: sel_ref[seq, i]))
        else:
            @pl.when(b + 1 < pl.num_programs(0))
            def _():
                start_all(page_copies(b + 1, 0, lambda seq, i: sel_ref[seq, i]))
        for c in page_copies(0, h, lambda seq, i: 0):
            c.wait()
        o_ref[0, h] = _paged_head(q_ref[0, h], kn_ref[0, h], vn_ref[0, h], kbuf[h % 2],
                                  vbuf[h % 2], sq, scale)


def _paged_head(q, kn, vn, k_pages, v_pages, sq, scale):
    n_slots, _, hd = k_pages.shape
    k_all = k_pages.reshape(n_slots * PAGE_SIZE, hd).astype(BF16)
    v_all = v_pages.reshape(n_slots * PAGE_SIZE, hd).astype(BF16)
    s = _dot_nt((q * scale).astype(BF16), k_all)
    keys_per_q = MOBA_TOP_K * MOBA_BLOCK
    row = lax.broadcasted_iota(jnp.int32, s.shape, 0)
    col = lax.broadcasted_iota(jnp.int32, s.shape, 1)
    s = jnp.where((col >= row * keys_per_q) & (col < (row + 1) * keys_per_q), s, NEG)
    row1 = lax.broadcasted_iota(jnp.int32, (q.shape[0], 1), 0)
    s_new = [jnp.where(row1 >= j, jnp.sum(q * kn[j:j + 1, :], axis=1, keepdims=True) * scale, NEG)
             for j in range(sq)]
    m = functools.reduce(jnp.maximum, s_new + [jnp.max(s, axis=1, keepdims=True)])
    p = jnp.exp(s - m)
    p_new = [jnp.exp(sj - m) for sj in s_new]
    l = functools.reduce(lambda x, y: x + y, p_new + [jnp.sum(p, axis=1, keepdims=True)])
    acc = functools.reduce(lambda x, y: x + y,
                           [p_new[j] * vn[j:j + 1, :] for j in range(sq)]
                           + [_dot(p.astype(BF16), v_all)])
    return acc / l


def _attn_paged(q, k_new, v_new, cache_k, cache_v, sel, sq):
    db, h, rows, hd = q.shape
    n_slots = sq * MOBA_TOP_K * PAGES_PER_BLOCK
    per_seq = pl.BlockSpec((1, h, rows, hd), lambda b, sel: (b, 0, 0, 0))
    hbm = pl.BlockSpec(memory_space=pl.ANY)
    head_of_slot = jnp.repeat(jnp.arange(h, dtype=jnp.int32), n_slots)
    sel_rows = sel * h + head_of_slot[None, :]
    return pl.pallas_call(
        functools.partial(_attn_paged_kernel, n_slots=n_slots, sq=sq, scale=hd ** -0.5),
        grid_spec=pltpu.PrefetchScalarGridSpec(
            num_scalar_prefetch=1,
            grid=(db,),
            in_specs=[per_seq, per_seq, per_seq, hbm, hbm],
            out_specs=per_seq,
            scratch_shapes=[pltpu.VMEM((2, n_slots, PAGE_SIZE, hd), F32),
                            pltpu.VMEM((2, n_slots, PAGE_SIZE, hd), F32),
                            pltpu.SemaphoreType.DMA((2, 2))],
        ),
        out_shape=jax.ShapeDtypeStruct((db, h, rows, hd), F32),
        compiler_params=_params("arbitrary"),
        name="attn_paged",
    )(sel_rows, q, k_new, v_new, cache_k.reshape(-1, PAGE_SIZE, hd),
      cache_v.reshape(-1, PAGE_SIZE, hd))


def _rope_tables(pos, hd):
    half = hd // 8
    inv = ROPE_THETA ** (-jnp.arange(half, dtype=F32) * 2.0 / (2 * half))
    ang = pos.astype(F32)[:, None] * inv[None, :]
    cos, sin = jnp.cos(ang), jnp.sin(ang)
    n = pos.shape[0]
    ones = jnp.ones((n, hd - 2 * half), F32)
    zeros = jnp.zeros((n, hd - 2 * half), F32)
    return (jnp.concatenate([cos, cos, ones], axis=1),
            jnp.concatenate([-sin, sin, zeros], axis=1))


def _block_diag_gate(w_gate):
    n, k, _ = w_gate.shape
    per = GATE_GROUP // k
    eye = jnp.eye(per, dtype=w_gate.dtype)

    def build(w):
        w = w.reshape(n // per, per, k, k)
        return jnp.einsum("gmkj,mn->gmknj", w, eye).reshape(n // per, GATE_GROUP, GATE_GROUP).astype(BF16)

    return build(w_gate[:, :, :k]), build(w_gate[:, :, k:])


def kernel(x_prompt, x_sample, c_prompt, c_sample, state_conv, state_rglru, cache_k, cache_v, page_table, g_norm, w_ada, b_ada, w_rec_in, conv_w, conv_b, w_gate, b_gate, lam, w_rec_out, g_kv, w_ada_kv, b_ada_kv, w_kv, g_k, w_q, g_q, w_o, w_up, w_down):
    bp, sp, d = x_prompt.shape
    db, sq, _ = x_sample.shape
    depth = g_norm.shape[0]
    n_a = w_rec_in.shape[0]
    dr = w_rec_in.shape[2] // 2
    hd = g_k.shape[0]
    n_heads = w_q.shape[2] // hd
    n_pages = page_table.shape[1]
    d_ff = w_up.shape[2]
    assert hd == LANES and n_heads == N_HEADS and sq <= SUBLANES and n_pages % PAGES_PER_BLOCK == 0

    c_all = jnp.concatenate([c_prompt, c_sample], axis=0)
    pad = (-c_all.shape[0]) % SUBLANES
    c_all = jnp.pad(c_all, ((0, pad), (0, 0)))
    mods = _adaln(c_all, w_ada.reshape(depth * 2, d, 3 * d), b_ada.reshape(depth * 2, 1, 3 * d))
    mods_kv = _adaln(c_all, w_ada_kv[None], b_ada_kv[None, None])[0]

    def group_mods(m, prompt):
        if prompt:
            return m[:bp, None, :]
        return jnp.repeat(m[bp:bp + db], sq, axis=0)[None]

    w_in16 = [w_rec_in[l].astype(BF16) for l in range(n_a)]
    w_out16 = [w_rec_out[l].astype(BF16) for l in range(n_a)]
    w_kv16 = _with_partner_columns(w_kv.astype(BF16), n_heads, hd)
    w_q16 = [_with_partner_columns(w_q[j].astype(BF16), n_heads, hd) for j in range(depth - n_a)]
    w_o16 = [w_o[j].astype(BF16) for j in range(depth - n_a)]
    w_up16 = [w_up[l].astype(BF16) for l in range(depth)]
    w_down16 = [w_down[l].astype(BF16) for l in range(depth)]
    gates = [_block_diag_gate(w_gate[l]) for l in range(n_a)]
    pages_f = jnp.pad(page_table.astype(F32), ((0, 0), (0, LANES - n_pages)))[:, None, :]

    prompt_tm = 512
    steps = bp * sp // prompt_tm
    gps = n_pages // PAGES_PER_MEANS_STEP
    seqs_per_call = steps // gps if gps and steps % gps == 0 else 0
    fuse_means = (n_pages % PAGES_PER_MEANS_STEP == 0 and seqs_per_call > 0
                  and db % seqs_per_call == 0 and db // seqs_per_call <= depth)
    means_parts = []

    def run_group(x3, prompt):
        nb, seq, _ = x3.shape
        t = nb * seq
        x = x3.reshape(t, d)
        tm = prompt_tm if prompt else t
        tps = seq // tm if prompt else 1
        if prompt:
            pos = jnp.arange(seq, dtype=jnp.int32)
            tables = _rope_tables(pos, hd)
            tab_nb, tab_seq = nb, seq
        else:
            pos = n_pages * PAGE_SIZE + jnp.arange(seq, dtype=jnp.int32)
            tables = tuple(jnp.tile(tb, (nb, 1)) for tb in _rope_tables(pos, hd))
            tab_nb, tab_seq = 1, t

        def pad_rows(a):
            a = a.reshape(n_heads, nb, seq, hd).transpose(1, 0, 2, 3)
            return jnp.pad(a, ((0, 0), (0, 0), (0, SUBLANES - seq), (0, 0)))

        conv_new, h_new = [], []
        k = v = means = k8 = v8 = None
        for l in range(depth):
            mod = group_mods(mods[2 * l], prompt)
            g1 = g_norm[l, 0][None]
            if l == n_a:
                k, v = _head_proj(x, group_mods(mods_kv, prompt), g_kv[None], w_kv16, g_k[None],
                                  tables, tab_nb, tab_seq, tm, n_heads, n_heads, hd)
                if not prompt:
                    means = (jnp.concatenate(means_parts, axis=0) if fuse_means
                             else _page_means(cache_k, page_table))
                    k8, v8 = pad_rows(k), pad_rows(v)
            if l < n_a:
                ga, xb = _rec_in(x, mod, g1, w_in16[l], tm, tps)
                wr, wi = gates[l]
                cw, cb, bg, lm = conv_w[l], conv_b[l][None], b_gate[l], lam[l][None]
                if prompt:
                    st8 = jnp.zeros((nb, SUBLANES, dr), F32)
                    h0 = jnp.zeros((nb, 1, dr), F32)
                    z, tail, h_last = _scan_prompt(xb, ga, st8, h0, cw, cb, wr, wi, bg, lm,
                                                   nb, seq, 256)
                    conv_new.append(tail[:, SUBLANES - (CONV_WIDTH - 1):])
                    h_new.append(h_last[:, 0])
                else:
                    xb_t = xb.reshape(nb, seq, dr).transpose(1, 0, 2)
                    ga_t = ga.reshape(nb, seq, dr).transpose(1, 0, 2)
                    st = state_conv[l].transpose(1, 0, 2)
                    z_t, h_last = _scan_steps(xb_t, ga_t, st, state_rglru[l], cw, cb, wr, wi, bg, lm)
                    z = z_t.transpose(1, 0, 2).reshape(t, dr)
                    hist = jnp.concatenate([state_conv[l], xb.reshape(nb, seq, dr)], axis=1)
                    conv_new.append(hist[:, -(CONV_WIDTH - 1):])
                    h_new.append(h_last)
                w_mix = w_out16[l]
            else:
                j = l - n_a
                (q,) = _head_proj(x, mod, g1, w_q16[j], g_q[j][None], tables, tab_nb, tab_seq, tm,
                                  n_heads, 0, hd)
                if prompt:
                    z = _attn_prompt(q, k, v).reshape(t, n_heads * hd)
                else:
                    q8 = pad_rows(q)
                    q_bd = jnp.einsum("bhqd,hg->bhqgd", q8, jnp.eye(n_heads, dtype=F32))
                    sel = _select(q_bd.reshape(nb, n_heads * SUBLANES, n_heads * hd), means, pages_f)
                    sel = sel.reshape(nb, n_heads, SUBLANES, LANES)
                    sel = sel[:, :, :seq, :MOBA_TOP_K * PAGES_PER_BLOCK].reshape(nb, -1)
                    z = _attn_paged(q8, k8, v8, cache_k, cache_v, sel, seq)
                    z = z[:, :, :seq].transpose(0, 2, 1, 3).reshape(t, n_heads * hd)
                w_mix = w_o16[j]
            mod2 = group_mods(mods[2 * l + 1], prompt)
            side = None
            if prompt and fuse_means and l * seqs_per_call < db:
                side = (cache_k, page_table, l * seqs_per_call)
            x = _mix_mlp(x, z, mod, w_mix, mod2, g_norm[l, 1][None], w_up16[l], w_down16[l],
                         tm, tps, side)
            if side is not None:
                x, part = x
                means_parts.append(part)
        if not prompt:
            k = k.reshape(n_heads, nb, seq, hd).transpose(1, 0, 2, 3)
            v = v.reshape(n_heads, nb, seq, hd).transpose(1, 0, 2, 3)
        return x.reshape(nb, seq, d), jnp.stack(conv_new), jnp.stack(h_new), k, v

    y_p, conv_p, h_p, k_p, v_p = run_group(x_prompt, True)
    y_s, conv_s, h_s, k_s, v_s = run_group(x_sample, False)
    return (y_p, y_s, conv_p, h_p, conv_s, h_s, k_p, v_p, k_s, v_s)
```

```python
import functools
import math

import jax
import jax.numpy as jnp
from jax import lax
from jax.experimental import pallas as pl
from jax.experimental.pallas import tpu as pltpu

F32 = jnp.float32
BF16 = jnp.bfloat16

NORM_EPS = 1e-6
LRU_C = 8.0
CONV_WIDTH = 4
PAGE_SIZE = 128
MOBA_BLOCK = 256
MOBA_TOP_K = 3
PAGES_PER_BLOCK = MOBA_BLOCK // PAGE_SIZE
N_HEADS = 8
ROPE_THETA = 500000.0
NEG = -1e30
LOG2_E = math.log2(math.e)
LANES = 128
SUBLANES = 8
GATE_GROUP = 256
VMEM_LIMIT = 56 * 1024 * 1024


def _params(*sem):
    return pltpu.CompilerParams(dimension_semantics=sem, vmem_limit_bytes=VMEM_LIMIT)


def _resident(shape):
    nd = len(shape)
    return pl.BlockSpec(shape, lambda *_: (0,) * nd, pipeline_mode=pl.Buffered(1))


def _dot(a, b):
    return jnp.dot(a, b, preferred_element_type=F32)


def _dot_nt(a, b):
    return lax.dot_general(a, b, (((1,), (1,)), ((), ())), preferred_element_type=F32)


def _split(a):
    hi = a.astype(BF16)
    return hi, (a - hi.astype(F32)).astype(BF16)


def _dot3(a, b, dot=_dot):
    ah, al = _split(a)
    bh, bl = _split(b)
    return dot(ah, bh) + (dot(ah, bl) + dot(al, bh))


def _sigmoid(x):
    return 0.5 + 0.5 * jnp.tanh(0.5 * x)


def _gelu_tanh(x):
    return 0.5 * x * (1.0 + jnp.tanh(math.sqrt(2.0 / math.pi) * (x + 0.044715 * (x * x * x))))


def _softplus(x):
    return jnp.maximum(x, 0.0) + jnp.log1p(jnp.exp(-jnp.abs(x)))


def _modulate(x, g, mod, d):
    ms = jnp.mean(x * x, axis=-1, keepdims=True)
    y = x * lax.rsqrt(ms + NORM_EPS) * g
    return y * (1.0 + mod[:, d:2 * d]) + mod[:, :d]


def _adaln_kernel(c_ref, w_ref, b_ref, o_ref):
    c = c_ref[...]
    a = c * _sigmoid(c)
    r = a.shape[0]
    a_hi = a.astype(BF16).astype(F32)
    w_hi, w_lo = _split(w_ref[0])
    stacked = _dot(jnp.concatenate([a_hi, a - a_hi], axis=0).astype(BF16), w_hi)
    o_ref[0] = stacked[:r] + (stacked[r:] + _dot(a_hi.astype(BF16), w_lo)) + b_ref[0]


def _adaln(c, w, b):
    r, d = c.shape
    m, _, n = w.shape
    tn = 1024
    return pl.pallas_call(
        _adaln_kernel,
        grid=(m, n // tn),
        in_specs=[
            pl.BlockSpec((r, d), lambda i, j: (0, 0)),
            pl.BlockSpec((1, d, tn), lambda i, j: (i, 0, j)),
            pl.BlockSpec((1, 1, tn), lambda i, j: (i, 0, j)),
        ],
        out_specs=pl.BlockSpec((1, r, tn), lambda i, j: (i, 0, j)),
        out_shape=jax.ShapeDtypeStruct((m, r, n), F32),
        compiler_params=_params("parallel", "parallel"),
        name="adaln",
    )(c, w, b)


def _mod_spec(mod, tm, tiles_per_seq):
    nb, rows, w = mod.shape
    if rows == 1:
        return pl.BlockSpec((1, 1, w), lambda i, *_: (i // tiles_per_seq, 0, 0))
    return pl.BlockSpec((1, tm, w), lambda i, *_: (0, i, 0))


def _rec_in_kernel(x_ref, mod_ref, g_ref, w_ref, ga_ref, xb_ref, *, d, dr):
    h = _modulate(x_ref[...], g_ref[...], mod_ref[0], d).astype(BF16)
    y = _dot(h, w_ref[...])
    ga_ref[...] = _gelu_tanh(y[:, :dr])
    xb_ref[...] = y[:, dr:]


def _rec_in(x, mod, g, w, tm, tiles_per_seq):
    t, d = x.shape
    dr = w.shape[1] // 2
    row = pl.BlockSpec((tm, dr), lambda i: (i, 0))
    return pl.pallas_call(
        functools.partial(_rec_in_kernel, d=d, dr=dr),
        grid=(t // tm,),
        in_specs=[
            pl.BlockSpec((tm, d), lambda i: (i, 0)),
            _mod_spec(mod, tm, tiles_per_seq),
            _resident((1, d)),
            _resident(w.shape),
        ],
        out_specs=[row, row],
        out_shape=[jax.ShapeDtypeStruct((t, dr), F32)] * 2,
        compiler_params=_params("parallel"),
        name="rec_in",
    )(x, mod, g, w)


def _rope_partner(hd):
    half = hd // 8
    i = jnp.arange(hd)
    return jnp.where(i < half, i + half, jnp.where(i < 2 * half, i - half, i))


def _with_partner_columns(w, n_rope, hd):
    d = w.shape[0]
    wr = w[:, :n_rope * hd].reshape(d, n_rope, hd)[:, :, _rope_partner(hd)]
    return jnp.concatenate([w, wr.reshape(d, n_rope * hd)], axis=1)


def _head_proj_kernel(x_ref, mod_ref, g_ref, w_ref, gh_ref, ghp_ref, cos_ref, sin_ref, *out_refs,
                      d, hd, n_rope, n_plain):
    h_ref = out_refs[-1]
    h_ref[...] = _modulate(x_ref[...], g_ref[...], mod_ref[0], d).astype(BF16)
    cg = cos_ref[...] * gh_ref[...]
    sg = sin_ref[...] * ghp_ref[...]
    partner0 = n_rope + n_plain
    roped, plain = [], []
    for i0 in range(0, n_rope + n_plain, 2):
        y = _dot(h_ref[...], w_ref[:, i0 * hd:(i0 + 2) * hd])
        if i0 < n_rope:
            yp = _dot(h_ref[...], w_ref[:, (partner0 + i0) * hd:(partner0 + i0 + 2) * hd])
        for i in (i0, i0 + 1):
            cols = slice((i - i0) * hd, (i - i0 + 1) * hd)
            if i >= n_rope:
                plain.append(y[:, cols])
                continue
            yh = y[:, cols]
            rs = lax.rsqrt(jnp.mean(yh * yh, axis=-1, keepdims=True) + NORM_EPS)
            roped.append(rs * (yh * cg + yp[:, cols] * sg))
    out_refs[0][0] = jnp.stack(roped, axis=0)
    if plain:
        out_refs[1][0] = jnp.stack(plain, axis=0)


def _head_proj(x, mod, g, w, gh, tables, nb, seq, tm, n_rope, n_plain, hd):
    t, d = x.shape
    assert n_rope % 2 == 0 and n_plain % 2 == 0
    tps = seq // tm
    ghp = gh[:, _rope_partner(hd)]
    tab = pl.BlockSpec((tm, hd), lambda i: (i % tps, 0))
    out_specs = [pl.BlockSpec((1, n_rope, tm, hd), lambda i: (i // tps, 0, i % tps, 0))]
    out_shape = [jax.ShapeDtypeStruct((nb, n_rope, seq, hd), F32)]
    if n_plain:
        out_specs.append(pl.BlockSpec((1, n_plain, tm, hd), lambda i: (i // tps, 0, i % tps, 0)))
        out_shape.append(jax.ShapeDtypeStruct((nb, n_plain, seq, hd), F32))
    return pl.pallas_call(
        functools.partial(_head_proj_kernel, d=d, hd=hd, n_rope=n_rope, n_plain=n_plain),
        grid=(t // tm,),
        in_specs=[
            pl.BlockSpec((tm, d), lambda i: (i, 0)),
            _mod_spec(mod, tm, tps),
            _resident((1, d)),
            _resident(w.shape),
            _resident((1, hd)),
            _resident((1, hd)),
            tab, tab,
        ],
        out_specs=out_specs,
        out_shape=out_shape,
        scratch_shapes=[pltpu.VMEM((tm, d), BF16)],
        compiler_params=_params("parallel"),
        name="head_proj",
    )(x, mod, g, w, gh, ghp, *tables)


FF_CHUNK = 1024


PAGES_PER_MEANS_STEP = 16


def _page_block_means(k_refs, o_ref):
    n = len(k_refs)
    n_heads, _, hd = k_refs[0].shape[1:]
    for h in range(n_heads):
        sums = [jnp.sum(k_refs[j][0, h], axis=0, keepdims=True) for j in range(n)]
        o_ref[0, :, h * hd:(h + 1) * hd] = jnp.concatenate(
            [(sums[2 * i] + sums[2 * i + 1]) * (1.0 / MOBA_BLOCK) for i in range(n // 2)], axis=0)


def _mix_mlp_kernel(*refs, d, attn):
    if attn:
        sel_ref, refs = refs[0], refs[1:]
        _paged_heads_step(sel_ref, *refs[8:13], refs[14], *refs[18:21], **attn)
        refs = refs[:8] + (refs[13],) + refs[15:18]
    x_ref, z_ref, mod1_ref, wo_ref, mod2_ref, g_ref, wu_ref, wd_ref, o_ref = refs[:9]
    x1_ref, h_ref, acc_ref = refs[-3:]
    x1 = x_ref[...] + mod1_ref[0][:, 2 * d:] * _dot(z_ref[...].astype(BF16), wo_ref[...])
    x1_ref[...] = x1
    mod2 = mod2_ref[0]
    h_ref[...] = _modulate(x1, g_ref[...], mod2, d).astype(BF16)
    for c in range(wu_ref.shape[1] // FF_CHUNK):
        cols = slice(c * FF_CHUNK, (c + 1) * FF_CHUNK)
        u = _dot(h_ref[...], wu_ref[:, cols])
        a = jnp.square(jnp.maximum(u, 0.0)).astype(BF16)
        part = _dot(a, wd_ref[cols, :])
        if c == 0:
            acc_ref[...] = part
        else:
            acc_ref[...] += part
    o_ref[...] = x1_ref[...] + mod2[:, 2 * d:] * acc_ref[...]


def _mix_mlp(x, z, mod1, wo, mod2, g, wu, wd, tm, tiles_per_seq, hosted=None):
    t, d = x.shape
    assert wu.shape[1] % FF_CHUNK == 0
    steps = t // tm
    row = pl.BlockSpec((tm, d), lambda i, *_: (i, 0))
    in_specs = [
        row,
        pl.BlockSpec((tm, z.shape[1]), lambda i, *_: (i, 0)),
        _mod_spec(mod1, tm, tiles_per_seq),
        _resident(wo.shape),
        _mod_spec(mod2, tm, tiles_per_seq),
        _resident((1, d)),
        _resident(wu.shape),
        _resident(wd.shape),
    ]
    scratch = [pltpu.VMEM((tm, d), F32), pltpu.VMEM((tm, d), BF16), pltpu.VMEM((tm, d), F32)]
    out_specs, out_shape = row, jax.ShapeDtypeStruct((t, d), F32)
    args = (x, z, mod1, wo, mod2, g, wu, wd)
    if hosted is None:
        return pl.pallas_call(
            functools.partial(_mix_mlp_kernel, d=d, attn=None),
            grid=(steps,), in_specs=in_specs, out_specs=out_specs, out_shape=out_shape,
            scratch_shapes=scratch, compiler_params=_params("parallel"), name="mix_mlp",
        )(*args)
    q, k_new, v_new, cache_k, cache_v, sel, sq = hosted
    db, h, rows, hd = q.shape
    assert steps % db == 0 and h % (steps // db) == 0
    spq = steps // db
    hps = h // spq
    n_slots = sq * MOBA_TOP_K * PAGES_PER_BLOCK
    heads = pl.BlockSpec((1, hps, rows, hd), lambda i, sel: (i // spq, i % spq, 0, 0))
    hbm = pl.BlockSpec(memory_space=pl.ANY)
    return pl.pallas_call(
        functools.partial(_mix_mlp_kernel, d=d,
                          attn=dict(n_slots=n_slots, sq=sq, scale=hd ** -0.5, n_heads=h)),
        grid_spec=pltpu.PrefetchScalarGridSpec(
            num_scalar_prefetch=1,
            grid=(steps,),
            in_specs=in_specs + [heads, heads, heads, hbm, hbm],
            out_specs=[out_specs, heads],
            scratch_shapes=scratch + [pltpu.VMEM((hps, n_slots, PAGE_SIZE, hd), F32),
                                      pltpu.VMEM((hps, n_slots, PAGE_SIZE, hd), F32),
                                      pltpu.SemaphoreType.DMA((2,))],
        ),
        out_shape=[out_shape, jax.ShapeDtypeStruct((db, h, rows, hd), F32)],
        compiler_params=_params("arbitrary"),
        name="mix_mlp_attn",
    )(_page_rows(sel, h, n_slots), *args, q, k_new, v_new,
      cache_k.reshape(-1, PAGE_SIZE, hd), cache_v.reshape(-1, PAGE_SIZE, hd))


def _lru_inputs(xc, wr_ref, wi_ref, bg_ref, sp):
    xh = xc.astype(BF16)
    gw = wr_ref.shape[1]
    r = jnp.concatenate([_dot(xh[:, j * gw:(j + 1) * gw], wr_ref[j])
                         for j in range(wr_ref.shape[0])], axis=1) + bg_ref[0:1, :]
    i = jnp.concatenate([_dot(xh[:, j * gw:(j + 1) * gw], wi_ref[j])
                         for j in range(wi_ref.shape[0])], axis=1) + bg_ref[1:2, :]
    log_a = -LRU_C * _sigmoid(r) * sp
    t = jnp.tanh(0.5 * log_a)
    rc = 1.0 / (1.0 - t)
    a = (1.0 + t) * rc
    u = (2.0 * rc) * jnp.sqrt(-t) * (_sigmoid(i) * xc)
    return a, u


def _scan_prompt_kernel(*refs, tt, n_side):
    if n_side:
        refs = refs[1:]
        _page_block_means(refs[10:10 + n_side], refs[13 + n_side])
    (xb_ref, ga_ref, st_ref, h0_ref, cw_ref, cb_ref, wr_ref, wi_ref, bg_ref,
     lam_ref) = refs[:10]
    z_ref, tail_out_ref, hl_ref = refs[10 + n_side:13 + n_side]
    tail_ref, hc_ref, a_ref, u_ref = refs[-4:]
    ti = pl.program_id(1)
    dr = xb_ref.shape[1]

    @pl.when(ti == 0)
    def _():
        tail_ref[...] = st_ref[0]
        hc_ref[...] = h0_ref[0]

    x = xb_ref[...]
    tail = tail_ref[...]
    row8 = lax.broadcasted_iota(jnp.int32, (SUBLANES, dr), 0)
    xc = cb_ref[...] + cw_ref[CONV_WIDTH - 1:CONV_WIDTH, :] * x
    for k in range(1, CONV_WIDTH):
        xs = pltpu.roll(x, k, 0)
        first = jnp.where(row8 < k, pltpu.roll(tail, k, 0), xs[:SUBLANES])
        xs = jnp.concatenate([first, xs[SUBLANES:]], axis=0)
        xc = xc + cw_ref[CONV_WIDTH - 1 - k:CONV_WIDTH - k, :] * xs
    tail_ref[...] = x[tt - SUBLANES:]
    tail_out_ref[0] = x[tt - SUBLANES:]

    a, u = _lru_inputs(xc, wr_ref, wi_ref, bg_ref, _softplus(-lam_ref[...]))
    a_ref[...] = a
    u_ref[...] = u

    def group(gi, carry):
        r0 = pl.multiple_of(gi * SUBLANES, SUBLANES)
        a8 = a_ref[pl.ds(r0, SUBLANES), :]
        u8 = u_ref[pl.ds(r0, SUBLANES), :]
        for s in (1, 2, 4):
            keep = row8 >= s
            u8 = jnp.where(keep, a8 * pltpu.roll(u8, s, 0) + u8, u8)
            a8 = jnp.where(keep, a8 * pltpu.roll(a8, s, 0), a8)
        h8 = a8 * carry + u8
        z_ref[pl.ds(r0, SUBLANES), :] = h8 * ga_ref[pl.ds(r0, SUBLANES), :]
        return jnp.broadcast_to(h8[SUBLANES - 1:SUBLANES, :], (SUBLANES, dr))

    carry = lax.fori_loop(0, tt // SUBLANES, group,
                          jnp.broadcast_to(hc_ref[...], (SUBLANES, dr)), unroll=4)
    hc_ref[...] = carry[0:1, :]
    hl_ref[0] = carry[0:1, :]


def _scan_prompt(xb, ga, st8, h0, cw, cb, wr, wi, bg, lam, nb, seq, tt, side=None):
    t, dr = xb.shape
    nt = seq // tt
    row = pl.BlockSpec((tt, dr), lambda b, i, *_: (b * nt + i, 0))
    per_seq8 = pl.BlockSpec((1, SUBLANES, dr), lambda b, i, *_: (b, 0, 0))
    per_seq1 = pl.BlockSpec((1, 1, dr), lambda b, i, *_: (b, 0, 0))
    in_specs = [row, row, per_seq8, per_seq1, _resident(cw.shape), _resident(cb.shape),
                _resident(wr.shape), _resident(wi.shape), _resident(bg.shape),
                _resident(lam.shape)]
    out_specs = [row, per_seq8, per_seq1]
    out_shape = [jax.ShapeDtypeStruct((t, dr), F32),
                 jax.ShapeDtypeStruct((nb, SUBLANES, dr), F32),
                 jax.ShapeDtypeStruct((nb, 1, dr), F32)]
    scratch = [pltpu.VMEM((SUBLANES, dr), F32), pltpu.VMEM((1, dr), F32),
               pltpu.VMEM((tt, dr), F32), pltpu.VMEM((tt, dr), F32)]
    args = (xb, ga, st8, h0, cw, cb, wr, wi, bg, lam)
    if side is None:
        return pl.pallas_call(
            functools.partial(_scan_prompt_kernel, tt=tt, n_side=0),
            grid=(nb, nt), in_specs=in_specs, out_specs=out_specs, out_shape=out_shape,
            scratch_shapes=scratch, compiler_params=_params("parallel", "arbitrary"),
            name="scan_prompt",
        )(*args)
    cache_k, page_table, first_seq, n_seqs = side
    n = PAGES_PER_MEANS_STEP
    n_pages = page_table.shape[1]
    _, h, _, hd = cache_k.shape
    gps = n_pages // n
    last = n_seqs * gps - 1

    def group(b, i):
        return jnp.minimum(b * nt + i, last)

    def page(j):
        return pl.BlockSpec(
            (1, h, PAGE_SIZE, hd),
            lambda b, i, pt: (pt[first_seq + group(b, i) // gps, (group(b, i) % gps) * n + j],
                              0, 0, 0))

    return pl.pallas_call(
        functools.partial(_scan_prompt_kernel, tt=tt, n_side=n),
        grid_spec=pltpu.PrefetchScalarGridSpec(
            num_scalar_prefetch=1,
            grid=(nb, nt),
            in_specs=in_specs + [page(j) for j in range(n)],
            out_specs=out_specs + [pl.BlockSpec(
                (1, n // PAGES_PER_BLOCK, h * hd),
                lambda b, i, pt: (group(b, i) // gps, group(b, i) % gps, 0))],
            scratch_shapes=scratch,
        ),
        out_shape=out_shape + [
            jax.ShapeDtypeStruct((n_seqs, n_pages // PAGES_PER_BLOCK, h * hd), F32)],
        compiler_params=_params("arbitrary", "arbitrary"),
        name="scan_prompt_means",
    )(page_table, *args, *([cache_k] * n))


def _scan_steps_kernel(xb_ref, ga_ref, st_ref, h0_ref, cw_ref, cb_ref, wr_ref, wi_ref, bg_ref,
                       lam_ref, z_ref, hl_ref, *, steps):
    rows = [st_ref[j] for j in range(CONV_WIDTH - 1)] + [xb_ref[t] for t in range(steps)]
    sp = _softplus(-lam_ref[...])
    h = h0_ref[...]
    for t in range(steps):
        xc = cb_ref[...]
        for j in range(CONV_WIDTH):
            xc = xc + cw_ref[j:j + 1, :] * rows[t + j]
        a, u = _lru_inputs(xc, wr_ref, wi_ref, bg_ref, sp)
        h = a * h + u
        z_ref[t] = h * ga_ref[t]
    hl_ref[...] = h


def _scan_steps(xb, ga, st, h0, cw, cb, wr, wi, bg, lam):
    steps, nb, dr = xb.shape
    args = (xb, ga, st, h0, cw, cb, wr, wi, bg, lam)
    return pl.pallas_call(
        functools.partial(_scan_steps_kernel, steps=steps),
        grid=(1,),
        in_specs=[_resident(a.shape) for a in args],
        out_specs=[pl.BlockSpec((steps, nb, dr), lambda i: (0, 0, 0)),
                   pl.BlockSpec((nb, dr), lambda i: (0, 0))],
        out_shape=[jax.ShapeDtypeStruct((steps, nb, dr), F32),
                   jax.ShapeDtypeStruct((nb, dr), F32)],
        compiler_params=_params("arbitrary"),
        name="scan_steps",
    )(*args)


ATTN_KEY_CHUNK = 128


def _block_bias_t(gate_t, cur):
    row = lax.broadcasted_iota(jnp.int32, gate_t.shape, 0)
    row_f = row.astype(F32)
    g = jnp.where(row < cur, gate_t, NEG)
    bias = jnp.where(row == cur, 0.0, NEG)
    for _ in range(min(MOBA_TOP_K, cur)):
        m = jnp.max(g, axis=0, keepdims=True)
        idx = jnp.min(jnp.where(g == m, row_f, float(LANES)), axis=0, keepdims=True)
        pick = row_f == idx
        bias = jnp.where(pick, 0.0, bias)
        g = jnp.where(pick, NEG, g)
    return bias


def _attn_prompt_kernel(q_ref, k_ref, v_ref, o_ref, kaug_ref, vt_ref, means_ref, *, seq, hd, scale):
    bq = MOBA_BLOCK
    nblk = seq // bq
    nbp = means_ref.shape[0]

    lane = lax.broadcasted_iota(jnp.int32, (bq, hd), 1)
    means_ref[...] = jnp.zeros(means_ref.shape, F32)
    for n in range(nblk):
        kb = k_ref[0, 0, n * bq:(n + 1) * bq, :]
        means_ref[n:n + 1, :] = jnp.sum(kb, axis=0, keepdims=True) * (1.0 / MOBA_BLOCK)
        onehot = jnp.where(lane == n, 1.0, 0.0).astype(BF16)
        kaug_ref[n * bq:(n + 1) * bq, :] = jnp.concatenate([kb.astype(BF16), onehot], axis=1)
        vt_ref[:, n * bq:(n + 1) * bq] = v_ref[0, 0, n * bq:(n + 1) * bq, :].T.astype(BF16)

    kc = ATTN_KEY_CHUNK
    key_idx = lax.broadcasted_iota(jnp.int32, (kc, bq), 0)
    query_idx = lax.broadcasted_iota(jnp.int32, (kc, bq), 1)

    outs = []
    for c in range(nblk):
        qt = q_ref[0, 0, c * bq:(c + 1) * bq, :].T
        gate_t = _dot3(means_ref[...], qt)
        qt_aug = jnp.concatenate(
            [qt * (scale * LOG2_E), _block_bias_t(gate_t, c), jnp.zeros((hd - nbp, bq), F32)],
            axis=0).astype(BF16)
        m = l = acc = None
        own = list(range(c * bq // kc, (c + 1) * bq // kc))
        for j in own + list(range(c * bq // kc)):
            s = _dot(kaug_ref[j * kc:(j + 1) * kc, :], qt_aug)
            if j in own:
                s = jnp.where(key_idx + (j * kc - c * bq) <= query_idx, s, NEG)
            mn = jnp.max(s, axis=0, keepdims=True)
            p = jnp.exp2(s - mn)
            ln = jnp.sum(p, axis=0, keepdims=True)
            on = _dot(vt_ref[:, j * kc:(j + 1) * kc], p.astype(BF16))
            if m is None:
                m, l, acc = mn, ln, on
            else:
                m_new = jnp.maximum(m, mn)
                wa, wb = jnp.exp2(m - m_new), jnp.exp2(mn - m_new)
                m, l, acc = m_new, l * wa + ln * wb, acc * wa + on * wb
        outs.append((acc * (1.0 / l)).T)
    o_ref[0] = jnp.concatenate(outs, axis=0)


def _attn_prompt(q, k, v):
    b, h, seq, hd = q.shape
    nblk = seq // MOBA_BLOCK
    assert seq % MOBA_BLOCK == 0 and nblk <= hd and hd == LANES
    nbp = -(-nblk // SUBLANES) * SUBLANES
    full = pl.BlockSpec((1, 1, seq, hd), lambda bi, hi: (bi, hi, 0, 0))
    return pl.pallas_call(
        functools.partial(_attn_prompt_kernel, seq=seq, hd=hd, scale=hd ** -0.5),
        grid=(b, h),
        in_specs=[full, full, full],
        out_specs=pl.BlockSpec((1, seq, hd), lambda bi, hi: (bi, 0, hi)),
        out_shape=jax.ShapeDtypeStruct((b, seq, h * hd), F32),
        scratch_shapes=[pltpu.VMEM((seq, 2 * hd), BF16), pltpu.VMEM((hd, seq), BF16),
                        pltpu.VMEM((nbp, hd), F32)],
        compiler_params=_params("parallel", "parallel"),
        name="attn_prompt",
    )(q, k, v)


def _page_means_kernel(pt_ref, *refs, n):
    _page_block_means(refs[:n], refs[n])


def _page_means(cache_k, page_table):
    db, n_pages = page_table.shape
    _, h, _, hd = cache_k.shape
    n = min(PAGES_PER_MEANS_STEP, n_pages)

    def page(j):
        return pl.BlockSpec((1, h, PAGE_SIZE, hd), lambda b, g, pt: (pt[b, g * n + j], 0, 0, 0))

    return pl.pallas_call(
        functools.partial(_page_means_kernel, n=n),
        grid_spec=pltpu.PrefetchScalarGridSpec(
            num_scalar_prefetch=1,
            grid=(db, n_pages // n),
            in_specs=[page(j) for j in range(n)],
            out_specs=pl.BlockSpec((1, n // 2, h * hd), lambda b, g, pt: (b, g, 0)),
        ),
        out_shape=jax.ShapeDtypeStruct((db, n_pages // 2, h * hd), F32),
        compiler_params=_params("parallel", "arbitrary"),
        name="page_means",
    )(page_table, *([cache_k] * n))


def _select_kernel(q_ref, means_ref, pt_ref, sel_ref):
    n_blocks = means_ref.shape[1]
    outs = []
    for i in range(q_ref.shape[0]):
        means = jnp.concatenate(
            [means_ref[i], jnp.zeros((LANES - n_blocks, means_ref.shape[2]), F32)], axis=0)
        gate = _dot3(q_ref[i], means, _dot_nt)
        lane = lax.broadcasted_iota(jnp.int32, gate.shape, 1)
        lane_f = lane.astype(F32)
        pages = pt_ref[i]
        g = jnp.where(lane < n_blocks, gate, NEG)
        out = jnp.zeros(gate.shape, F32)
        for j in range(MOBA_TOP_K):
            m = jnp.max(g, axis=1, keepdims=True)
            idx = jnp.min(jnp.where(g == m, lane_f, float(LANES)), axis=1, keepdims=True)
            for p in range(PAGES_PER_BLOCK):
                phys = jnp.sum(jnp.where(lane_f == idx * PAGES_PER_BLOCK + p, pages, 0.0),
                               axis=1, keepdims=True)
                out = jnp.where(lane == j * PAGES_PER_BLOCK + p, phys, out)
            g = jnp.where(lane_f == idx, NEG, g)
        outs.append(out.astype(jnp.int32))
    sel_ref[...] = jnp.stack(outs, axis=0)


def _select(q_bd, means, pages_f):
    db, r, w = q_bd.shape
    nb = means.shape[1]
    assert MOBA_TOP_K <= nb and nb * PAGES_PER_BLOCK <= LANES
    g = math.gcd(db, 8)
    return pl.pallas_call(
        _select_kernel,
        grid=(db // g,),
        in_specs=[pl.BlockSpec((g, r, w), lambda b: (b, 0, 0)),
                  pl.BlockSpec((g, nb, w), lambda b: (b, 0, 0)),
                  pl.BlockSpec((g, 1, LANES), lambda b: (b, 0, 0))],
        out_specs=pl.BlockSpec((g, r, LANES), lambda b: (b, 0, 0)),
        out_shape=jax.ShapeDtypeStruct((db, r, LANES), jnp.int32),
        compiler_params=_params("parallel"),
        name="select_blocks",
    )(q_bd, means, pages_f)


def _attn_paged_kernel(sel_ref, q_ref, kn_ref, vn_ref, ck_ref, cv_ref, o_ref, kbuf, vbuf, sems,
                       *, n_slots, sq, scale):
    b = pl.program_id(0)
    n_heads = q_ref.shape[1]
    assert n_heads % 2 == 0

    def page_copies(seq, h, page_of_slot):
        out = []
        for s in range(n_slots):
            row = page_of_slot(seq, h * n_slots + s)
            out.append(pltpu.make_async_copy(ck_ref.at[row], kbuf.at[h % 2, s], sems.at[0, h % 2]))
            out.append(pltpu.make_async_copy(cv_ref.at[row], vbuf.at[h % 2, s], sems.at[1, h % 2]))
        return out

    def start_all(copies):
        for i, c in enumerate(copies):
            c.start(priority=i % 2)

    @pl.when(b == 0)
    def _():
        start_all(page_copies(b, 0, lambda seq, i: sel_ref[seq, i]))

    for h in range(n_heads):
        if h + 1 < n_heads:
            start_all(page_copies(b, h + 1, lambda seq, i: sel_ref[seq, i]))
        else:
            @pl.when(b + 1 < pl.num_programs(0))
            def _():
                start_all(page_copies(b + 1, 0, lambda seq, i: sel_ref[seq, i]))
        for c in page_copies(0, h, lambda seq, i: 0):
            c.wait()
        o_ref[0, h] = _paged_head(q_ref[0, h], kn_ref[0, h], vn_ref[0, h], kbuf[h % 2],
                                  vbuf[h % 2], sq, scale)


def _page_rows(sel, n_heads, n_slots):
    head_of_slot = jnp.repeat(jnp.arange(n_heads, dtype=jnp.int32), n_slots)
    return sel * n_heads + head_of_slot[None, :]


def _paged_heads_step(sel_ref, q_ref, kn_ref, vn_ref, ck_ref, cv_ref, o_ref, kbuf, vbuf, sems,
                      *, n_slots, sq, scale, n_heads):
    i = pl.program_id(0)
    hps = q_ref.shape[1]
    spq = n_heads // hps

    def page_copies(step, page_of):
        seq, h0 = step // spq, (step % spq) * hps
        out = []
        for j in range(hps):
            for s in range(n_slots):
                row = page_of(seq, (h0 + j) * n_slots + s)
                out.append(pltpu.make_async_copy(ck_ref.at[row], kbuf.at[j, s], sems.at[0]))
                out.append(pltpu.make_async_copy(cv_ref.at[row], vbuf.at[j, s], sems.at[1]))
        return out

    def start_all(copies):
        for n, c in enumerate(copies):
            c.start(priority=n % 2)

    @pl.when(i == 0)
    def _():
        start_all(page_copies(i, lambda seq, n: sel_ref[seq, n]))

    for c in page_copies(0, lambda seq, n: 0):
        c.wait()
    for j in range(hps):
        o_ref[0, j] = _paged_head(q_ref[0, j], kn_ref[0, j], vn_ref[0, j], kbuf[j], vbuf[j],
                                  sq, scale)

    @pl.when(i + 1 < pl.num_programs(0))
    def _():
        start_all(page_copies(i + 1, lambda seq, n: sel_ref[seq, n]))


def _paged_head(q, kn, vn, k_pages, v_pages, sq, scale):
    n_slots, _, hd = k_pages.shape
    k_all = k_pages.reshape(n_slots * PAGE_SIZE, hd).astype(BF16)
    v_all = v_pages.reshape(n_slots * PAGE_SIZE, hd).astype(BF16)
    s = _dot_nt((q * scale).astype(BF16), k_all)
    keys_per_q = MOBA_TOP_K * MOBA_BLOCK
    row = lax.broadcasted_iota(jnp.int32, s.shape, 0)
    col = lax.broadcasted_iota(jnp.int32, s.shape, 1)
    s = jnp.where((col >= row * keys_per_q) & (col < (row + 1) * keys_per_q), s, NEG)
    row1 = lax.broadcasted_iota(jnp.int32, (q.shape[0], 1), 0)
    s_new = [jnp.where(row1 >= j, jnp.sum(q * kn[j:j + 1, :], axis=1, keepdims=True) * scale, NEG)
             for j in range(sq)]
    m = functools.reduce(jnp.maximum, s_new + [jnp.max(s, axis=1, keepdims=True)])
    p = jnp.exp(s - m)
    p_new = [jnp.exp(sj - m) for sj in s_new]
    l = functools.reduce(lambda x, y: x + y, p_new + [jnp.sum(p, axis=1, keepdims=True)])
    acc = functools.reduce(lambda x, y: x + y,
                           [p_new[j] * vn[j:j + 1, :] for j in range(sq)]
                           + [_dot(p.astype(BF16), v_all)])
    return acc / l


def _attn_paged(q, k_new, v_new, cache_k, cache_v, sel, sq):
    db, h, rows, hd = q.shape
    n_slots = sq * MOBA_TOP_K * PAGES_PER_BLOCK
    per_seq = pl.BlockSpec((1, h, rows, hd), lambda b, sel: (b, 0, 0, 0))
    hbm = pl.BlockSpec(memory_space=pl.ANY)
    sel_rows = _page_rows(sel, h, n_slots)
    return pl.pallas_call(
        functools.partial(_attn_paged_kernel, n_slots=n_slots, sq=sq, scale=hd ** -0.5),
        grid_spec=pltpu.PrefetchScalarGridSpec(
            num_scalar_prefetch=1,
            grid=(db,),
            in_specs=[per_seq, per_seq, per_seq, hbm, hbm],
            out_specs=per_seq,
            scratch_shapes=[pltpu.VMEM((2, n_slots, PAGE_SIZE, hd), F32),
                            pltpu.VMEM((2, n_slots, PAGE_SIZE, hd), F32),
                            pltpu.SemaphoreType.DMA((2, 2))],
        ),
        out_shape=jax.ShapeDtypeStruct((db, h, rows, hd), F32),
        compiler_params=_params("arbitrary"),
        name="attn_paged",
    )(sel_rows, q, k_new, v_new, cache_k.reshape(-1, PAGE_SIZE, hd),
      cache_v.reshape(-1, PAGE_SIZE, hd))


def _rope_tables(pos, hd):
    half = hd // 8
    inv = ROPE_THETA ** (-jnp.arange(half, dtype=F32) * 2.0 / (2 * half))
    ang = pos.astype(F32)[:, None] * inv[None, :]
    cos, sin = jnp.cos(ang), jnp.sin(ang)
    n = pos.shape[0]
    ones = jnp.ones((n, hd - 2 * half), F32)
    zeros = jnp.zeros((n, hd - 2 * half), F32)
    return (jnp.concatenate([cos, cos, ones], axis=1),
            jnp.concatenate([-sin, sin, zeros], axis=1))


def _block_diag_gate(w_gate):
    n, k, _ = w_gate.shape
    per = GATE_GROUP // k
    eye = jnp.eye(per, dtype=w_gate.dtype)

    def build(w):
        w = w.reshape(n // per, per, k, k)
        return jnp.einsum("gmkj,mn->gmknj", w, eye).reshape(n // per, GATE_GROUP, GATE_GROUP).astype(BF16)

    return build(w_gate[:, :, :k]), build(w_gate[:, :, k:])


def kernel(x_prompt, x_sample, c_prompt, c_sample, state_conv, state_rglru, cache_k, cache_v, page_table, g_norm, w_ada, b_ada, w_rec_in, conv_w, conv_b, w_gate, b_gate, lam, w_rec_out, g_kv, w_ada_kv, b_ada_kv, w_kv, g_k, w_q, g_q, w_o, w_up, w_down):
    bp, sp, d = x_prompt.shape
    db, sq, _ = x_sample.shape
    depth = g_norm.shape[0]
    n_a = w_rec_in.shape[0]
    dr = w_rec_in.shape[2] // 2
    hd = g_k.shape[0]
    n_heads = w_q.shape[2] // hd
    n_pages = page_table.shape[1]
    d_ff = w_up.shape[2]
    assert hd == LANES and n_heads == N_HEADS and sq <= SUBLANES and n_pages % PAGES_PER_BLOCK == 0

    c_all = jnp.concatenate([c_prompt, c_sample], axis=0)
    pad = (-c_all.shape[0]) % SUBLANES
    c_all = jnp.pad(c_all, ((0, pad), (0, 0)))
    mods = _adaln(c_all, w_ada.reshape(depth * 2, d, 3 * d), b_ada.reshape(depth * 2, 1, 3 * d))
    mods_kv = _adaln(c_all, w_ada_kv[None], b_ada_kv[None, None])[0]

    def group_mods(m, prompt):
        if prompt:
            return m[:bp, None, :]
        return jnp.repeat(m[bp:bp + db], sq, axis=0)[None]

    w_in16 = [w_rec_in[l].astype(BF16) for l in range(n_a)]
    w_out16 = [w_rec_out[l].astype(BF16) for l in range(n_a)]
    w_kv16 = _with_partner_columns(w_kv.astype(BF16), n_heads, hd)
    w_q16 = [_with_partner_columns(w_q[j].astype(BF16), n_heads, hd) for j in range(depth - n_a)]
    w_o16 = [w_o[j].astype(BF16) for j in range(depth - n_a)]
    w_up16 = [w_up[l].astype(BF16) for l in range(depth)]
    w_down16 = [w_down[l].astype(BF16) for l in range(depth)]
    gates = [_block_diag_gate(w_gate[l]) for l in range(n_a)]
    pages_f = jnp.pad(page_table.astype(F32), ((0, 0), (0, LANES - n_pages)))[:, None, :]

    prompt_tm, scan_tt, host_tm = 512, 256, 256
    scan_steps = bp * (sp // scan_tt)
    gps = n_pages // PAGES_PER_MEANS_STEP
    fuse_means = (n_pages % PAGES_PER_MEANS_STEP == 0 and gps > 0 and scan_steps % gps == 0
                  and n_a * scan_steps >= db * gps)
    host_steps = bp * sp // host_tm
    host_attn = (sp % host_tm == 0 and host_steps % db == 0
                 and n_heads % (host_steps // db) == 0)
    means_parts = []

    def setup(x3, prompt):
        nb, seq, _ = x3.shape
        t = nb * seq
        tm = prompt_tm if prompt else t
        st = dict(prompt=prompt, nb=nb, seq=seq, t=t, x=x3.reshape(t, d), tm=tm,
                  tps=seq // tm if prompt else 1, conv_new=[], h_new=[])
        if prompt:
            pos = jnp.arange(seq, dtype=jnp.int32)
            st.update(tables=_rope_tables(pos, hd), tab_nb=nb, tab_seq=seq)
        else:
            pos = n_pages * PAGE_SIZE + jnp.arange(seq, dtype=jnp.int32)
            st.update(tables=tuple(jnp.tile(tb, (nb, 1)) for tb in _rope_tables(pos, hd)),
                      tab_nb=1, tab_seq=t)
        return st

    def pad_rows(a, nb, seq):
        a = a.reshape(n_heads, nb, seq, hd).transpose(1, 0, 2, 3)
        return jnp.pad(a, ((0, 0), (0, 0), (0, SUBLANES - seq), (0, 0)))

    def unpad_rows(o, nb, seq):
        return o[:, :, :seq].transpose(0, 2, 1, 3).reshape(nb * seq, n_heads * hd)

    def mixer(st, l):
        prompt, nb, seq, t, tm, tps, x = (st[k] for k in ("prompt", "nb", "seq", "t", "tm", "tps", "x"))
        mod = group_mods(mods[2 * l], prompt)
        g1 = g_norm[l, 0][None]
        st.update(mod=mod, paged=None)
        if l == n_a:
            k, v = _head_proj(x, group_mods(mods_kv, prompt), g_kv[None], w_kv16, g_k[None],
                              st["tables"], st["tab_nb"], st["tab_seq"], tm, n_heads, n_heads, hd)
            st.update(k=k, v=v)
            if not prompt:
                st.update(means=(jnp.concatenate(means_parts, axis=0) if fuse_means
                                 else _page_means(cache_k, page_table)),
                          k8=pad_rows(k, nb, seq), v8=pad_rows(v, nb, seq))
        if l < n_a:
            ga, xb = _rec_in(x, mod, g1, w_in16[l], tm, tps)
            wr, wi = gates[l]
            cw, cb, bg, lm = conv_w[l], conv_b[l][None], b_gate[l], lam[l][None]
            if prompt:
                st8 = jnp.zeros((nb, SUBLANES, dr), F32)
                h0 = jnp.zeros((nb, 1, dr), F32)
                side = None
                first_seq = l * scan_steps // gps if fuse_means else db
                if first_seq < db:
                    side = (cache_k, page_table, first_seq, min(scan_steps // gps, db - first_seq))
                z, tail, h_last, *part = _scan_prompt(xb, ga, st8, h0, cw, cb, wr, wi, bg, lm,
                                                      nb, seq, scan_tt, side)
                means_parts.extend(part)
                st["conv_new"].append(tail[:, SUBLANES - (CONV_WIDTH - 1):])
                st["h_new"].append(h_last[:, 0])
            else:
                xb_t = xb.reshape(nb, seq, dr).transpose(1, 0, 2)
                ga_t = ga.reshape(nb, seq, dr).transpose(1, 0, 2)
                conv0 = state_conv[l].transpose(1, 0, 2)
                z_t, h_last = _scan_steps(xb_t, ga_t, conv0, state_rglru[l], cw, cb, wr, wi, bg, lm)
                z = z_t.transpose(1, 0, 2).reshape(t, dr)
                hist = jnp.concatenate([state_conv[l], xb.reshape(nb, seq, dr)], axis=1)
                st["conv_new"].append(hist[:, -(CONV_WIDTH - 1):])
                st["h_new"].append(h_last)
            st.update(z=z, w_mix=w_out16[l])
        else:
            j = l - n_a
            (q,) = _head_proj(x, mod, g1, w_q16[j], g_q[j][None], st["tables"], st["tab_nb"],
                              st["tab_seq"], tm, n_heads, 0, hd)
            if prompt:
                st.update(z=_attn_prompt(q, st["k"], st["v"]).reshape(t, n_heads * hd))
            else:
                q8 = pad_rows(q, nb, seq)
                q_bd = jnp.einsum("bhqd,hg->bhqgd", q8, jnp.eye(n_heads, dtype=F32))
                sel = _select(q_bd.reshape(nb, n_heads * SUBLANES, n_heads * hd), st["means"],
                              pages_f)
                sel = sel.reshape(nb, n_heads, SUBLANES, LANES)
                sel = sel[:, :, :seq, :MOBA_TOP_K * PAGES_PER_BLOCK].reshape(nb, -1)
                st.update(z=None, paged=(q8, st["k8"], st["v8"], cache_k, cache_v, sel, seq))
            st.update(w_mix=w_o16[j])

    def mlp(st, l, hosted=None):
        tm = host_tm if hosted is not None else st["tm"]
        out = _mix_mlp(st["x"], st["z"], st["mod"], st["w_mix"],
                       group_mods(mods[2 * l + 1], st["prompt"]), g_norm[l, 1][None],
                       w_up16[l], w_down16[l], tm, st["seq"] // tm if st["prompt"] else 1, hosted)
        if hosted is None:
            st["x"] = out
            return None
        st["x"] = out[0]
        return out[1]

    grp_p, grp_s = setup(x_prompt, True), setup(x_sample, False)
    for l in range(depth):
        mixer(grp_p, l)
        mixer(grp_s, l)
        paged = grp_s["paged"]
        if paged is not None and host_attn:
            o = mlp(grp_p, l, hosted=paged)
        else:
            mlp(grp_p, l)
            o = _attn_paged(*paged) if paged is not None else None
        if o is not None:
            grp_s["z"] = unpad_rows(o, grp_s["nb"], grp_s["seq"])
        mlp(grp_s, l)

    def finish(st):
        nb, seq = st["nb"], st["seq"]
        k, v = st["k"], st["v"]
        if not st["prompt"]:
            k = k.reshape(n_heads, nb, seq, hd).transpose(1, 0, 2, 3)
            v = v.reshape(n_heads, nb, seq, hd).transpose(1, 0, 2, 3)
        return (st["x"].reshape(nb, seq, d), jnp.stack(st["conv_new"]), jnp.stack(st["h_new"]),
                k, v)

    y_p, conv_p, h_p, k_p, v_p = finish(grp_p)
    y_s, conv_s, h_s, k_s, v_s = finish(grp_s)
    return (y_p, y_s, conv_p, h_p, conv_s, h_s, k_p, v_p, k_s, v_s)
```

```python
import functools
import math

import jax
import jax.numpy as jnp
from jax import lax
from jax.experimental import pallas as pl
from jax.experimental.pallas import tpu as pltpu

F32 = jnp.float32
BF16 = jnp.bfloat16

NORM_EPS = 1e-6
LRU_C = 8.0
CONV_WIDTH = 4
PAGE_SIZE = 128
MOBA_BLOCK = 256
MOBA_TOP_K = 3
PAGES_PER_BLOCK = MOBA_BLOCK // PAGE_SIZE
N_HEADS = 8
ROPE_THETA = 500000.0
NEG = -1e30
LOG2_E = math.log2(math.e)
LANES = 128
SUBLANES = 8
GATE_GROUP = 256
VMEM_LIMIT = 56 * 1024 * 1024


def _params(*sem):
    return pltpu.CompilerParams(dimension_semantics=sem, vmem_limit_bytes=VMEM_LIMIT)


def _resident(shape):
    nd = len(shape)
    return pl.BlockSpec(shape, lambda *_: (0,) * nd, pipeline_mode=pl.Buffered(1))


def _resident_layer(shape, layer):
    nd = len(shape)
    return pl.BlockSpec((1,) + tuple(shape[1:]), lambda *_: (layer,) + (0,) * (nd - 1),
                        pipeline_mode=pl.Buffered(1))


def _dot(a, b):
    return jnp.dot(a, b, preferred_element_type=F32)


def _dot_nt(a, b):
    return lax.dot_general(a, b, (((1,), (1,)), ((), ())), preferred_element_type=F32)


def _split(a):
    hi = a.astype(BF16)
    return hi, (a - hi.astype(F32)).astype(BF16)


def _dot3(a, b, dot=_dot):
    ah, al = _split(a)
    bh, bl = _split(b)
    return dot(ah, bh) + (dot(ah, bl) + dot(al, bh))


def _sigmoid(x):
    return 0.5 + 0.5 * jnp.tanh(0.5 * x)


def _gelu_tanh(x):
    return 0.5 * x * (1.0 + jnp.tanh(math.sqrt(2.0 / math.pi) * (x + 0.044715 * (x * x * x))))


def _softplus(x):
    return jnp.maximum(x, 0.0) + jnp.log1p(jnp.exp(-jnp.abs(x)))


def _modulate(x, g, mod, d):
    ms = jnp.mean(x * x, axis=-1, keepdims=True)
    y = x * lax.rsqrt(ms + NORM_EPS) * g
    return y * (1.0 + mod[:, d:2 * d]) + mod[:, :d]


def _adaln_kernel(c_ref, w_ref, b_ref, o_ref):
    c = c_ref[...]
    a = c * _sigmoid(c)
    r = a.shape[0]
    a_hi = a.astype(BF16).astype(F32)
    w_hi, w_lo = _split(w_ref[0])
    stacked = _dot(jnp.concatenate([a_hi, a - a_hi], axis=0).astype(BF16), w_hi)
    o_ref[0] = stacked[:r] + (stacked[r:] + _dot(a_hi.astype(BF16), w_lo)) + b_ref[0]


def _adaln(c, w, b):
    r, d = c.shape
    m, _, n = w.shape
    tn = 1024
    return pl.pallas_call(
        _adaln_kernel,
        grid=(m, n // tn),
        in_specs=[
            pl.BlockSpec((r, d), lambda i, j: (0, 0)),
            pl.BlockSpec((1, d, tn), lambda i, j: (i, 0, j)),
            pl.BlockSpec((1, 1, tn), lambda i, j: (i, 0, j)),
        ],
        out_specs=pl.BlockSpec((1, r, tn), lambda i, j: (i, 0, j)),
        out_shape=jax.ShapeDtypeStruct((m, r, n), F32),
        compiler_params=_params("parallel", "parallel"),
        name="adaln",
    )(c, w, b)


def _mod_spec(mod, tm, tiles_per_seq):
    nb, rows, w = mod.shape
    if rows == 1:
        return pl.BlockSpec((1, 1, w), lambda i, *_: (i // tiles_per_seq, 0, 0))
    return pl.BlockSpec((1, tm, w), lambda i, *_: (0, i, 0))


def _rec_in_kernel(x_ref, mod_ref, g_ref, w_ref, ga_ref, xb_ref, *, d, dr):
    h = _modulate(x_ref[...], g_ref[...], mod_ref[0], d).astype(BF16)
    y = _dot(h, w_ref[...])
    ga_ref[...] = _gelu_tanh(y[:, :dr])
    xb_ref[...] = y[:, dr:]


def _rec_in(x, mod, g, w, tm, tiles_per_seq):
    t, d = x.shape
    dr = w.shape[1] // 2
    row = pl.BlockSpec((tm, dr), lambda i: (i, 0))
    return pl.pallas_call(
        functools.partial(_rec_in_kernel, d=d, dr=dr),
        grid=(t // tm,),
        in_specs=[
            pl.BlockSpec((tm, d), lambda i: (i, 0)),
            _mod_spec(mod, tm, tiles_per_seq),
            _resident((1, d)),
            _resident(w.shape),
        ],
        out_specs=[row, row],
        out_shape=[jax.ShapeDtypeStruct((t, dr), F32)] * 2,
        compiler_params=_params("parallel"),
        name="rec_in",
    )(x, mod, g, w)


def _rope_partner(hd):
    half = hd // 8
    i = jnp.arange(hd)
    return jnp.where(i < half, i + half, jnp.where(i < 2 * half, i - half, i))


def _with_partner_columns(w, n_rope, hd):
    d = w.shape[0]
    wr = w[:, :n_rope * hd].reshape(d, n_rope, hd)[:, :, _rope_partner(hd)]
    return jnp.concatenate([w, wr.reshape(d, n_rope * hd)], axis=1)


def _head_proj_kernel(x_ref, mod_ref, g_ref, w_ref, gh_ref, ghp_ref, cos_ref, sin_ref, *out_refs,
                      d, hd, n_rope, n_plain):
    h_ref = out_refs[-1]
    h_ref[...] = _modulate(x_ref[...], g_ref[...], mod_ref[0], d).astype(BF16)
    cg = cos_ref[...] * gh_ref[...]
    sg = sin_ref[...] * ghp_ref[...]
    partner0 = n_rope + n_plain
    roped, plain = [], []
    for i0 in range(0, n_rope + n_plain, 2):
        y = _dot(h_ref[...], w_ref[:, i0 * hd:(i0 + 2) * hd])
        if i0 < n_rope:
            yp = _dot(h_ref[...], w_ref[:, (partner0 + i0) * hd:(partner0 + i0 + 2) * hd])
        for i in (i0, i0 + 1):
            cols = slice((i - i0) * hd, (i - i0 + 1) * hd)
            if i >= n_rope:
                plain.append(y[:, cols])
                continue
            yh = y[:, cols]
            rs = lax.rsqrt(jnp.mean(yh * yh, axis=-1, keepdims=True) + NORM_EPS)
            roped.append(rs * (yh * cg + yp[:, cols] * sg))
    out_refs[0][0] = jnp.stack(roped, axis=0)
    if plain:
        out_refs[1][0] = jnp.stack(plain, axis=0)


def _head_proj(x, mod, g, w, gh, tables, nb, seq, tm, n_rope, n_plain, hd):
    t, d = x.shape
    assert n_rope % 2 == 0 and n_plain % 2 == 0
    tps = seq // tm
    ghp = gh[:, _rope_partner(hd)]
    tab = pl.BlockSpec((tm, hd), lambda i: (i % tps, 0))
    out_specs = [pl.BlockSpec((1, n_rope, tm, hd), lambda i: (i // tps, 0, i % tps, 0))]
    out_shape = [jax.ShapeDtypeStruct((nb, n_rope, seq, hd), F32)]
    if n_plain:
        out_specs.append(pl.BlockSpec((1, n_plain, tm, hd), lambda i: (i // tps, 0, i % tps, 0)))
        out_shape.append(jax.ShapeDtypeStruct((nb, n_plain, seq, hd), F32))
    return pl.pallas_call(
        functools.partial(_head_proj_kernel, d=d, hd=hd, n_rope=n_rope, n_plain=n_plain),
        grid=(t // tm,),
        in_specs=[
            pl.BlockSpec((tm, d), lambda i: (i, 0)),
            _mod_spec(mod, tm, tps),
            _resident((1, d)),
            _resident(w.shape),
            _resident((1, hd)),
            _resident((1, hd)),
            tab, tab,
        ],
        out_specs=out_specs,
        out_shape=out_shape,
        scratch_shapes=[pltpu.VMEM((tm, d), BF16)],
        compiler_params=_params("parallel"),
        name="head_proj",
    )(x, mod, g, w, gh, ghp, *tables)


FF_CHUNK = 1024


PAGES_PER_MEANS_STEP = 16


def _page_block_means(k_refs, o_ref):
    n = len(k_refs)
    n_heads, _, hd = k_refs[0].shape[1:]
    for h in range(n_heads):
        sums = [jnp.sum(k_refs[j][0, h], axis=0, keepdims=True) for j in range(n)]
        o_ref[0, :, h * hd:(h + 1) * hd] = jnp.concatenate(
            [(sums[2 * i] + sums[2 * i + 1]) * (1.0 / MOBA_BLOCK) for i in range(n // 2)], axis=0)


def _mix_mlp_kernel(*refs, d, attn):
    if attn:
        sel_ref, refs = refs[0], refs[1:]
        _paged_heads_step(sel_ref, *refs[8:13], refs[14], *refs[18:21], **attn)
        refs = refs[:8] + (refs[13],) + refs[15:18]
    x_ref, z_ref, mod1_ref, wo_ref, mod2_ref, g_ref, wu_ref, wd_ref, o_ref = refs[:9]
    x1_ref, h_ref, acc_ref = refs[-3:]
    x1 = x_ref[...] + mod1_ref[0][:, 2 * d:] * _dot(z_ref[...].astype(BF16), wo_ref[...])
    x1_ref[...] = x1
    mod2 = mod2_ref[0]
    h_ref[...] = _modulate(x1, g_ref[...], mod2, d).astype(BF16)
    for c in range(wu_ref.shape[2] // FF_CHUNK):
        cols = slice(c * FF_CHUNK, (c + 1) * FF_CHUNK)
        u = _dot(h_ref[...], wu_ref[0, :, cols])
        a = jnp.square(jnp.maximum(u, 0.0)).astype(BF16)
        part = _dot(a, wd_ref[0, cols, :])
        if c == 0:
            acc_ref[...] = part
        else:
            acc_ref[...] += part
    o_ref[...] = x1_ref[...] + mod2[:, 2 * d:] * acc_ref[...]


def _mix_mlp(x, z, mod1, wo, mod2, g, wu, wd, layer, tm, tiles_per_seq, hosted=None):
    t, d = x.shape
    assert wu.shape[2] % FF_CHUNK == 0
    steps = t // tm
    row = pl.BlockSpec((tm, d), lambda i, *_: (i, 0))
    in_specs = [
        row,
        pl.BlockSpec((tm, z.shape[1]), lambda i, *_: (i, 0)),
        _mod_spec(mod1, tm, tiles_per_seq),
        _resident(wo.shape),
        _mod_spec(mod2, tm, tiles_per_seq),
        _resident((1, d)),
        _resident_layer(wu.shape, layer),
        _resident_layer(wd.shape, layer),
    ]
    scratch = [pltpu.VMEM((tm, d), F32), pltpu.VMEM((tm, d), BF16), pltpu.VMEM((tm, d), F32)]
    out_specs, out_shape = row, jax.ShapeDtypeStruct((t, d), F32)
    args = (x, z, mod1, wo, mod2, g, wu, wd)
    if hosted is None:
        return pl.pallas_call(
            functools.partial(_mix_mlp_kernel, d=d, attn=None),
            grid=(steps,), in_specs=in_specs, out_specs=out_specs, out_shape=out_shape,
            scratch_shapes=scratch, compiler_params=_params("parallel"), name="mix_mlp",
        )(*args)
    q, k_new, v_new, cache_k, cache_v, sel, sq = hosted
    db, h, rows, hd = q.shape
    assert steps % db == 0 and h % (steps // db) == 0
    spq = steps // db
    hps = h // spq
    n_slots = sq * MOBA_TOP_K * PAGES_PER_BLOCK
    heads = pl.BlockSpec((1, hps, rows, hd), lambda i, sel: (i // spq, i % spq, 0, 0))
    hbm = pl.BlockSpec(memory_space=pl.ANY)
    return pl.pallas_call(
        functools.partial(_mix_mlp_kernel, d=d,
                          attn=dict(n_slots=n_slots, sq=sq, scale=hd ** -0.5, n_heads=h)),
        grid_spec=pltpu.PrefetchScalarGridSpec(
            num_scalar_prefetch=1,
            grid=(steps,),
            in_specs=in_specs + [heads, heads, heads, hbm, hbm],
            out_specs=[out_specs, heads],
            scratch_shapes=scratch + [pltpu.VMEM((hps, n_slots, PAGE_SIZE, hd), F32),
                                      pltpu.VMEM((hps, n_slots, PAGE_SIZE, hd), F32),
                                      pltpu.SemaphoreType.DMA((2,))],
        ),
        out_shape=[out_shape, jax.ShapeDtypeStruct((db, h, rows, hd), F32)],
        compiler_params=_params("arbitrary"),
        name="mix_mlp_attn",
    )(_page_rows(sel, h, n_slots), *args, q, k_new, v_new,
      cache_k.reshape(-1, PAGE_SIZE, hd), cache_v.reshape(-1, PAGE_SIZE, hd))


def _lru_inputs(xc, wr_ref, wi_ref, bg_ref, sp):
    xh = xc.astype(BF16)
    gw = wr_ref.shape[1]
    r = jnp.concatenate([_dot(xh[:, j * gw:(j + 1) * gw], wr_ref[j])
                         for j in range(wr_ref.shape[0])], axis=1) + bg_ref[0:1, :]
    i = jnp.concatenate([_dot(xh[:, j * gw:(j + 1) * gw], wi_ref[j])
                         for j in range(wi_ref.shape[0])], axis=1) + bg_ref[1:2, :]
    log_a = -LRU_C * _sigmoid(r) * sp
    t = jnp.tanh(0.5 * log_a)
    rc = 1.0 / (1.0 - t)
    a = (1.0 + t) * rc
    u = (2.0 * rc) * jnp.sqrt(-t) * (_sigmoid(i) * xc)
    return a, u


def _scan_prompt_kernel(*refs, tt, n_side):
    if n_side:
        refs = refs[1:]
        _page_block_means(refs[12:12 + n_side], refs[15 + n_side])
    (x_ref, mod_ref, g_ref, win_ref, st_ref, h0_ref, cw_ref, cb_ref, wr_ref, wi_ref, bg_ref,
     lam_ref) = refs[:12]
    z_ref, tail_out_ref, hl_ref = refs[12 + n_side:15 + n_side]
    tail_ref, hc_ref, a_ref, u_ref, ga_ref = refs[-5:]
    ti = pl.program_id(1)
    d, dr = x_ref.shape[1], z_ref.shape[1]

    @pl.when(ti == 0)
    def _():
        tail_ref[...] = st_ref[0]
        hc_ref[...] = h0_ref[0]

    y = _dot(_modulate(x_ref[...], g_ref[...], mod_ref[0], d).astype(BF16), win_ref[...])
    ga_ref[...] = _gelu_tanh(y[:, :dr])
    x = y[:, dr:]
    tail = tail_ref[...]
    row8 = lax.broadcasted_iota(jnp.int32, (SUBLANES, dr), 0)
    xc = cb_ref[...] + cw_ref[CONV_WIDTH - 1:CONV_WIDTH, :] * x
    for k in range(1, CONV_WIDTH):
        xs = pltpu.roll(x, k, 0)
        first = jnp.where(row8 < k, pltpu.roll(tail, k, 0), xs[:SUBLANES])
        xs = jnp.concatenate([first, xs[SUBLANES:]], axis=0)
        xc = xc + cw_ref[CONV_WIDTH - 1 - k:CONV_WIDTH - k, :] * xs
    tail_ref[...] = x[tt - SUBLANES:]
    tail_out_ref[0] = x[tt - SUBLANES:]

    a, u = _lru_inputs(xc, wr_ref, wi_ref, bg_ref, _softplus(-lam_ref[...]))
    a_ref[...] = a
    u_ref[...] = u

    def group(gi, carry):
        r0 = pl.multiple_of(gi * SUBLANES, SUBLANES)
        a8 = a_ref[pl.ds(r0, SUBLANES), :]
        u8 = u_ref[pl.ds(r0, SUBLANES), :]
        for s in (1, 2, 4):
            keep = row8 >= s
            u8 = jnp.where(keep, a8 * pltpu.roll(u8, s, 0) + u8, u8)
            a8 = jnp.where(keep, a8 * pltpu.roll(a8, s, 0), a8)
        h8 = a8 * carry + u8
        z_ref[pl.ds(r0, SUBLANES), :] = h8 * ga_ref[pl.ds(r0, SUBLANES), :]
        return jnp.broadcast_to(h8[SUBLANES - 1:SUBLANES, :], (SUBLANES, dr))

    carry = lax.fori_loop(0, tt // SUBLANES, group,
                          jnp.broadcast_to(hc_ref[...], (SUBLANES, dr)), unroll=4)
    hc_ref[...] = carry[0:1, :]
    hl_ref[0] = carry[0:1, :]


def _scan_prompt(x, mod, g, w_in, st8, h0, cw, cb, wr, wi, bg, lam, nb, seq, tt, side=None):
    t, d = x.shape
    dr = w_in.shape[1] // 2
    nt = seq // tt
    row = pl.BlockSpec((tt, dr), lambda b, i, *_: (b * nt + i, 0))
    per_seq8 = pl.BlockSpec((1, SUBLANES, dr), lambda b, i, *_: (b, 0, 0))
    per_seq1 = pl.BlockSpec((1, 1, dr), lambda b, i, *_: (b, 0, 0))
    in_specs = [pl.BlockSpec((tt, d), lambda b, i, *_: (b * nt + i, 0)),
                pl.BlockSpec((1, 1, mod.shape[2]), lambda b, i, *_: (b, 0, 0)),
                _resident(g.shape), _resident(w_in.shape),
                per_seq8, per_seq1, _resident(cw.shape), _resident(cb.shape),
                _resident(wr.shape), _resident(wi.shape), _resident(bg.shape),
                _resident(lam.shape)]
    out_specs = [row, per_seq8, per_seq1]
    out_shape = [jax.ShapeDtypeStruct((t, dr), F32),
                 jax.ShapeDtypeStruct((nb, SUBLANES, dr), F32),
                 jax.ShapeDtypeStruct((nb, 1, dr), F32)]
    scratch = [pltpu.VMEM((SUBLANES, dr), F32), pltpu.VMEM((1, dr), F32),
               pltpu.VMEM((tt, dr), F32), pltpu.VMEM((tt, dr), F32), pltpu.VMEM((tt, dr), F32)]
    args = (x, mod, g, w_in, st8, h0, cw, cb, wr, wi, bg, lam)
    if side is None:
        return pl.pallas_call(
            functools.partial(_scan_prompt_kernel, tt=tt, n_side=0),
            grid=(nb, nt), in_specs=in_specs, out_specs=out_specs, out_shape=out_shape,
            scratch_shapes=scratch, compiler_params=_params("parallel", "arbitrary"),
            name="scan_prompt",
        )(*args)
    cache_k, page_table, first_seq, n_seqs = side
    n = PAGES_PER_MEANS_STEP
    n_pages = page_table.shape[1]
    _, h, _, hd = cache_k.shape
    gps = n_pages // n
    last = n_seqs * gps - 1

    def group(b, i):
        return jnp.minimum(b * nt + i, last)

    def page(j):
        return pl.BlockSpec(
            (1, h, PAGE_SIZE, hd),
            lambda b, i, pt: (pt[first_seq + group(b, i) // gps, (group(b, i) % gps) * n + j],
                              0, 0, 0))

    return pl.pallas_call(
        functools.partial(_scan_prompt_kernel, tt=tt, n_side=n),
        grid_spec=pltpu.PrefetchScalarGridSpec(
            num_scalar_prefetch=1,
            grid=(nb, nt),
            in_specs=in_specs + [page(j) for j in range(n)],
            out_specs=out_specs + [pl.BlockSpec(
                (1, n // PAGES_PER_BLOCK, h * hd),
                lambda b, i, pt: (group(b, i) // gps, group(b, i) % gps, 0))],
            scratch_shapes=scratch,
        ),
        out_shape=out_shape + [
            jax.ShapeDtypeStruct((n_seqs, n_pages // PAGES_PER_BLOCK, h * hd), F32)],
        compiler_params=_params("arbitrary", "arbitrary"),
        name="scan_prompt_means",
    )(page_table, *args, *([cache_k] * n))


def _scan_steps_kernel(xb_ref, ga_ref, st_ref, h0_ref, cw_ref, cb_ref, wr_ref, wi_ref, bg_ref,
                       lam_ref, z_ref, hl_ref, *, steps):
    rows = [st_ref[j] for j in range(CONV_WIDTH - 1)] + [xb_ref[t] for t in range(steps)]
    sp = _softplus(-lam_ref[...])
    h = h0_ref[...]
    for t in range(steps):
        xc = cb_ref[...]
        for j in range(CONV_WIDTH):
            xc = xc + cw_ref[j:j + 1, :] * rows[t + j]
        a, u = _lru_inputs(xc, wr_ref, wi_ref, bg_ref, sp)
        h = a * h + u
        z_ref[t] = h * ga_ref[t]
    hl_ref[...] = h


def _scan_steps(xb, ga, st, h0, cw, cb, wr, wi, bg, lam):
    steps, nb, dr = xb.shape
    args = (xb, ga, st, h0, cw, cb, wr, wi, bg, lam)
    return pl.pallas_call(
        functools.partial(_scan_steps_kernel, steps=steps),
        grid=(1,),
        in_specs=[_resident(a.shape) for a in args],
        out_specs=[pl.BlockSpec((steps, nb, dr), lambda i: (0, 0, 0)),
                   pl.BlockSpec((nb, dr), lambda i: (0, 0))],
        out_shape=[jax.ShapeDtypeStruct((steps, nb, dr), F32),
                   jax.ShapeDtypeStruct((nb, dr), F32)],
        compiler_params=_params("arbitrary"),
        name="scan_steps",
    )(*args)


ATTN_KEY_CHUNK = 128


def _block_bias_t(gate_t, cur):
    row = lax.broadcasted_iota(jnp.int32, gate_t.shape, 0)
    row_f = row.astype(F32)
    g = jnp.where(row < cur, gate_t, NEG)
    bias = jnp.where(row == cur, 0.0, NEG)
    for _ in range(min(MOBA_TOP_K, cur)):
        m = jnp.max(g, axis=0, keepdims=True)
        idx = jnp.min(jnp.where(g == m, row_f, float(LANES)), axis=0, keepdims=True)
        pick = row_f == idx
        bias = jnp.where(pick, 0.0, bias)
        g = jnp.where(pick, NEG, g)
    return bias


def _attn_prompt_kernel(q_ref, k_ref, v_ref, o_ref, kaug_ref, vt_ref, means_ref, *, seq, hd, scale):
    bq = MOBA_BLOCK
    nblk = seq // bq
    nbp = means_ref.shape[0]

    lane = lax.broadcasted_iota(jnp.int32, (bq, hd), 1)
    means_ref[...] = jnp.zeros(means_ref.shape, F32)
    for n in range(nblk):
        kb = k_ref[0, 0, n * bq:(n + 1) * bq, :]
        means_ref[n:n + 1, :] = jnp.sum(kb, axis=0, keepdims=True) * (1.0 / MOBA_BLOCK)
        onehot = jnp.where(lane == n, 1.0, 0.0).astype(BF16)
        kaug_ref[n * bq:(n + 1) * bq, :] = jnp.concatenate([kb.astype(BF16), onehot], axis=1)
        vt_ref[:, n * bq:(n + 1) * bq] = v_ref[0, 0, n * bq:(n + 1) * bq, :].T.astype(BF16)

    kc = ATTN_KEY_CHUNK
    key_idx = lax.broadcasted_iota(jnp.int32, (kc, bq), 0)
    query_idx = lax.broadcasted_iota(jnp.int32, (kc, bq), 1)

    outs = []
    for c in range(nblk):
        qt = q_ref[0, 0, c * bq:(c + 1) * bq, :].T
        gate_t = _dot3(means_ref[...], qt)
        qt_aug = jnp.concatenate(
            [qt * (scale * LOG2_E), _block_bias_t(gate_t, c), jnp.zeros((hd - nbp, bq), F32)],
            axis=0).astype(BF16)
        m = l = acc = None
        own = list(range(c * bq // kc, (c + 1) * bq // kc))
        for j in own + list(range(c * bq // kc)):
            s = _dot(kaug_ref[j * kc:(j + 1) * kc, :], qt_aug)
            if j in own:
                s = jnp.where(key_idx + (j * kc - c * bq) <= query_idx, s, NEG)
            mn = jnp.max(s, axis=0, keepdims=True)
            p = jnp.exp2(s - mn)
            ln = jnp.sum(p, axis=0, keepdims=True)
            on = _dot(vt_ref[:, j * kc:(j + 1) * kc], p.astype(BF16))
            if m is None:
                m, l, acc = mn, ln, on
            else:
                m_new = jnp.maximum(m, mn)
                wa, wb = jnp.exp2(m - m_new), jnp.exp2(mn - m_new)
                m, l, acc = m_new, l * wa + ln * wb, acc * wa + on * wb
        outs.append((acc * (1.0 / l)).T)
    o_ref[0] = jnp.concatenate(outs, axis=0)


def _attn_prompt(q, k, v):
    b, h, seq, hd = q.shape
    nblk = seq // MOBA_BLOCK
    assert seq % MOBA_BLOCK == 0 and nblk <= hd and hd == LANES
    nbp = -(-nblk // SUBLANES) * SUBLANES
    full = pl.BlockSpec((1, 1, seq, hd), lambda bi, hi: (bi, hi, 0, 0))
    return pl.pallas_call(
        functools.partial(_attn_prompt_kernel, seq=seq, hd=hd, scale=hd ** -0.5),
        grid=(b, h),
        in_specs=[full, full, full],
        out_specs=pl.BlockSpec((1, seq, hd), lambda bi, hi: (bi, 0, hi)),
        out_shape=jax.ShapeDtypeStruct((b, seq, h * hd), F32),
        scratch_shapes=[pltpu.VMEM((seq, 2 * hd), BF16), pltpu.VMEM((hd, seq), BF16),
                        pltpu.VMEM((nbp, hd), F32)],
        compiler_params=_params("parallel", "parallel"),
        name="attn_prompt",
    )(q, k, v)


def _page_means_kernel(pt_ref, *refs, n):
    _page_block_means(refs[:n], refs[n])


def _page_means(cache_k, page_table):
    db, n_pages = page_table.shape
    _, h, _, hd = cache_k.shape
    n = min(PAGES_PER_MEANS_STEP, n_pages)

    def page(j):
        return pl.BlockSpec((1, h, PAGE_SIZE, hd), lambda b, g, pt: (pt[b, g * n + j], 0, 0, 0))

    return pl.pallas_call(
        functools.partial(_page_means_kernel, n=n),
        grid_spec=pltpu.PrefetchScalarGridSpec(
            num_scalar_prefetch=1,
            grid=(db, n_pages // n),
            in_specs=[page(j) for j in range(n)],
            out_specs=pl.BlockSpec((1, n // 2, h * hd), lambda b, g, pt: (b, g, 0)),
        ),
        out_shape=jax.ShapeDtypeStruct((db, n_pages // 2, h * hd), F32),
        compiler_params=_params("parallel", "arbitrary"),
        name="page_means",
    )(page_table, *([cache_k] * n))


def _select_kernel(q_ref, means_ref, pt_ref, sel_ref):
    n_blocks = means_ref.shape[1]
    outs = []
    for i in range(q_ref.shape[0]):
        means = jnp.concatenate(
            [means_ref[i], jnp.zeros((LANES - n_blocks, means_ref.shape[2]), F32)], axis=0)
        gate = _dot3(q_ref[i], means, _dot_nt)
        lane = lax.broadcasted_iota(jnp.int32, gate.shape, 1)
        lane_f = lane.astype(F32)
        pages = pt_ref[i]
        g = jnp.where(lane < n_blocks, gate, NEG)
        out = jnp.zeros(gate.shape, F32)
        for j in range(MOBA_TOP_K):
            m = jnp.max(g, axis=1, keepdims=True)
            idx = jnp.min(jnp.where(g == m, lane_f, float(LANES)), axis=1, keepdims=True)
            for p in range(PAGES_PER_BLOCK):
                phys = jnp.sum(jnp.where(lane_f == idx * PAGES_PER_BLOCK + p, pages, 0.0),
                               axis=1, keepdims=True)
                out = jnp.where(lane == j * PAGES_PER_BLOCK + p, phys, out)
            g = jnp.where(lane_f == idx, NEG, g)
        outs.append(out.astype(jnp.int32))
    sel_ref[...] = jnp.stack(outs, axis=0)


def _select(q_bd, means, pages_f):
    db, r, w = q_bd.shape
    nb = means.shape[1]
    assert MOBA_TOP_K <= nb and nb * PAGES_PER_BLOCK <= LANES
    g = math.gcd(db, 8)
    return pl.pallas_call(
        _select_kernel,
        grid=(db // g,),
        in_specs=[pl.BlockSpec((g, r, w), lambda b: (b, 0, 0)),
                  pl.BlockSpec((g, nb, w), lambda b: (b, 0, 0)),
                  pl.BlockSpec((g, 1, LANES), lambda b: (b, 0, 0))],
        out_specs=pl.BlockSpec((g, r, LANES), lambda b: (b, 0, 0)),
        out_shape=jax.ShapeDtypeStruct((db, r, LANES), jnp.int32),
        compiler_params=_params("parallel"),
        name="select_blocks",
    )(q_bd, means, pages_f)


def _attn_paged_kernel(sel_ref, q_ref, kn_ref, vn_ref, ck_ref, cv_ref, o_ref, kbuf, vbuf, sems,
                       *, n_slots, sq, scale):
    b = pl.program_id(0)
    n_heads = q_ref.shape[1]
    assert n_heads % 2 == 0

    def page_copies(seq, h, page_of_slot):
        out = []
        for s in range(n_slots):
            row = page_of_slot(seq, h * n_slots + s)
            out.append(pltpu.make_async_copy(ck_ref.at[row], kbuf.at[h % 2, s], sems.at[0, h % 2]))
            out.append(pltpu.make_async_copy(cv_ref.at[row], vbuf.at[h % 2, s], sems.at[1, h % 2]))
        return out

    def start_all(copies):
        for i, c in enumerate(copies):
            c.start(priority=i % 2)

    @pl.when(b == 0)
    def _():
        start_all(page_copies(b, 0, lambda seq, i: sel_ref[seq, i]))

    for h in range(n_heads):
        if h + 1 < n_heads:
            start_all(page_copies(b, h + 1, lambda seq, i: sel_ref[seq, i]))
        else:
            @pl.when(b + 1 < pl.num_programs(0))
            def _():
                start_all(page_copies(b + 1, 0, lambda seq, i: sel_ref[seq, i]))
        for c in page_copies(0, h, lambda seq, i: 0):
            c.wait()
        o_ref[0, h] = _paged_head(q_ref[0, h], kn_ref[0, h], vn_ref[0, h], kbuf[h % 2],
                                  vbuf[h % 2], sq, scale)


def _page_rows(sel, n_heads, n_slots):
    head_of_slot = jnp.repeat(jnp.arange(n_heads, dtype=jnp.int32), n_slots)
    return sel * n_heads + head_of_slot[None, :]


def _paged_heads_step(sel_ref, q_ref, kn_ref, vn_ref, ck_ref, cv_ref, o_ref, kbuf, vbuf, sems,
                      *, n_slots, sq, scale, n_heads):
    i = pl.program_id(0)
    hps = q_ref.shape[1]
    spq = n_heads // hps

    def page_copies(step, page_of):
        seq, h0 = step // spq, (step % spq) * hps
        out = []
        for j in range(hps):
            for s in range(n_slots):
                row = page_of(seq, (h0 + j) * n_slots + s)
                out.append(pltpu.make_async_copy(ck_ref.at[row], kbuf.at[j, s], sems.at[0]))
                out.append(pltpu.make_async_copy(cv_ref.at[row], vbuf.at[j, s], sems.at[1]))
        return out

    def start_all(copies):
        for n, c in enumerate(copies):
            c.start(priority=n % 2)

    @pl.when(i == 0)
    def _():
        start_all(page_copies(i, lambda seq, n: sel_ref[seq, n]))

    for c in page_copies(0, lambda seq, n: 0):
        c.wait()
    for j in range(hps):
        o_ref[0, j] = _paged_head(q_ref[0, j], kn_ref[0, j], vn_ref[0, j], kbuf[j], vbuf[j],
                                  sq, scale)

    @pl.when(i + 1 < pl.num_programs(0))
    def _():
        start_all(page_copies(i + 1, lambda seq, n: sel_ref[seq, n]))


def _paged_head(q, kn, vn, k_pages, v_pages, sq, scale):
    n_slots, _, hd = k_pages.shape
    k_all = k_pages.reshape(n_slots * PAGE_SIZE, hd).astype(BF16)
    v_all = v_pages.reshape(n_slots * PAGE_SIZE, hd).astype(BF16)
    s = _dot_nt((q * scale).astype(BF16), k_all)
    keys_per_q = MOBA_TOP_K * MOBA_BLOCK
    row = lax.broadcasted_iota(jnp.int32, s.shape, 0)
    col = lax.broadcasted_iota(jnp.int32, s.shape, 1)
    s = jnp.where((col >= row * keys_per_q) & (col < (row + 1) * keys_per_q), s, NEG)
    row1 = lax.broadcasted_iota(jnp.int32, (q.shape[0], 1), 0)
    s_new = [jnp.where(row1 >= j, jnp.sum(q * kn[j:j + 1, :], axis=1, keepdims=True) * scale, NEG)
             for j in range(sq)]
    m = functools.reduce(jnp.maximum, s_new + [jnp.max(s, axis=1, keepdims=True)])
    p = jnp.exp(s - m)
    p_new = [jnp.exp(sj - m) for sj in s_new]
    l = functools.reduce(lambda x, y: x + y, p_new + [jnp.sum(p, axis=1, keepdims=True)])
    acc = functools.reduce(lambda x, y: x + y,
                           [p_new[j] * vn[j:j + 1, :] for j in range(sq)]
                           + [_dot(p.astype(BF16), v_all)])
    return acc / l


def _attn_paged(q, k_new, v_new, cache_k, cache_v, sel, sq):
    db, h, rows, hd = q.shape
    n_slots = sq * MOBA_TOP_K * PAGES_PER_BLOCK
    per_seq = pl.BlockSpec((1, h, rows, hd), lambda b, sel: (b, 0, 0, 0))
    hbm = pl.BlockSpec(memory_space=pl.ANY)
    sel_rows = _page_rows(sel, h, n_slots)
    return pl.pallas_call(
        functools.partial(_attn_paged_kernel, n_slots=n_slots, sq=sq, scale=hd ** -0.5),
        grid_spec=pltpu.PrefetchScalarGridSpec(
            num_scalar_prefetch=1,
            grid=(db,),
            in_specs=[per_seq, per_seq, per_seq, hbm, hbm],
            out_specs=per_seq,
            scratch_shapes=[pltpu.VMEM((2, n_slots, PAGE_SIZE, hd), F32),
                            pltpu.VMEM((2, n_slots, PAGE_SIZE, hd), F32),
                            pltpu.SemaphoreType.DMA((2, 2))],
        ),
        out_shape=jax.ShapeDtypeStruct((db, h, rows, hd), F32),
        compiler_params=_params("arbitrary"),
        name="attn_paged",
    )(sel_rows, q, k_new, v_new, cache_k.reshape(-1, PAGE_SIZE, hd),
      cache_v.reshape(-1, PAGE_SIZE, hd))


def _rope_tables(pos, hd):
    half = hd // 8
    inv = ROPE_THETA ** (-jnp.arange(half, dtype=F32) * 2.0 / (2 * half))
    ang = pos.astype(F32)[:, None] * inv[None, :]
    cos, sin = jnp.cos(ang), jnp.sin(ang)
    n = pos.shape[0]
    ones = jnp.ones((n, hd - 2 * half), F32)
    zeros = jnp.zeros((n, hd - 2 * half), F32)
    return (jnp.concatenate([cos, cos, ones], axis=1),
            jnp.concatenate([-sin, sin, zeros], axis=1))


def _block_diag_gate(w_gate):
    n, k, _ = w_gate.shape
    per = GATE_GROUP // k
    eye = jnp.eye(per, dtype=w_gate.dtype)

    def build(w):
        w = w.reshape(n // per, per, k, k)
        return jnp.einsum("gmkj,mn->gmknj", w, eye).reshape(n // per, GATE_GROUP, GATE_GROUP).astype(BF16)

    return build(w_gate[:, :, :k]), build(w_gate[:, :, k:])


def kernel(x_prompt, x_sample, c_prompt, c_sample, state_conv, state_rglru, cache_k, cache_v, page_table, g_norm, w_ada, b_ada, w_rec_in, conv_w, conv_b, w_gate, b_gate, lam, w_rec_out, g_kv, w_ada_kv, b_ada_kv, w_kv, g_k, w_q, g_q, w_o, w_up, w_down):
    bp, sp, d = x_prompt.shape
    db, sq, _ = x_sample.shape
    depth = g_norm.shape[0]
    n_a = w_rec_in.shape[0]
    dr = w_rec_in.shape[2] // 2
    hd = g_k.shape[0]
    n_heads = w_q.shape[2] // hd
    n_pages = page_table.shape[1]
    d_ff = w_up.shape[2]
    assert hd == LANES and n_heads == N_HEADS and sq <= SUBLANES and n_pages % PAGES_PER_BLOCK == 0

    c_all = jnp.concatenate([c_prompt, c_sample], axis=0)
    pad = (-c_all.shape[0]) % SUBLANES
    c_all = jnp.pad(c_all, ((0, pad), (0, 0)))
    mods = _adaln(c_all, w_ada.reshape(depth * 2, d, 3 * d), b_ada.reshape(depth * 2, 1, 3 * d))
    mods_kv = _adaln(c_all, w_ada_kv[None], b_ada_kv[None, None])[0]

    def group_mods(m, prompt):
        if prompt:
            return m[:bp, None, :]
        return jnp.repeat(m[bp:bp + db], sq, axis=0)[None]

    w_in16 = [w_rec_in[l].astype(BF16) for l in range(n_a)]
    w_out16 = [w_rec_out[l].astype(BF16) for l in range(n_a)]
    w_kv16 = _with_partner_columns(w_kv.astype(BF16), n_heads, hd)
    w_q16 = [_with_partner_columns(w_q[j].astype(BF16), n_heads, hd) for j in range(depth - n_a)]
    w_o16 = [w_o[j].astype(BF16) for j in range(depth - n_a)]
    w_up16, w_down16 = w_up.astype(BF16), w_down.astype(BF16)
    gates = [_block_diag_gate(w_gate[l]) for l in range(n_a)]
    pages_f = jnp.pad(page_table.astype(F32), ((0, 0), (0, LANES - n_pages)))[:, None, :]

    prompt_tm, scan_tt, host_tm = 512, 256, 256
    scan_steps = bp * (sp // scan_tt)
    gps = n_pages // PAGES_PER_MEANS_STEP
    fuse_means = (n_pages % PAGES_PER_MEANS_STEP == 0 and gps > 0 and scan_steps % gps == 0
                  and n_a * scan_steps >= db * gps)
    host_steps = bp * sp // host_tm
    host_attn = (sp % host_tm == 0 and host_steps % db == 0
                 and n_heads % (host_steps // db) == 0)
    means_parts = []

    def setup(x3, prompt):
        nb, seq, _ = x3.shape
        t = nb * seq
        tm = prompt_tm if prompt else t
        st = dict(prompt=prompt, nb=nb, seq=seq, t=t, x=x3.reshape(t, d), tm=tm,
                  tps=seq // tm if prompt else 1, conv_new=[], h_new=[])
        if prompt:
            pos = jnp.arange(seq, dtype=jnp.int32)
            st.update(tables=_rope_tables(pos, hd), tab_nb=nb, tab_seq=seq)
        else:
            pos = n_pages * PAGE_SIZE + jnp.arange(seq, dtype=jnp.int32)
            st.update(tables=tuple(jnp.tile(tb, (nb, 1)) for tb in _rope_tables(pos, hd)),
                      tab_nb=1, tab_seq=t)
        return st

    def pad_rows(a, nb, seq):
        a = a.reshape(n_heads, nb, seq, hd).transpose(1, 0, 2, 3)
        return jnp.pad(a, ((0, 0), (0, 0), (0, SUBLANES - seq), (0, 0)))

    def unpad_rows(o, nb, seq):
        return o[:, :, :seq].transpose(0, 2, 1, 3).reshape(nb * seq, n_heads * hd)

    def mixer(st, l):
        prompt, nb, seq, t, tm, tps, x = (st[k] for k in ("prompt", "nb", "seq", "t", "tm", "tps", "x"))
        mod = group_mods(mods[2 * l], prompt)
        g1 = g_norm[l, 0][None]
        st.update(mod=mod, paged=None)
        if l == n_a:
            k, v = _head_proj(x, group_mods(mods_kv, prompt), g_kv[None], w_kv16, g_k[None],
                              st["tables"], st["tab_nb"], st["tab_seq"], tm, n_heads, n_heads, hd)
            st.update(k=k, v=v)
            if not prompt:
                st.update(means=(jnp.concatenate(means_parts, axis=0) if fuse_means
                                 else _page_means(cache_k, page_table)),
                          k8=pad_rows(k, nb, seq), v8=pad_rows(v, nb, seq))
        if l < n_a:
            wr, wi = gates[l]
            cw, cb, bg, lm = conv_w[l], conv_b[l][None], b_gate[l], lam[l][None]
            if prompt:
                st8 = jnp.zeros((nb, SUBLANES, dr), F32)
                h0 = jnp.zeros((nb, 1, dr), F32)
                side = None
                first_seq = l * scan_steps // gps if fuse_means else db
                if first_seq < db:
                    side = (cache_k, page_table, first_seq, min(scan_steps // gps, db - first_seq))
                z, tail, h_last, *part = _scan_prompt(x, mod, g1, w_in16[l], st8, h0, cw, cb, wr,
                                                      wi, bg, lm, nb, seq, scan_tt, side)
                means_parts.extend(part)
                st["conv_new"].append(tail[:, SUBLANES - (CONV_WIDTH - 1):])
                st["h_new"].append(h_last[:, 0])
            else:
                ga, xb = _rec_in(x, mod, g1, w_in16[l], tm, tps)
                xb_t = xb.reshape(nb, seq, dr).transpose(1, 0, 2)
                ga_t = ga.reshape(nb, seq, dr).transpose(1, 0, 2)
                conv0 = state_conv[l].transpose(1, 0, 2)
                z_t, h_last = _scan_steps(xb_t, ga_t, conv0, state_rglru[l], cw, cb, wr, wi, bg, lm)
                z = z_t.transpose(1, 0, 2).reshape(t, dr)
                hist = jnp.concatenate([state_conv[l], xb.reshape(nb, seq, dr)], axis=1)
                st["conv_new"].append(hist[:, -(CONV_WIDTH - 1):])
                st["h_new"].append(h_last)
            st.update(z=z, w_mix=w_out16[l])
        else:
            j = l - n_a
            (q,) = _head_proj(x, mod, g1, w_q16[j], g_q[j][None], st["tables"], st["tab_nb"],
                              st["tab_seq"], tm, n_heads, 0, hd)
            if prompt:
                st.update(z=_attn_prompt(q, st["k"], st["v"]).reshape(t, n_heads * hd))
            else:
                q8 = pad_rows(q, nb, seq)
                q_bd = jnp.einsum("bhqd,hg->bhqgd", q8, jnp.eye(n_heads, dtype=F32))
                sel = _select(q_bd.reshape(nb, n_heads * SUBLANES, n_heads * hd), st["means"],
                              pages_f)
                sel = sel.reshape(nb, n_heads, SUBLANES, LANES)
                sel = sel[:, :, :seq, :MOBA_TOP_K * PAGES_PER_BLOCK].reshape(nb, -1)
                st.update(z=None, paged=(q8, st["k8"], st["v8"], cache_k, cache_v, sel, seq))
            st.update(w_mix=w_o16[j])

    def mlp(st, l, hosted=None):
        tm = host_tm if hosted is not None else st["tm"]
        out = _mix_mlp(st["x"], st["z"], st["mod"], st["w_mix"],
                       group_mods(mods[2 * l + 1], st["prompt"]), g_norm[l, 1][None],
                       w_up16, w_down16, l, tm, st["seq"] // tm if st["prompt"] else 1, hosted)
        if hosted is None:
            st["x"] = out
            return None
        st["x"] = out[0]
        return out[1]

    grp_p, grp_s = setup(x_prompt, True), setup(x_sample, False)
    for l in range(depth):
        mixer(grp_p, l)
        mixer(grp_s, l)
        paged = grp_s["paged"]
        if paged is not None and host_attn:
            o = mlp(grp_p, l, hosted=paged)
        else:
            mlp(grp_p, l)
            o = _attn_paged(*paged) if paged is not None else None
        if o is not None:
            grp_s["z"] = unpad_rows(o, grp_s["nb"], grp_s["seq"])
        mlp(grp_s, l)

    def finish(st):
        nb, seq = st["nb"], st["seq"]
        k, v = st["k"], st["v"]
        if not st["prompt"]:
            k = k.reshape(n_heads, nb, seq, hd).transpose(1, 0, 2, 3)
            v = v.reshape(n_heads, nb, seq, hd).transpose(1, 0, 2, 3)
        return (st["x"].reshape(nb, seq, d), jnp.stack(st["conv_new"]), jnp.stack(st["h_new"]),
                k, v)

    y_p, conv_p, h_p, k_p, v_p = finish(grp_p)
    y_s, conv_s, h_s, k_s, v_s = finish(grp_s)
    return (y_p, y_s, conv_p, h_p, conv_s, h_s, k_p, v_p, k_s, v_s)
```

```python
import functools
import math

import jax
import jax.numpy as jnp
from jax import lax
from jax.experimental import pallas as pl
from jax.experimental.pallas import tpu as pltpu

F32 = jnp.float32
BF16 = jnp.bfloat16

NORM_EPS = 1e-6
LRU_C = 8.0
CONV_WIDTH = 4
PAGE_SIZE = 128
MOBA_BLOCK = 256
MOBA_TOP_K = 3
PAGES_PER_BLOCK = MOBA_BLOCK // PAGE_SIZE
N_HEADS = 8
ROPE_THETA = 500000.0
NEG = -1e30
LOG2_E = math.log2(math.e)
LANES = 128
SUBLANES = 8
GATE_GROUP = 256
VMEM_LIMIT = 56 * 1024 * 1024


def _params(*sem):
    return pltpu.CompilerParams(dimension_semantics=sem, vmem_limit_bytes=VMEM_LIMIT)


def _resident(shape):
    nd = len(shape)
    return pl.BlockSpec(shape, lambda *_: (0,) * nd, pipeline_mode=pl.Buffered(1))


def _resident_layer(shape, layer):
    nd = len(shape)
    return pl.BlockSpec((1,) + tuple(shape[1:]), lambda *_: (layer,) + (0,) * (nd - 1),
                        pipeline_mode=pl.Buffered(1))


def _dot(a, b):
    return jnp.dot(a, b, preferred_element_type=F32)


def _dot_nt(a, b):
    return lax.dot_general(a, b, (((1,), (1,)), ((), ())), preferred_element_type=F32)


def _split(a):
    hi = a.astype(BF16)
    return hi, (a - hi.astype(F32)).astype(BF16)


def _dot3(a, b, dot=_dot):
    ah, al = _split(a)
    bh, bl = _split(b)
    return dot(ah, bh) + (dot(ah, bl) + dot(al, bh))


def _sigmoid(x):
    return 0.5 + 0.5 * jnp.tanh(0.5 * x)


def _gelu_tanh(x):
    return 0.5 * x * (1.0 + jnp.tanh(math.sqrt(2.0 / math.pi) * (x + 0.044715 * (x * x * x))))


def _softplus(x):
    return jnp.maximum(x, 0.0) + jnp.log1p(jnp.exp(-jnp.abs(x)))


def _modulate(x, g, mod, d):
    ms = jnp.mean(x * x, axis=-1, keepdims=True)
    y = x * lax.rsqrt(ms + NORM_EPS) * g
    return y * (1.0 + mod[:, d:2 * d]) + mod[:, :d]


def _adaln_kernel(c_ref, w_ref, b_ref, o_ref):
    c = c_ref[...]
    a = c * _sigmoid(c)
    r = a.shape[0]
    a_hi = a.astype(BF16).astype(F32)
    w_hi, w_lo = _split(w_ref[0])
    stacked = _dot(jnp.concatenate([a_hi, a - a_hi], axis=0).astype(BF16), w_hi)
    o_ref[0] = stacked[:r] + (stacked[r:] + _dot(a_hi.astype(BF16), w_lo)) + b_ref[0]


def _adaln(c, w, b):
    r, d = c.shape
    m, _, n = w.shape
    tn = 1024
    return pl.pallas_call(
        _adaln_kernel,
        grid=(m, n // tn),
        in_specs=[
            pl.BlockSpec((r, d), lambda i, j: (0, 0)),
            pl.BlockSpec((1, d, tn), lambda i, j: (i, 0, j)),
            pl.BlockSpec((1, 1, tn), lambda i, j: (i, 0, j)),
        ],
        out_specs=pl.BlockSpec((1, r, tn), lambda i, j: (i, 0, j)),
        out_shape=jax.ShapeDtypeStruct((m, r, n), F32),
        compiler_params=_params("parallel", "parallel"),
        name="adaln",
    )(c, w, b)


def _mod_spec(mod, idx, nb, tm, tiles_per_seq):
    _, rows, w = mod.shape
    if rows == 1:
        return pl.BlockSpec((1, 1, w), lambda i, *_: (idx * nb + i // tiles_per_seq, 0, 0))
    return pl.BlockSpec((1, tm, w), lambda i, *_: (idx, i, 0))


ROW_G1, ROW_G2, ROW_CONV_B, ROW_LAM, ROW_GATE_B, ROW_CONV_W, PACK_ROWS = 0, 1, 2, 3, 4, 6, 16


def _row(pk_ref, r, n=1):
    return pk_ref[0, r:r + n, :]


def _rec_in_kernel(x_ref, mod_ref, pk_ref, w_ref, ga_ref, xb_ref, *, d, dr):
    h = _modulate(x_ref[...], _row(pk_ref, ROW_G1), mod_ref[0], d).astype(BF16)
    y = _dot(h, w_ref[...])
    ga_ref[...] = _gelu_tanh(y[:, :dr])
    xb_ref[...] = y[:, dr:]


def _rec_in(x, mod, mod_idx, nb, pk, layer, w, tm, tiles_per_seq):
    t, d = x.shape
    dr = w.shape[1] // 2
    row = pl.BlockSpec((tm, dr), lambda i: (i, 0))
    return pl.pallas_call(
        functools.partial(_rec_in_kernel, d=d, dr=dr),
        grid=(t // tm,),
        in_specs=[
            pl.BlockSpec((tm, d), lambda i: (i, 0)),
            _mod_spec(mod, mod_idx, nb, tm, tiles_per_seq),
            _resident_layer(pk.shape, layer),
            _resident(w.shape),
        ],
        out_specs=[row, row],
        out_shape=[jax.ShapeDtypeStruct((t, dr), F32)] * 2,
        compiler_params=_params("parallel"),
        name="rec_in",
    )(x, mod, pk, w)


def _rope_partner(hd):
    half = hd // 8
    i = jnp.arange(hd)
    return jnp.where(i < half, i + half, jnp.where(i < 2 * half, i - half, i))


def _with_partner_columns(w, n_rope, hd):
    d = w.shape[0]
    wr = w[:, :n_rope * hd].reshape(d, n_rope, hd)[:, :, _rope_partner(hd)]
    return jnp.concatenate([w, wr.reshape(d, n_rope * hd)], axis=1)


def _head_proj_kernel(x_ref, mod_ref, pk_ref, w_ref, gh_ref, ghp_ref, cos_ref, sin_ref, *out_refs,
                      d, hd, n_rope, n_plain):
    h_ref = out_refs[-1]
    h_ref[...] = _modulate(x_ref[...], _row(pk_ref, ROW_G1), mod_ref[0], d).astype(BF16)
    cg = cos_ref[...] * gh_ref[0]
    sg = sin_ref[...] * ghp_ref[0]
    partner0 = n_rope + n_plain
    roped, plain = [], []
    for i0 in range(0, n_rope + n_plain, 2):
        y = _dot(h_ref[...], w_ref[:, i0 * hd:(i0 + 2) * hd])
        if i0 < n_rope:
            yp = _dot(h_ref[...], w_ref[:, (partner0 + i0) * hd:(partner0 + i0 + 2) * hd])
        for i in (i0, i0 + 1):
            cols = slice((i - i0) * hd, (i - i0 + 1) * hd)
            if i >= n_rope:
                plain.append(y[:, cols])
                continue
            yh = y[:, cols]
            rs = lax.rsqrt(jnp.mean(yh * yh, axis=-1, keepdims=True) + NORM_EPS)
            roped.append(rs * (yh * cg + yp[:, cols] * sg))
    out_refs[0][0] = jnp.stack(roped, axis=0)
    if plain:
        out_refs[1][0] = jnp.stack(plain, axis=0)


def _head_proj(x, mod, mod_idx, mod_nb, pk, layer, w, gains, gain_idx, tables, nb, seq, tm,
               n_rope, n_plain, hd):
    t, d = x.shape
    assert n_rope % 2 == 0 and n_plain % 2 == 0
    tps = seq // tm
    tab = pl.BlockSpec((tm, hd), lambda i: (i % tps, 0))
    out_specs = [pl.BlockSpec((1, n_rope, tm, hd), lambda i: (i // tps, 0, i % tps, 0))]
    out_shape = [jax.ShapeDtypeStruct((nb, n_rope, seq, hd), F32)]
    if n_plain:
        out_specs.append(pl.BlockSpec((1, n_plain, tm, hd), lambda i: (i // tps, 0, i % tps, 0)))
        out_shape.append(jax.ShapeDtypeStruct((nb, n_plain, seq, hd), F32))
    return pl.pallas_call(
        functools.partial(_head_proj_kernel, d=d, hd=hd, n_rope=n_rope, n_plain=n_plain),
        grid=(t // tm,),
        in_specs=[
            pl.BlockSpec((tm, d), lambda i: (i, 0)),
            _mod_spec(mod, mod_idx, mod_nb, tm, tps),
            _resident_layer(pk.shape, layer),
            _resident(w.shape),
            _resident_layer(gains[0].shape, gain_idx),
            _resident_layer(gains[1].shape, gain_idx),
            tab, tab,
        ],
        out_specs=out_specs,
        out_shape=out_shape,
        scratch_shapes=[pltpu.VMEM((tm, d), BF16)],
        compiler_params=_params("parallel"),
        name="head_proj",
    )(x, mod, pk, w, *gains, *tables)


FF_CHUNK = 1024


PAGES_PER_MEANS_STEP = 16


def _page_block_means(k_refs, o_ref):
    n = len(k_refs)
    n_heads, _, hd = k_refs[0].shape[1:]
    for h in range(n_heads):
        sums = [jnp.sum(k_refs[j][0, h], axis=0, keepdims=True) for j in range(n)]
        o_ref[0, :, h * hd:(h + 1) * hd] = jnp.concatenate(
            [(sums[2 * i] + sums[2 * i + 1]) * (1.0 / MOBA_BLOCK) for i in range(n // 2)], axis=0)


def _mix_mlp_kernel(*refs, d, attn):
    if attn:
        sel_ref, refs = refs[0], refs[1:]
        _paged_heads_step(sel_ref, *refs[8:13], refs[14], *refs[18:21], **attn)
        refs = refs[:8] + (refs[13],) + refs[15:18]
    x_ref, z_ref, mod1_ref, wo_ref, mod2_ref, pk_ref, wu_ref, wd_ref, o_ref = refs[:9]
    x1_ref, h_ref, acc_ref = refs[-3:]
    x1 = x_ref[...] + mod1_ref[0][:, 2 * d:] * _dot(z_ref[...].astype(BF16), wo_ref[...])
    x1_ref[...] = x1
    mod2 = mod2_ref[0]
    h_ref[...] = _modulate(x1, _row(pk_ref, ROW_G2), mod2, d).astype(BF16)
    for c in range(wu_ref.shape[2] // FF_CHUNK):
        cols = slice(c * FF_CHUNK, (c + 1) * FF_CHUNK)
        u = _dot(h_ref[...], wu_ref[0, :, cols])
        a = jnp.square(jnp.maximum(u, 0.0)).astype(BF16)
        part = _dot(a, wd_ref[0, cols, :])
        if c == 0:
            acc_ref[...] = part
        else:
            acc_ref[...] += part
    o_ref[...] = x1_ref[...] + mod2[:, 2 * d:] * acc_ref[...]


def _mix_mlp(x, z, mods, mod_nb, wo, pk, wu, wd, layer, tm, tiles_per_seq, hosted=None):
    t, d = x.shape
    assert wu.shape[2] % FF_CHUNK == 0
    steps = t // tm
    row = pl.BlockSpec((tm, d), lambda i, *_: (i, 0))
    in_specs = [
        row,
        pl.BlockSpec((tm, z.shape[1]), lambda i, *_: (i, 0)),
        _mod_spec(mods, 2 * layer, mod_nb, tm, tiles_per_seq),
        _resident(wo.shape),
        _mod_spec(mods, 2 * layer + 1, mod_nb, tm, tiles_per_seq),
        _resident_layer(pk.shape, layer),
        _resident_layer(wu.shape, layer),
        _resident_layer(wd.shape, layer),
    ]
    scratch = [pltpu.VMEM((tm, d), F32), pltpu.VMEM((tm, d), BF16), pltpu.VMEM((tm, d), F32)]
    out_specs, out_shape = row, jax.ShapeDtypeStruct((t, d), F32)
    args = (x, z, mods, wo, mods, pk, wu, wd)
    if hosted is None:
        return pl.pallas_call(
            functools.partial(_mix_mlp_kernel, d=d, attn=None),
            grid=(steps,), in_specs=in_specs, out_specs=out_specs, out_shape=out_shape,
            scratch_shapes=scratch, compiler_params=_params("parallel"), name="mix_mlp",
        )(*args)
    q, k_new, v_new, cache_k, cache_v, sel, sq = hosted
    db, h, rows, hd = q.shape
    assert steps % db == 0 and h % (steps // db) == 0
    spq = steps // db
    hps = h // spq
    n_slots = sq * MOBA_TOP_K * PAGES_PER_BLOCK
    heads = pl.BlockSpec((1, hps, rows, hd), lambda i, sel: (i // spq, i % spq, 0, 0))
    hbm = pl.BlockSpec(memory_space=pl.ANY)
    return pl.pallas_call(
        functools.partial(_mix_mlp_kernel, d=d,
                          attn=dict(n_slots=n_slots, sq=sq, scale=hd ** -0.5, n_heads=h)),
        grid_spec=pltpu.PrefetchScalarGridSpec(
            num_scalar_prefetch=1,
            grid=(steps,),
            in_specs=in_specs + [heads, heads, heads, hbm, hbm],
            out_specs=[out_specs, heads],
            scratch_shapes=scratch + [pltpu.VMEM((hps, n_slots, PAGE_SIZE, hd), F32),
                                      pltpu.VMEM((hps, n_slots, PAGE_SIZE, hd), F32),
                                      pltpu.SemaphoreType.DMA((2,))],
        ),
        out_shape=[out_shape, jax.ShapeDtypeStruct((db, h, rows, hd), F32)],
        compiler_params=_params("arbitrary"),
        name="mix_mlp_attn",
    )(_page_rows(sel, h, n_slots), *args, q, k_new, v_new,
      cache_k.reshape(-1, PAGE_SIZE, hd), cache_v.reshape(-1, PAGE_SIZE, hd))


def _lru_inputs(xc, wr_ref, wi_ref, pk_ref):
    xh = xc.astype(BF16)
    gw = wr_ref.shape[2]
    r = jnp.concatenate([_dot(xh[:, j * gw:(j + 1) * gw], wr_ref[0, j])
                         for j in range(wr_ref.shape[1])], axis=1) + _row(pk_ref, ROW_GATE_B)
    i = jnp.concatenate([_dot(xh[:, j * gw:(j + 1) * gw], wi_ref[0, j])
                         for j in range(wi_ref.shape[1])], axis=1) + _row(pk_ref, ROW_GATE_B + 1)
    log_a = -LRU_C * _sigmoid(r) * _softplus(-_row(pk_ref, ROW_LAM))
    t = jnp.tanh(0.5 * log_a)
    rc = 1.0 / (1.0 - t)
    a = (1.0 + t) * rc
    u = (2.0 * rc) * jnp.sqrt(-t) * (_sigmoid(i) * xc)
    return a, u


def _scan_prompt_kernel(*refs, tt, n_side):
    if n_side:
        refs = refs[1:]
        _page_block_means(refs[6:6 + n_side], refs[9 + n_side])
    x_ref, mod_ref, pk_ref, win_ref, wr_ref, wi_ref = refs[:6]
    z_ref, tail_out_ref, hl_ref = refs[6 + n_side:9 + n_side]
    tail_ref, hc_ref, a_ref, u_ref, ga_ref = refs[-5:]
    ti = pl.program_id(1)
    d, dr = x_ref.shape[1], z_ref.shape[1]

    @pl.when(ti == 0)
    def _():
        tail_ref[...] = jnp.zeros(tail_ref.shape, F32)
        hc_ref[...] = jnp.zeros(hc_ref.shape, F32)

    y = _dot(_modulate(x_ref[...], _row(pk_ref, ROW_G1), mod_ref[0], d).astype(BF16),
             win_ref[...])
    ga_ref[...] = _gelu_tanh(y[:, :dr])
    x = y[:, dr:]
    tail = tail_ref[...]
    row8 = lax.broadcasted_iota(jnp.int32, (SUBLANES, dr), 0)
    xc = _row(pk_ref, ROW_CONV_B) + _row(pk_ref, ROW_CONV_W + CONV_WIDTH - 1) * x
    for k in range(1, CONV_WIDTH):
        xs = pltpu.roll(x, k, 0)
        first = jnp.where(row8 < k, pltpu.roll(tail, k, 0), xs[:SUBLANES])
        xs = jnp.concatenate([first, xs[SUBLANES:]], axis=0)
        xc = xc + _row(pk_ref, ROW_CONV_W + CONV_WIDTH - 1 - k) * xs
    tail_ref[...] = x[tt - SUBLANES:]
    tail_out_ref[0] = x[tt - SUBLANES:]

    a, u = _lru_inputs(xc, wr_ref, wi_ref, pk_ref)
    a_ref[...] = a
    u_ref[...] = u

    def group(gi, carry):
        r0 = pl.multiple_of(gi * SUBLANES, SUBLANES)
        a8 = a_ref[pl.ds(r0, SUBLANES), :]
        u8 = u_ref[pl.ds(r0, SUBLANES), :]
        for s in (1, 2, 4):
            keep = row8 >= s
            u8 = jnp.where(keep, a8 * pltpu.roll(u8, s, 0) + u8, u8)
            a8 = jnp.where(keep, a8 * pltpu.roll(a8, s, 0), a8)
        h8 = a8 * carry + u8
        z_ref[pl.ds(r0, SUBLANES), :] = h8 * ga_ref[pl.ds(r0, SUBLANES), :]
        return jnp.broadcast_to(h8[SUBLANES - 1:SUBLANES, :], (SUBLANES, dr))

    carry = lax.fori_loop(0, tt // SUBLANES, group,
                          jnp.broadcast_to(hc_ref[...], (SUBLANES, dr)), unroll=4)
    hc_ref[...] = carry[0:1, :]
    hl_ref[0] = carry[0:1, :]


def _scan_prompt(x, mods, mod_idx, pk, layer, w_in, wr, wi, nb, seq, tt, side=None):
    t, d = x.shape
    dr = w_in.shape[1] // 2
    nt = seq // tt
    row = pl.BlockSpec((tt, dr), lambda b, i, *_: (b * nt + i, 0))
    per_seq8 = pl.BlockSpec((1, SUBLANES, dr), lambda b, i, *_: (b, 0, 0))
    per_seq1 = pl.BlockSpec((1, 1, dr), lambda b, i, *_: (b, 0, 0))
    in_specs = [pl.BlockSpec((tt, d), lambda b, i, *_: (b * nt + i, 0)),
                pl.BlockSpec((1, 1, mods.shape[2]), lambda b, i, *_: (mod_idx * nb + b, 0, 0)),
                _resident_layer(pk.shape, layer), _resident(w_in.shape),
                _resident_layer(wr.shape, layer), _resident_layer(wi.shape, layer)]
    out_specs = [row, per_seq8, per_seq1]
    out_shape = [jax.ShapeDtypeStruct((t, dr), F32),
                 jax.ShapeDtypeStruct((nb, SUBLANES, dr), F32),
                 jax.ShapeDtypeStruct((nb, 1, dr), F32)]
    scratch = [pltpu.VMEM((SUBLANES, dr), F32), pltpu.VMEM((1, dr), F32),
               pltpu.VMEM((tt, dr), F32), pltpu.VMEM((tt, dr), F32), pltpu.VMEM((tt, dr), F32)]
    args = (x, mods, pk, w_in, wr, wi)
    if side is None:
        return pl.pallas_call(
            functools.partial(_scan_prompt_kernel, tt=tt, n_side=0),
            grid=(nb, nt), in_specs=in_specs, out_specs=out_specs, out_shape=out_shape,
            scratch_shapes=scratch, compiler_params=_params("parallel", "arbitrary"),
            name="scan_prompt",
        )(*args)
    cache_k, page_table, first_seq, n_seqs = side
    n = PAGES_PER_MEANS_STEP
    n_pages = page_table.shape[1]
    _, h, _, hd = cache_k.shape
    gps = n_pages // n
    last = n_seqs * gps - 1

    def group(b, i):
        return jnp.minimum(b * nt + i, last)

    def page(j):
        return pl.BlockSpec(
            (1, h, PAGE_SIZE, hd),
            lambda b, i, pt: (pt[first_seq + group(b, i) // gps, (group(b, i) % gps) * n + j],
                              0, 0, 0))

    return pl.pallas_call(
        functools.partial(_scan_prompt_kernel, tt=tt, n_side=n),
        grid_spec=pltpu.PrefetchScalarGridSpec(
            num_scalar_prefetch=1,
            grid=(nb, nt),
            in_specs=in_specs + [page(j) for j in range(n)],
            out_specs=out_specs + [pl.BlockSpec(
                (1, n // PAGES_PER_BLOCK, h * hd),
                lambda b, i, pt: (group(b, i) // gps, group(b, i) % gps, 0))],
            scratch_shapes=scratch,
        ),
        out_shape=out_shape + [
            jax.ShapeDtypeStruct((n_seqs, n_pages // PAGES_PER_BLOCK, h * hd), F32)],
        compiler_params=_params("arbitrary", "arbitrary"),
        name="scan_prompt_means",
    )(page_table, *args, *([cache_k] * n))


def _scan_steps_kernel(xb_ref, ga_ref, st_ref, h0_ref, pk_ref, wr_ref, wi_ref, z_ref, hl_ref,
                       *, steps):
    rows = [st_ref[0, j] for j in range(CONV_WIDTH - 1)] + [xb_ref[t] for t in range(steps)]
    h = h0_ref[0]
    for t in range(steps):
        xc = _row(pk_ref, ROW_CONV_B)
        for j in range(CONV_WIDTH):
            xc = xc + _row(pk_ref, ROW_CONV_W + j) * rows[t + j]
        a, u = _lru_inputs(xc, wr_ref, wi_ref, pk_ref)
        h = a * h + u
        z_ref[t] = h * ga_ref[t]
    hl_ref[...] = h


def _scan_steps(xb, ga, st, h0, pk, wr, wi, layer):
    steps, nb, dr = xb.shape
    args = (xb, ga, st, h0, pk, wr, wi)
    return pl.pallas_call(
        functools.partial(_scan_steps_kernel, steps=steps),
        grid=(1,),
        in_specs=[_resident(xb.shape), _resident(ga.shape)]
        + [_resident_layer(a.shape, layer) for a in args[2:]],
        out_specs=[pl.BlockSpec((steps, nb, dr), lambda i: (0, 0, 0)),
                   pl.BlockSpec((nb, dr), lambda i: (0, 0))],
        out_shape=[jax.ShapeDtypeStruct((steps, nb, dr), F32),
                   jax.ShapeDtypeStruct((nb, dr), F32)],
        compiler_params=_params("arbitrary"),
        name="scan_steps",
    )(*args)


ATTN_KEY_CHUNK = 128


def _block_bias_t(gate_t, cur):
    row = lax.broadcasted_iota(jnp.int32, gate_t.shape, 0)
    row_f = row.astype(F32)
    g = jnp.where(row < cur, gate_t, NEG)
    bias = jnp.where(row == cur, 0.0, NEG)
    for _ in range(min(MOBA_TOP_K, cur)):
        m = jnp.max(g, axis=0, keepdims=True)
        idx = jnp.min(jnp.where(g == m, row_f, float(LANES)), axis=0, keepdims=True)
        pick = row_f == idx
        bias = jnp.where(pick, 0.0, bias)
        g = jnp.where(pick, NEG, g)
    return bias


def _attn_prompt_kernel(q_ref, k_ref, v_ref, o_ref, kaug_ref, vt_ref, means_ref, *, seq, hd, scale):
    bq = MOBA_BLOCK
    nblk = seq // bq
    nbp = means_ref.shape[0]

    lane = lax.broadcasted_iota(jnp.int32, (bq, hd), 1)
    means_ref[...] = jnp.zeros(means_ref.shape, F32)
    for n in range(nblk):
        kb = k_ref[0, 0, n * bq:(n + 1) * bq, :]
        means_ref[n:n + 1, :] = jnp.sum(kb, axis=0, keepdims=True) * (1.0 / MOBA_BLOCK)
        onehot = jnp.where(lane == n, 1.0, 0.0).astype(BF16)
        kaug_ref[n * bq:(n + 1) * bq, :] = jnp.concatenate([kb.astype(BF16), onehot], axis=1)
        vt_ref[:, n * bq:(n + 1) * bq] = v_ref[0, 0, n * bq:(n + 1) * bq, :].T.astype(BF16)

    kc = ATTN_KEY_CHUNK
    key_idx = lax.broadcasted_iota(jnp.int32, (kc, bq), 0)
    query_idx = lax.broadcasted_iota(jnp.int32, (kc, bq), 1)

    outs = []
    for c in range(nblk):
        qt = q_ref[0, 0, c * bq:(c + 1) * bq, :].T
        gate_t = _dot3(means_ref[...], qt)
        qt_aug = jnp.concatenate(
            [qt * (scale * LOG2_E), _block_bias_t(gate_t, c), jnp.zeros((hd - nbp, bq), F32)],
            axis=0).astype(BF16)
        m = l = acc = None
        own = list(range(c * bq // kc, (c + 1) * bq // kc))
        for j in own + list(range(c * bq // kc)):
            s = _dot(kaug_ref[j * kc:(j + 1) * kc, :], qt_aug)
            if j in own:
                s = jnp.where(key_idx + (j * kc - c * bq) <= query_idx, s, NEG)
            mn = jnp.max(s, axis=0, keepdims=True)
            p = jnp.exp2(s - mn)
            ln = jnp.sum(p, axis=0, keepdims=True)
            on = _dot(vt_ref[:, j * kc:(j + 1) * kc], p.astype(BF16))
            if m is None:
                m, l, acc = mn, ln, on
            else:
                m_new = jnp.maximum(m, mn)
                wa, wb = jnp.exp2(m - m_new), jnp.exp2(mn - m_new)
                m, l, acc = m_new, l * wa + ln * wb, acc * wa + on * wb
        outs.append((acc * (1.0 / l)).T)
    o_ref[0] = jnp.concatenate(outs, axis=0)


def _attn_prompt(q, k, v):
    b, h, seq, hd = q.shape
    nblk = seq // MOBA_BLOCK
    assert seq % MOBA_BLOCK == 0 and nblk <= hd and hd == LANES
    nbp = -(-nblk // SUBLANES) * SUBLANES
    full = pl.BlockSpec((1, 1, seq, hd), lambda bi, hi: (bi, hi, 0, 0))
    return pl.pallas_call(
        functools.partial(_attn_prompt_kernel, seq=seq, hd=hd, scale=hd ** -0.5),
        grid=(b, h),
        in_specs=[full, full, full],
        out_specs=pl.BlockSpec((1, seq, hd), lambda bi, hi: (bi, 0, hi)),
        out_shape=jax.ShapeDtypeStruct((b, seq, h * hd), F32),
        scratch_shapes=[pltpu.VMEM((seq, 2 * hd), BF16), pltpu.VMEM((hd, seq), BF16),
                        pltpu.VMEM((nbp, hd), F32)],
        compiler_params=_params("parallel", "parallel"),
        name="attn_prompt",
    )(q, k, v)


def _page_means_kernel(pt_ref, *refs, n):
    _page_block_means(refs[:n], refs[n])


def _page_means(cache_k, page_table):
    db, n_pages = page_table.shape
    _, h, _, hd = cache_k.shape
    n = min(PAGES_PER_MEANS_STEP, n_pages)

    def page(j):
        return pl.BlockSpec((1, h, PAGE_SIZE, hd), lambda b, g, pt: (pt[b, g * n + j], 0, 0, 0))

    return pl.pallas_call(
        functools.partial(_page_means_kernel, n=n),
        grid_spec=pltpu.PrefetchScalarGridSpec(
            num_scalar_prefetch=1,
            grid=(db, n_pages // n),
            in_specs=[page(j) for j in range(n)],
            out_specs=pl.BlockSpec((1, n // 2, h * hd), lambda b, g, pt: (b, g, 0)),
        ),
        out_shape=jax.ShapeDtypeStruct((db, n_pages // 2, h * hd), F32),
        compiler_params=_params("parallel", "arbitrary"),
        name="page_means",
    )(page_table, *([cache_k] * n))


def _select_kernel(q_ref, means_ref, pt_ref, sel_ref):
    n_blocks = means_ref.shape[1]
    outs = []
    for i in range(q_ref.shape[0]):
        means = jnp.concatenate(
            [means_ref[i], jnp.zeros((LANES - n_blocks, means_ref.shape[2]), F32)], axis=0)
        gate = _dot3(q_ref[i], means, _dot_nt)
        lane = lax.broadcasted_iota(jnp.int32, gate.shape, 1)
        lane_f = lane.astype(F32)
        pages = pt_ref[i]
        g = jnp.where(lane < n_blocks, gate, NEG)
        out = jnp.zeros(gate.shape, F32)
        for j in range(MOBA_TOP_K):
            m = jnp.max(g, axis=1, keepdims=True)
            idx = jnp.min(jnp.where(g == m, lane_f, float(LANES)), axis=1, keepdims=True)
            for p in range(PAGES_PER_BLOCK):
                phys = jnp.sum(jnp.where(lane_f == idx * PAGES_PER_BLOCK + p, pages, 0.0),
                               axis=1, keepdims=True)
                out = jnp.where(lane == j * PAGES_PER_BLOCK + p, phys, out)
            g = jnp.where(lane_f == idx, NEG, g)
        outs.append(out.astype(jnp.int32))
    sel_ref[...] = jnp.stack(outs, axis=0)


def _select(q_bd, means, pages_f):
    db, r, w = q_bd.shape
    nb = means.shape[1]
    assert MOBA_TOP_K <= nb and nb * PAGES_PER_BLOCK <= LANES
    g = math.gcd(db, 8)
    return pl.pallas_call(
        _select_kernel,
        grid=(db // g,),
        in_specs=[pl.BlockSpec((g, r, w), lambda b: (b, 0, 0)),
                  pl.BlockSpec((g, nb, w), lambda b: (b, 0, 0)),
                  pl.BlockSpec((g, 1, LANES), lambda b: (b, 0, 0))],
        out_specs=pl.BlockSpec((g, r, LANES), lambda b: (b, 0, 0)),
        out_shape=jax.ShapeDtypeStruct((db, r, LANES), jnp.int32),
        compiler_params=_params("parallel"),
        name="select_blocks",
    )(q_bd, means, pages_f)


def _attn_paged_kernel(sel_ref, q_ref, kn_ref, vn_ref, ck_ref, cv_ref, o_ref, kbuf, vbuf, sems,
                       *, n_slots, sq, scale):
    b = pl.program_id(0)
    n_heads = q_ref.shape[1]
    assert n_heads % 2 == 0

    def page_copies(seq, h, page_of_slot):
        out = []
        for s in range(n_slots):
            row = page_of_slot(seq, h * n_slots + s)
            out.append(pltpu.make_async_copy(ck_ref.at[row], kbuf.at[h % 2, s], sems.at[0, h % 2]))
            out.append(pltpu.make_async_copy(cv_ref.at[row], vbuf.at[h % 2, s], sems.at[1, h % 2]))
        return out

    def start_all(copies):
        for i, c in enumerate(copies):
            c.start(priority=i % 2)

    @pl.when(b == 0)
    def _():
        start_all(page_copies(b, 0, lambda seq, i: sel_ref[seq, i]))

    for h in range(n_heads):
        if h + 1 < n_heads:
            start_all(page_copies(b, h + 1, lambda seq, i: sel_ref[seq, i]))
        else:
            @pl.when(b + 1 < pl.num_programs(0))
            def _():
                start_all(page_copies(b + 1, 0, lambda seq, i: sel_ref[seq, i]))
        for c in page_copies(0, h, lambda seq, i: 0):
            c.wait()
        o_ref[0, h] = _paged_head(q_ref[0, h], kn_ref[0, h], vn_ref[0, h], kbuf[h % 2],
                                  vbuf[h % 2], sq, scale)


def _page_rows(sel, n_heads, n_slots):
    head_of_slot = jnp.repeat(jnp.arange(n_heads, dtype=jnp.int32), n_slots)
    return sel * n_heads + head_of_slot[None, :]


def _paged_heads_step(sel_ref, q_ref, kn_ref, vn_ref, ck_ref, cv_ref, o_ref, kbuf, vbuf, sems,
                      *, n_slots, sq, scale, n_heads):
    i = pl.program_id(0)
    hps = q_ref.shape[1]
    spq = n_heads // hps

    def page_copies(step, page_of):
        seq, h0 = step // spq, (step % spq) * hps
        out = []
        for j in range(hps):
            for s in range(n_slots):
                row = page_of(seq, (h0 + j) * n_slots + s)
                out.append(pltpu.make_async_copy(ck_ref.at[row], kbuf.at[j, s], sems.at[0]))
                out.append(pltpu.make_async_copy(cv_ref.at[row], vbuf.at[j, s], sems.at[1]))
        return out

    def start_all(copies):
        for n, c in enumerate(copies):
            c.start(priority=n % 2)

    @pl.when(i == 0)
    def _():
        start_all(page_copies(i, lambda seq, n: sel_ref[seq, n]))

    for c in page_copies(0, lambda seq, n: 0):
        c.wait()
    for j in range(hps):
        o_ref[0, j] = _paged_head(q_ref[0, j], kn_ref[0, j], vn_ref[0, j], kbuf[j], vbuf[j],
                                  sq, scale)

    @pl.when(i + 1 < pl.num_programs(0))
    def _():
        start_all(page_copies(i + 1, lambda seq, n: sel_ref[seq, n]))


def _paged_head(q, kn, vn, k_pages, v_pages, sq, scale):
    n_slots, _, hd = k_pages.shape
    k_all = k_pages.reshape(n_slots * PAGE_SIZE, hd).astype(BF16)
    v_all = v_pages.reshape(n_slots * PAGE_SIZE, hd).astype(BF16)
    s = _dot_nt((q * scale).astype(BF16), k_all)
    keys_per_q = MOBA_TOP_K * MOBA_BLOCK
    row = lax.broadcasted_iota(jnp.int32, s.shape, 0)
    col = lax.broadcasted_iota(jnp.int32, s.shape, 1)
    s = jnp.where((col >= row * keys_per_q) & (col < (row + 1) * keys_per_q), s, NEG)
    row1 = lax.broadcasted_iota(jnp.int32, (q.shape[0], 1), 0)
    s_new = [jnp.where(row1 >= j, jnp.sum(q * kn[j:j + 1, :], axis=1, keepdims=True) * scale, NEG)
             for j in range(sq)]
    m = functools.reduce(jnp.maximum, s_new + [jnp.max(s, axis=1, keepdims=True)])
    p = jnp.exp(s - m)
    p_new = [jnp.exp(sj - m) for sj in s_new]
    l = functools.reduce(lambda x, y: x + y, p_new + [jnp.sum(p, axis=1, keepdims=True)])
    acc = functools.reduce(lambda x, y: x + y,
                           [p_new[j] * vn[j:j + 1, :] for j in range(sq)]
                           + [_dot(p.astype(BF16), v_all)])
    return acc / l


def _attn_paged(q, k_new, v_new, cache_k, cache_v, sel, sq):
    db, h, rows, hd = q.shape
    n_slots = sq * MOBA_TOP_K * PAGES_PER_BLOCK
    per_seq = pl.BlockSpec((1, h, rows, hd), lambda b, sel: (b, 0, 0, 0))
    hbm = pl.BlockSpec(memory_space=pl.ANY)
    sel_rows = _page_rows(sel, h, n_slots)
    return pl.pallas_call(
        functools.partial(_attn_paged_kernel, n_slots=n_slots, sq=sq, scale=hd ** -0.5),
        grid_spec=pltpu.PrefetchScalarGridSpec(
            num_scalar_prefetch=1,
            grid=(db,),
            in_specs=[per_seq, per_seq, per_seq, hbm, hbm],
            out_specs=per_seq,
            scratch_shapes=[pltpu.VMEM((2, n_slots, PAGE_SIZE, hd), F32),
                            pltpu.VMEM((2, n_slots, PAGE_SIZE, hd), F32),
                            pltpu.SemaphoreType.DMA((2, 2))],
        ),
        out_shape=jax.ShapeDtypeStruct((db, h, rows, hd), F32),
        compiler_params=_params("arbitrary"),
        name="attn_paged",
    )(sel_rows, q, k_new, v_new, cache_k.reshape(-1, PAGE_SIZE, hd),
      cache_v.reshape(-1, PAGE_SIZE, hd))


def _rope_tables(pos, hd):
    half = hd // 8
    inv = ROPE_THETA ** (-jnp.arange(half, dtype=F32) * 2.0 / (2 * half))
    ang = pos.astype(F32)[:, None] * inv[None, :]
    cos, sin = jnp.cos(ang), jnp.sin(ang)
    n = pos.shape[0]
    ones = jnp.ones((n, hd - 2 * half), F32)
    zeros = jnp.zeros((n, hd - 2 * half), F32)
    return (jnp.concatenate([cos, cos, ones], axis=1),
            jnp.concatenate([-sin, sin, zeros], axis=1))


def _block_diag_gate(w_gate):
    n, k, _ = w_gate.shape
    per = GATE_GROUP // k
    eye = jnp.eye(per, dtype=w_gate.dtype)

    def build(w):
        w = w.reshape(n // per, per, k, k)
        return jnp.einsum("gmkj,mn->gmknj", w, eye).reshape(n // per, GATE_GROUP, GATE_GROUP).astype(BF16)

    return build(w_gate[:, :, :k]), build(w_gate[:, :, k:])


def kernel(x_prompt, x_sample, c_prompt, c_sample, state_conv, state_rglru, cache_k, cache_v, page_table, g_norm, w_ada, b_ada, w_rec_in, conv_w, conv_b, w_gate, b_gate, lam, w_rec_out, g_kv, w_ada_kv, b_ada_kv, w_kv, g_k, w_q, g_q, w_o, w_up, w_down):
    bp, sp, d = x_prompt.shape
    db, sq, _ = x_sample.shape
    depth = g_norm.shape[0]
    n_a = w_rec_in.shape[0]
    dr = w_rec_in.shape[2] // 2
    hd = g_k.shape[0]
    n_heads = w_q.shape[2] // hd
    n_pages = page_table.shape[1]
    d_ff = w_up.shape[2]
    assert hd == LANES and n_heads == N_HEADS and sq <= SUBLANES and n_pages % PAGES_PER_BLOCK == 0

    c_all = jnp.concatenate([c_prompt, c_sample], axis=0)
    pad = (-c_all.shape[0]) % SUBLANES
    c_all = jnp.pad(c_all, ((0, pad), (0, 0)))
    mods = _adaln(c_all, w_ada.reshape(depth * 2, d, 3 * d), b_ada.reshape(depth * 2, 1, 3 * d))
    mods_kv = _adaln(c_all, w_ada_kv[None], b_ada_kv[None, None])

    def group_mods(m, prompt):
        if prompt:
            return m[:, :bp].reshape(m.shape[0] * bp, 1, m.shape[2])
        return jnp.repeat(m[:, bp:bp + db], sq, axis=1)

    assert d == dr
    n_b = depth - n_a
    pack = jnp.concatenate(
        [g_norm,
         jnp.pad(jnp.concatenate([conv_b[:, None], lam[:, None], b_gate, conv_w], axis=1),
                 ((0, n_b), (0, 0), (0, 0))),
         jnp.zeros((depth, PACK_ROWS - ROW_CONV_W - CONV_WIDTH, d), F32)], axis=1)
    pack_kv = jnp.pad(g_kv[None, None], ((0, 0), (0, PACK_ROWS - 1), (0, 0)))
    head_gains = jnp.concatenate([g_k[None], g_q], axis=0)[:, None]
    head_gains = (head_gains, head_gains[:, :, _rope_partner(hd)])

    w_in16 = [w_rec_in[l].astype(BF16) for l in range(n_a)]
    w_out16 = [w_rec_out[l].astype(BF16) for l in range(n_a)]
    w_kv16 = _with_partner_columns(w_kv.astype(BF16), n_heads, hd)
    w_q16 = [_with_partner_columns(w_q[j].astype(BF16), n_heads, hd) for j in range(depth - n_a)]
    w_o16 = [w_o[j].astype(BF16) for j in range(depth - n_a)]
    w_up16, w_down16 = w_up.astype(BF16), w_down.astype(BF16)
    gates = [_block_diag_gate(w_gate[l]) for l in range(n_a)]
    wr_all, wi_all = (jnp.stack([g[i] for g in gates]) for i in range(2))
    conv0_t = state_conv.transpose(0, 2, 1, 3)
    pages_f = jnp.pad(page_table.astype(F32), ((0, 0), (0, LANES - n_pages)))[:, None, :]

    prompt_tm, scan_tt, host_tm = 512, 256, 256
    scan_steps = bp * (sp // scan_tt)
    gps = n_pages // PAGES_PER_MEANS_STEP
    fuse_means = (n_pages % PAGES_PER_MEANS_STEP == 0 and gps > 0 and scan_steps % gps == 0
                  and n_a * scan_steps >= db * gps)
    host_steps = bp * sp // host_tm
    host_attn = (sp % host_tm == 0 and host_steps % db == 0
                 and n_heads % (host_steps // db) == 0)
    means_parts = []

    def setup(x3, prompt):
        nb, seq, _ = x3.shape
        t = nb * seq
        tm = prompt_tm if prompt else t
        st = dict(prompt=prompt, nb=nb, seq=seq, t=t, x=x3.reshape(t, d), tm=tm,
                  tps=seq // tm if prompt else 1, conv_new=[], h_new=[],
                  mods=group_mods(mods, prompt), mods_kv=group_mods(mods_kv, prompt))
        if prompt:
            pos = jnp.arange(seq, dtype=jnp.int32)
            st.update(tables=_rope_tables(pos, hd), tab_nb=nb, tab_seq=seq)
        else:
            pos = n_pages * PAGE_SIZE + jnp.arange(seq, dtype=jnp.int32)
            st.update(tables=tuple(jnp.tile(tb, (nb, 1)) for tb in _rope_tables(pos, hd)),
                      tab_nb=1, tab_seq=t)
        return st

    def pad_rows(a, nb, seq):
        a = a.reshape(n_heads, nb, seq, hd).transpose(1, 0, 2, 3)
        return jnp.pad(a, ((0, 0), (0, 0), (0, SUBLANES - seq), (0, 0)))

    def unpad_rows(o, nb, seq):
        return o[:, :, :seq].transpose(0, 2, 1, 3).reshape(nb * seq, n_heads * hd)

    def mixer(st, l):
        prompt, nb, seq, t, tm, tps, x = (st[k] for k in ("prompt", "nb", "seq", "t", "tm", "tps", "x"))
        st.update(paged=None)
        if l == n_a:
            k, v = _head_proj(x, st["mods_kv"], 0, nb, pack_kv, 0, w_kv16, head_gains, 0,
                              st["tables"], st["tab_nb"], st["tab_seq"], tm, n_heads, n_heads, hd)
            st.update(k=k, v=v)
            if not prompt:
                st.update(means=(jnp.concatenate(means_parts, axis=0) if fuse_means
                                 else _page_means(cache_k, page_table)),
                          k8=pad_rows(k, nb, seq), v8=pad_rows(v, nb, seq))
        if l < n_a:
            if prompt:
                side = None
                first_seq = l * scan_steps // gps if fuse_means else db
                if first_seq < db:
                    side = (cache_k, page_table, first_seq, min(scan_steps // gps, db - first_seq))
                z, tail, h_last, *part = _scan_prompt(x, st["mods"], 2 * l, pack, l, w_in16[l],
                                                      wr_all, wi_all, nb, seq, scan_tt, side)
                means_parts.extend(part)
                st["conv_new"].append(tail[:, SUBLANES - (CONV_WIDTH - 1):])
                st["h_new"].append(h_last[:, 0])
            else:
                ga, xb = _rec_in(x, st["mods"], 2 * l, nb, pack, l, w_in16[l], tm, tps)
                xb_t = xb.reshape(nb, seq, dr).transpose(1, 0, 2)
                ga_t = ga.reshape(nb, seq, dr).transpose(1, 0, 2)
                z_t, h_last = _scan_steps(xb_t, ga_t, conv0_t, state_rglru, pack, wr_all, wi_all, l)
                z = z_t.transpose(1, 0, 2).reshape(t, dr)
                hist = jnp.concatenate([state_conv[l], xb.reshape(nb, seq, dr)], axis=1)
                st["conv_new"].append(hist[:, -(CONV_WIDTH - 1):])
                st["h_new"].append(h_last)
            st.update(z=z, w_mix=w_out16[l])
        else:
            j = l - n_a
            (q,) = _head_proj(x, st["mods"], 2 * l, nb, pack, l, w_q16[j], head_gains, 1 + j,
                              st["tables"], st["tab_nb"], st["tab_seq"], tm, n_heads, 0, hd)
            if prompt:
                st.update(z=_attn_prompt(q, st["k"], st["v"]).reshape(t, n_heads * hd))
            else:
                q8 = pad_rows(q, nb, seq)
                q_bd = jnp.einsum("bhqd,hg->bhqgd", q8, jnp.eye(n_heads, dtype=F32))
                sel = _select(q_bd.reshape(nb, n_heads * SUBLANES, n_heads * hd), st["means"],
                              pages_f)
                sel = sel.reshape(nb, n_heads, SUBLANES, LANES)
                sel = sel[:, :, :seq, :MOBA_TOP_K * PAGES_PER_BLOCK].reshape(nb, -1)
                st.update(z=None, paged=(q8, st["k8"], st["v8"], cache_k, cache_v, sel, seq))
            st.update(w_mix=w_o16[j])

    def mlp(st, l, hosted=None):
        tm = host_tm if hosted is not None else st["tm"]
        out = _mix_mlp(st["x"], st["z"], st["mods"], st["nb"], st["w_mix"], pack,
                       w_up16, w_down16, l, tm, st["seq"] // tm if st["prompt"] else 1, hosted)
        if hosted is None:
            st["x"] = out
            return None
        st["x"] = out[0]
        return out[1]

    grp_p, grp_s = setup(x_prompt, True), setup(x_sample, False)
    for l in range(depth):
        mixer(grp_p, l)
        mixer(grp_s, l)
        paged = grp_s["paged"]
        if paged is not None and host_attn:
            o = mlp(grp_p, l, hosted=paged)
        else:
            mlp(grp_p, l)
            o = _attn_paged(*paged) if paged is not None else None
        if o is not None:
            grp_s["z"] = unpad_rows(o, grp_s["nb"], grp_s["seq"])
        mlp(grp_s, l)

    def finish(st):
        nb, seq = st["nb"], st["seq"]
        k, v = st["k"], st["v"]
        if not st["prompt"]:
            k = k.reshape(n_heads, nb, seq, hd).transpose(1, 0, 2, 3)
            v = v.reshape(n_heads, nb, seq, hd).transpose(1, 0, 2, 3)
        return (st["x"].reshape(nb, seq, d), jnp.stack(st["conv_new"]), jnp.stack(st["h_new"]),
                k, v)

    y_p, conv_p, h_p, k_p, v_p = finish(grp_p)
    y_s, conv_s, h_s, k_s, v_s = finish(grp_s)
    return (y_p, y_s, conv_p, h_p, conv_s, h_s, k_p, v_p, k_s, v_s)
```

```python
import functools
import math

import jax
import jax.numpy as jnp
from jax import lax
from jax.experimental import pallas as pl
from jax.experimental.pallas import tpu as pltpu

F32 = jnp.float32
BF16 = jnp.bfloat16

NORM_EPS = 1e-6
LRU_C = 8.0
CONV_WIDTH = 4
PAGE_SIZE = 128
MOBA_BLOCK = 256
MOBA_TOP_K = 3
PAGES_PER_BLOCK = MOBA_BLOCK // PAGE_SIZE
N_HEADS = 8
ROPE_THETA = 500000.0
NEG = -1e30
LOG2_E = math.log2(math.e)
LANES = 128
SUBLANES = 8
GATE_GROUP = 256
VMEM_LIMIT = 56 * 1024 * 1024


def _params(*sem):
    return pltpu.CompilerParams(dimension_semantics=sem, vmem_limit_bytes=VMEM_LIMIT)


def _resident(shape):
    nd = len(shape)
    return pl.BlockSpec(shape, lambda *_: (0,) * nd, pipeline_mode=pl.Buffered(1))


def _resident_layer(shape, layer):
    nd = len(shape)
    return pl.BlockSpec((1,) + tuple(shape[1:]), lambda *_: (layer,) + (0,) * (nd - 1),
                        pipeline_mode=pl.Buffered(1))


def _dot(a, b):
    return jnp.dot(a, b, preferred_element_type=F32)


def _dot_nt(a, b):
    return lax.dot_general(a, b, (((1,), (1,)), ((), ())), preferred_element_type=F32)


def _split(a):
    hi = a.astype(BF16)
    return hi, (a - hi.astype(F32)).astype(BF16)


def _dot3(a, b, dot=_dot):
    ah, al = _split(a)
    bh, bl = _split(b)
    return dot(ah, bh) + (dot(ah, bl) + dot(al, bh))


def _sigmoid(x):
    return 0.5 + 0.5 * jnp.tanh(0.5 * x)


def _gelu_tanh(x):
    return 0.5 * x * (1.0 + jnp.tanh(math.sqrt(2.0 / math.pi) * (x + 0.044715 * (x * x * x))))


def _softplus(x):
    return jnp.maximum(x, 0.0) + jnp.log1p(jnp.exp(-jnp.abs(x)))


def _modulate(x, g, mod, d):
    ms = jnp.mean(x * x, axis=-1, keepdims=True)
    y = x * lax.rsqrt(ms + NORM_EPS) * g
    return y * (1.0 + mod[:, d:2 * d]) + mod[:, :d]


def _adaln_kernel(c_ref, w_ref, b_ref, o_ref):
    c = c_ref[...]
    a = c * _sigmoid(c)
    r = a.shape[0]
    a_hi = a.astype(BF16).astype(F32)
    w_hi, w_lo = _split(w_ref[0])
    stacked = _dot(jnp.concatenate([a_hi, a - a_hi], axis=0).astype(BF16), w_hi)
    o_ref[0] = stacked[:r] + (stacked[r:] + _dot(a_hi.astype(BF16), w_lo)) + b_ref[0]


def _adaln(c, w, b):
    r, d = c.shape
    m, _, n = w.shape
    tn = 1024
    return pl.pallas_call(
        _adaln_kernel,
        grid=(m, n // tn),
        in_specs=[
            pl.BlockSpec((r, d), lambda i, j: (0, 0)),
            pl.BlockSpec((1, d, tn), lambda i, j: (i, 0, j)),
            pl.BlockSpec((1, 1, tn), lambda i, j: (i, 0, j)),
        ],
        out_specs=pl.BlockSpec((1, r, tn), lambda i, j: (i, 0, j)),
        out_shape=jax.ShapeDtypeStruct((m, r, n), F32),
        compiler_params=_params("parallel", "parallel"),
        name="adaln",
    )(c, w, b)


def _mod_spec(mod, idx, nb, tm, tiles_per_seq):
    _, rows, w = mod.shape
    if rows == 1:
        return pl.BlockSpec((1, 1, w), lambda i, *_: (idx * nb + i // tiles_per_seq, 0, 0))
    assert tm % rows == 0
    return pl.BlockSpec((1, rows, w), lambda i, *_: (idx, 0, 0))


def _mod_rows(mod_ref, tm):
    mod = mod_ref[0]
    if mod.shape[0] in (1, tm):
        return mod
    return jnp.concatenate([mod] * (tm // mod.shape[0]), axis=0)


ROW_G1, ROW_G2, ROW_CONV_B, ROW_LAM, ROW_GATE_B, ROW_CONV_W, PACK_ROWS = 0, 1, 2, 3, 4, 6, 16


def _row(pk_ref, r, n=1):
    return pk_ref[0, r:r + n, :]


def _rec_in_kernel(x_ref, mod_ref, pk_ref, w_ref, ga_ref, xb_ref, *, d, dr):
    h = _modulate(x_ref[...], _row(pk_ref, ROW_G1), _mod_rows(mod_ref, x_ref.shape[0]),
                  d).astype(BF16)
    y = _dot(h, w_ref[...])
    ga_ref[...] = _gelu_tanh(y[:, :dr])
    xb_ref[...] = y[:, dr:]


def _rec_in(x, mod, mod_idx, nb, pk, layer, w, tm, tiles_per_seq):
    t, d = x.shape
    dr = w.shape[1] // 2
    row = pl.BlockSpec((tm, dr), lambda i: (i, 0))
    return pl.pallas_call(
        functools.partial(_rec_in_kernel, d=d, dr=dr),
        grid=(t // tm,),
        in_specs=[
            pl.BlockSpec((tm, d), lambda i: (i, 0)),
            _mod_spec(mod, mod_idx, nb, tm, tiles_per_seq),
            _resident_layer(pk.shape, layer),
            _resident(w.shape),
        ],
        out_specs=[row, row],
        out_shape=[jax.ShapeDtypeStruct((t, dr), F32)] * 2,
        compiler_params=_params("parallel"),
        name="rec_in",
    )(x, mod, pk, w)


def _rope_partner(hd):
    half = hd // 8
    i = jnp.arange(hd)
    return jnp.where(i < half, i + half, jnp.where(i < 2 * half, i - half, i))


def _with_partner_columns(w, n_rope, hd):
    d = w.shape[0]
    wr = w[:, :n_rope * hd].reshape(d, n_rope, hd)[:, :, _rope_partner(hd)]
    return jnp.concatenate([w, wr.reshape(d, n_rope * hd)], axis=1)


def _head_proj_kernel(x_ref, mod_ref, pk_ref, w_ref, gh_ref, ghp_ref, cos_ref, sin_ref, *out_refs,
                      d, hd, n_rope, n_plain):
    h_ref = out_refs[-1]
    h_ref[...] = _modulate(x_ref[...], _row(pk_ref, ROW_G1), _mod_rows(mod_ref, x_ref.shape[0]),
                           d).astype(BF16)
    cg = cos_ref[...] * gh_ref[0]
    sg = sin_ref[...] * ghp_ref[0]
    partner0 = n_rope + n_plain
    roped, plain = [], []
    for i0 in range(0, n_rope + n_plain, 2):
        y = _dot(h_ref[...], w_ref[:, i0 * hd:(i0 + 2) * hd])
        if i0 < n_rope:
            yp = _dot(h_ref[...], w_ref[:, (partner0 + i0) * hd:(partner0 + i0 + 2) * hd])
        for i in (i0, i0 + 1):
            cols = slice((i - i0) * hd, (i - i0 + 1) * hd)
            if i >= n_rope:
                plain.append(y[:, cols])
                continue
            yh = y[:, cols]
            rs = lax.rsqrt(jnp.mean(yh * yh, axis=-1, keepdims=True) + NORM_EPS)
            roped.append(rs * (yh * cg + yp[:, cols] * sg))
    out_refs[0][0] = jnp.stack(roped, axis=0)
    if plain:
        out_refs[1][0] = jnp.stack(plain, axis=0)


def _head_proj(x, mod, mod_idx, mod_nb, pk, layer, w, gains, gain_idx, tables, nb, seq, tm,
               n_rope, n_plain, hd):
    t, d = x.shape
    assert n_rope % 2 == 0 and n_plain % 2 == 0
    tps = seq // tm
    tab = pl.BlockSpec((tm, hd), lambda i: (i % tps, 0))
    out_specs = [pl.BlockSpec((1, n_rope, tm, hd), lambda i: (i // tps, 0, i % tps, 0))]
    out_shape = [jax.ShapeDtypeStruct((nb, n_rope, seq, hd), F32)]
    if n_plain:
        out_specs.append(pl.BlockSpec((1, n_plain, tm, hd), lambda i: (i // tps, 0, i % tps, 0)))
        out_shape.append(jax.ShapeDtypeStruct((nb, n_plain, seq, hd), F32))
    return pl.pallas_call(
        functools.partial(_head_proj_kernel, d=d, hd=hd, n_rope=n_rope, n_plain=n_plain),
        grid=(t // tm,),
        in_specs=[
            pl.BlockSpec((tm, d), lambda i: (i, 0)),
            _mod_spec(mod, mod_idx, mod_nb, tm, tps),
            _resident_layer(pk.shape, layer),
            _resident(w.shape),
            _resident_layer(gains[0].shape, gain_idx),
            _resident_layer(gains[1].shape, gain_idx),
            tab, tab,
        ],
        out_specs=out_specs,
        out_shape=out_shape,
        scratch_shapes=[pltpu.VMEM((tm, d), BF16)],
        compiler_params=_params("parallel"),
        name="head_proj",
    )(x, mod, pk, w, *gains, *tables)


FF_CHUNK = 1024


PAGES_PER_MEANS_STEP = 16


def _page_block_means(k_refs, o_ref):
    n = len(k_refs)
    n_heads, _, hd = k_refs[0].shape[1:]
    for h in range(n_heads):
        sums = [jnp.sum(k_refs[j][0, h], axis=0, keepdims=True) for j in range(n)]
        o_ref[0, :, h * hd:(h + 1) * hd] = jnp.concatenate(
            [(sums[2 * i] + sums[2 * i + 1]) * (1.0 / MOBA_BLOCK) for i in range(n // 2)], axis=0)


def _mix_mlp_kernel(*refs, d, attn):
    if attn:
        sel_ref, refs = refs[0], refs[1:]
        _paged_heads_step(sel_ref, *refs[8:13], refs[14], *refs[18:21], **attn)
        refs = refs[:8] + (refs[13],) + refs[15:18]
    x_ref, z_ref, mod1_ref, wo_ref, mod2_ref, pk_ref, wu_ref, wd_ref, o_ref = refs[:9]
    x1_ref, h_ref, acc_ref = refs[-3:]
    tm = x_ref.shape[0]
    x1 = x_ref[...] + _mod_rows(mod1_ref, tm)[:, 2 * d:] * _dot(z_ref[...].astype(BF16),
                                                                wo_ref[...])
    x1_ref[...] = x1
    mod2 = _mod_rows(mod2_ref, tm)
    h_ref[...] = _modulate(x1, _row(pk_ref, ROW_G2), mod2, d).astype(BF16)
    for c in range(wu_ref.shape[2] // FF_CHUNK):
        cols = slice(c * FF_CHUNK, (c + 1) * FF_CHUNK)
        u = _dot(h_ref[...], wu_ref[0, :, cols])
        a = jnp.square(jnp.maximum(u, 0.0)).astype(BF16)
        part = _dot(a, wd_ref[0, cols, :])
        if c == 0:
            acc_ref[...] = part
        else:
            acc_ref[...] += part
    o_ref[...] = x1_ref[...] + mod2[:, 2 * d:] * acc_ref[...]


def _mix_mlp(x, z, mods, mod_nb, wo, pk, wu, wd, layer, tm, tiles_per_seq, hosted=None):
    t, d = x.shape
    assert wu.shape[2] % FF_CHUNK == 0
    steps = t // tm
    row = pl.BlockSpec((tm, d), lambda i, *_: (i, 0))
    in_specs = [
        row,
        pl.BlockSpec((tm, z.shape[1]), lambda i, *_: (i, 0)),
        _mod_spec(mods, 2 * layer, mod_nb, tm, tiles_per_seq),
        _resident(wo.shape),
        _mod_spec(mods, 2 * layer + 1, mod_nb, tm, tiles_per_seq),
        _resident_layer(pk.shape, layer),
        _resident_layer(wu.shape, layer),
        _resident_layer(wd.shape, layer),
    ]
    scratch = [pltpu.VMEM((tm, d), F32), pltpu.VMEM((tm, d), BF16), pltpu.VMEM((tm, d), F32)]
    out_specs, out_shape = row, jax.ShapeDtypeStruct((t, d), F32)
    args = (x, z, mods, wo, mods, pk, wu, wd)
    if hosted is None:
        return pl.pallas_call(
            functools.partial(_mix_mlp_kernel, d=d, attn=None),
            grid=(steps,), in_specs=in_specs, out_specs=out_specs, out_shape=out_shape,
            scratch_shapes=scratch, compiler_params=_params("parallel"), name="mix_mlp",
        )(*args)
    q, k_new, v_new, cache_k, cache_v, sel, sq = hosted
    db, h, rows, hd = q.shape
    assert steps % db == 0 and h % (steps // db) == 0
    spq = steps // db
    hps = h // spq
    n_slots = sq * MOBA_TOP_K * PAGES_PER_BLOCK
    heads = pl.BlockSpec((1, hps, rows, hd), lambda i, sel: (i // spq, i % spq, 0, 0))
    hbm = pl.BlockSpec(memory_space=pl.ANY)
    return pl.pallas_call(
        functools.partial(_mix_mlp_kernel, d=d,
                          attn=dict(n_slots=n_slots, sq=sq, scale=hd ** -0.5, n_heads=h)),
        grid_spec=pltpu.PrefetchScalarGridSpec(
            num_scalar_prefetch=1,
            grid=(steps,),
            in_specs=in_specs + [heads, heads, heads, hbm, hbm],
            out_specs=[out_specs, heads],
            scratch_shapes=scratch + [pltpu.VMEM((hps, n_slots, PAGE_SIZE, hd), F32),
                                      pltpu.VMEM((hps, n_slots, PAGE_SIZE, hd), F32),
                                      pltpu.SemaphoreType.DMA((2,))],
        ),
        out_shape=[out_shape, jax.ShapeDtypeStruct((db, h, rows, hd), F32)],
        compiler_params=_params("arbitrary"),
        name="mix_mlp_attn",
    )(_page_rows(sel, h, n_slots), *args, q, k_new, v_new,
      cache_k.reshape(-1, PAGE_SIZE, hd), cache_v.reshape(-1, PAGE_SIZE, hd))


def _lru_inputs(xc, wr_ref, wi_ref, pk_ref):
    xh = xc.astype(BF16)
    gw = wr_ref.shape[2]
    r = jnp.concatenate([_dot(xh[:, j * gw:(j + 1) * gw], wr_ref[0, j])
                         for j in range(wr_ref.shape[1])], axis=1) + _row(pk_ref, ROW_GATE_B)
    i = jnp.concatenate([_dot(xh[:, j * gw:(j + 1) * gw], wi_ref[0, j])
                         for j in range(wi_ref.shape[1])], axis=1) + _row(pk_ref, ROW_GATE_B + 1)
    log_a = -LRU_C * _sigmoid(r) * _softplus(-_row(pk_ref, ROW_LAM))
    t = jnp.tanh(0.5 * log_a)
    rc = 1.0 / (1.0 - t)
    a = (1.0 + t) * rc
    u = (2.0 * rc) * jnp.sqrt(-t) * (_sigmoid(i) * xc)
    return a, u


def _scan_prompt_kernel(*refs, tt, n_side):
    if n_side:
        refs = refs[1:]
        _page_block_means(refs[6:6 + n_side], refs[9 + n_side])
    x_ref, mod_ref, pk_ref, win_ref, wr_ref, wi_ref = refs[:6]
    z_ref, tail_out_ref, hl_ref = refs[6 + n_side:9 + n_side]
    tail_ref, hc_ref, a_ref, u_ref, ga_ref = refs[-5:]
    ti = pl.program_id(1)
    d, dr = x_ref.shape[1], z_ref.shape[1]

    @pl.when(ti == 0)
    def _():
        tail_ref[...] = jnp.zeros(tail_ref.shape, F32)
        hc_ref[...] = jnp.zeros(hc_ref.shape, F32)

    y = _dot(_modulate(x_ref[...], _row(pk_ref, ROW_G1), mod_ref[0], d).astype(BF16),
             win_ref[...])
    ga_ref[...] = _gelu_tanh(y[:, :dr])
    x = y[:, dr:]
    tail = tail_ref[...]
    row8 = lax.broadcasted_iota(jnp.int32, (SUBLANES, dr), 0)
    xc = _row(pk_ref, ROW_CONV_B) + _row(pk_ref, ROW_CONV_W + CONV_WIDTH - 1) * x
    for k in range(1, CONV_WIDTH):
        xs = pltpu.roll(x, k, 0)
        first = jnp.where(row8 < k, pltpu.roll(tail, k, 0), xs[:SUBLANES])
        xs = jnp.concatenate([first, xs[SUBLANES:]], axis=0)
        xc = xc + _row(pk_ref, ROW_CONV_W + CONV_WIDTH - 1 - k) * xs
    tail_ref[...] = x[tt - SUBLANES:]
    tail_out_ref[0] = x[tt - SUBLANES:]

    a, u = _lru_inputs(xc, wr_ref, wi_ref, pk_ref)
    a_ref[...] = a
    u_ref[...] = u

    def group(gi, carry):
        r0 = pl.multiple_of(gi * SUBLANES, SUBLANES)
        a8 = a_ref[pl.ds(r0, SUBLANES), :]
        u8 = u_ref[pl.ds(r0, SUBLANES), :]
        for s in (1, 2, 4):
            keep = row8 >= s
            u8 = jnp.where(keep, a8 * pltpu.roll(u8, s, 0) + u8, u8)
            a8 = jnp.where(keep, a8 * pltpu.roll(a8, s, 0), a8)
        h8 = a8 * carry + u8
        z_ref[pl.ds(r0, SUBLANES), :] = h8 * ga_ref[pl.ds(r0, SUBLANES), :]
        return jnp.broadcast_to(h8[SUBLANES - 1:SUBLANES, :], (SUBLANES, dr))

    carry = lax.fori_loop(0, tt // SUBLANES, group,
                          jnp.broadcast_to(hc_ref[...], (SUBLANES, dr)), unroll=4)
    hc_ref[...] = carry[0:1, :]
    hl_ref[0] = carry[0:1, :]


def _scan_prompt(x, mods, mod_idx, pk, layer, w_in, wr, wi, nb, seq, tt, side=None):
    t, d = x.shape
    dr = w_in.shape[1] // 2
    nt = seq // tt
    row = pl.BlockSpec((tt, dr), lambda b, i, *_: (b * nt + i, 0))
    per_seq8 = pl.BlockSpec((1, SUBLANES, dr), lambda b, i, *_: (b, 0, 0))
    per_seq1 = pl.BlockSpec((1, 1, dr), lambda b, i, *_: (b, 0, 0))
    in_specs = [pl.BlockSpec((tt, d), lambda b, i, *_: (b * nt + i, 0)),
                pl.BlockSpec((1, 1, mods.shape[2]), lambda b, i, *_: (mod_idx * nb + b, 0, 0)),
                _resident_layer(pk.shape, layer), _resident(w_in.shape),
                _resident_layer(wr.shape, layer), _resident_layer(wi.shape, layer)]
    out_specs = [row, per_seq8, per_seq1]
    out_shape = [jax.ShapeDtypeStruct((t, dr), F32),
                 jax.ShapeDtypeStruct((nb, SUBLANES, dr), F32),
                 jax.ShapeDtypeStruct((nb, 1, dr), F32)]
    scratch = [pltpu.VMEM((SUBLANES, dr), F32), pltpu.VMEM((1, dr), F32),
               pltpu.VMEM((tt, dr), F32), pltpu.VMEM((tt, dr), F32), pltpu.VMEM((tt, dr), F32)]
    args = (x, mods, pk, w_in, wr, wi)
    if side is None:
        return pl.pallas_call(
            functools.partial(_scan_prompt_kernel, tt=tt, n_side=0),
            grid=(nb, nt), in_specs=in_specs, out_specs=out_specs, out_shape=out_shape,
            scratch_shapes=scratch, compiler_params=_params("parallel", "arbitrary"),
            name="scan_prompt",
        )(*args)
    cache_k, page_table, first_seq, n_seqs = side
    n = PAGES_PER_MEANS_STEP
    n_pages = page_table.shape[1]
    _, h, _, hd = cache_k.shape
    gps = n_pages // n
    last = n_seqs * gps - 1

    def group(b, i):
        return jnp.minimum(b * nt + i, last)

    def page(j):
        return pl.BlockSpec(
            (1, h, PAGE_SIZE, hd),
            lambda b, i, pt: (pt[first_seq + group(b, i) // gps, (group(b, i) % gps) * n + j],
                              0, 0, 0))

    return pl.pallas_call(
        functools.partial(_scan_prompt_kernel, tt=tt, n_side=n),
        grid_spec=pltpu.PrefetchScalarGridSpec(
            num_scalar_prefetch=1,
            grid=(nb, nt),
            in_specs=in_specs + [page(j) for j in range(n)],
            out_specs=out_specs + [pl.BlockSpec(
                (1, n // PAGES_PER_BLOCK, h * hd),
                lambda b, i, pt: (group(b, i) // gps, group(b, i) % gps, 0))],
            scratch_shapes=scratch,
        ),
        out_shape=out_shape + [
            jax.ShapeDtypeStruct((n_seqs, n_pages // PAGES_PER_BLOCK, h * hd), F32)],
        compiler_params=_params("arbitrary", "arbitrary"),
        name="scan_prompt_means",
    )(page_table, *args, *([cache_k] * n))


def _scan_steps_kernel(xb_ref, ga_ref, st_ref, h0_ref, pk_ref, wr_ref, wi_ref, z_ref, hl_ref,
                       *, steps):
    rows = [st_ref[0, j] for j in range(CONV_WIDTH - 1)] + [xb_ref[t] for t in range(steps)]
    h = h0_ref[0]
    for t in range(steps):
        xc = _row(pk_ref, ROW_CONV_B)
        for j in range(CONV_WIDTH):
            xc = xc + _row(pk_ref, ROW_CONV_W + j) * rows[t + j]
        a, u = _lru_inputs(xc, wr_ref, wi_ref, pk_ref)
        h = a * h + u
        z_ref[t] = h * ga_ref[t]
    hl_ref[...] = h


def _scan_steps(xb, ga, st, h0, pk, wr, wi, layer):
    steps, nb, dr = xb.shape
    args = (xb, ga, st, h0, pk, wr, wi)
    return pl.pallas_call(
        functools.partial(_scan_steps_kernel, steps=steps),
        grid=(1,),
        in_specs=[_resident(xb.shape), _resident(ga.shape)]
        + [_resident_layer(a.shape, layer) for a in args[2:]],
        out_specs=[pl.BlockSpec((steps, nb, dr), lambda i: (0, 0, 0)),
                   pl.BlockSpec((nb, dr), lambda i: (0, 0))],
        out_shape=[jax.ShapeDtypeStruct((steps, nb, dr), F32),
                   jax.ShapeDtypeStruct((nb, dr), F32)],
        compiler_params=_params("arbitrary"),
        name="scan_steps",
    )(*args)


ATTN_KEY_CHUNK = 128


def _block_bias_t(gate_t, cur):
    row = lax.broadcasted_iota(jnp.int32, gate_t.shape, 0)
    row_f = row.astype(F32)
    g = jnp.where(row < cur, gate_t, NEG)
    bias = jnp.where(row == cur, 0.0, NEG)
    for _ in range(min(MOBA_TOP_K, cur)):
        m = jnp.max(g, axis=0, keepdims=True)
        idx = jnp.min(jnp.where(g == m, row_f, float(LANES)), axis=0, keepdims=True)
        pick = row_f == idx
        bias = jnp.where(pick, 0.0, bias)
        g = jnp.where(pick, NEG, g)
    return bias


def _attn_prompt_kernel(q_ref, k_ref, v_ref, o_ref, kaug_ref, vt_ref, means_ref, *, seq, hd, scale):
    bq = MOBA_BLOCK
    nblk = seq // bq
    nbp = means_ref.shape[0]

    lane = lax.broadcasted_iota(jnp.int32, (bq, hd), 1)
    means_ref[...] = jnp.zeros(means_ref.shape, F32)
    for n in range(nblk):
        kb = k_ref[0, 0, n * bq:(n + 1) * bq, :]
        means_ref[n:n + 1, :] = jnp.sum(kb, axis=0, keepdims=True) * (1.0 / MOBA_BLOCK)
        onehot = jnp.where(lane == n, 1.0, 0.0).astype(BF16)
        kaug_ref[n * bq:(n + 1) * bq, :] = jnp.concatenate([kb.astype(BF16), onehot], axis=1)
        vt_ref[:, n * bq:(n + 1) * bq] = v_ref[0, 0, n * bq:(n + 1) * bq, :].T.astype(BF16)

    kc = ATTN_KEY_CHUNK
    key_idx = lax.broadcasted_iota(jnp.int32, (kc, bq), 0)
    query_idx = lax.broadcasted_iota(jnp.int32, (kc, bq), 1)

    outs = []
    for c in range(nblk):
        qt = q_ref[0, 0, c * bq:(c + 1) * bq, :].T
        gate_t = _dot3(means_ref[...], qt)
        qt_aug = jnp.concatenate(
            [qt * (scale * LOG2_E), _block_bias_t(gate_t, c), jnp.zeros((hd - nbp, bq), F32)],
            axis=0).astype(BF16)
        m = l = acc = None
        own = list(range(c * bq // kc, (c + 1) * bq // kc))
        for j in own + list(range(c * bq // kc)):
            s = _dot(kaug_ref[j * kc:(j + 1) * kc, :], qt_aug)
            if j in own:
                s = jnp.where(key_idx + (j * kc - c * bq) <= query_idx, s, NEG)
            mn = jnp.max(s, axis=0, keepdims=True)
            m_new = mn if m is None else jnp.maximum(m, mn)
            p = jnp.exp2(s - m_new)
            ln = jnp.sum(p, axis=0, keepdims=True)
            on = _dot(vt_ref[:, j * kc:(j + 1) * kc], p.astype(BF16))
            if m is None:
                l, acc = ln, on
            else:
                wa = jnp.exp2(m - m_new)
                l, acc = l * wa + ln, acc * wa + on
            m = m_new
        outs.append((acc * (1.0 / l)).T)
    o_ref[0] = jnp.concatenate(outs, axis=0)


def _attn_prompt(q, k, v):
    b, h, seq, hd = q.shape
    nblk = seq // MOBA_BLOCK
    assert seq % MOBA_BLOCK == 0 and nblk <= hd and hd == LANES
    nbp = -(-nblk // SUBLANES) * SUBLANES
    full = pl.BlockSpec((1, 1, seq, hd), lambda bi, hi: (bi, hi, 0, 0))
    return pl.pallas_call(
        functools.partial(_attn_prompt_kernel, seq=seq, hd=hd, scale=hd ** -0.5),
        grid=(b, h),
        in_specs=[full, full, full],
        out_specs=pl.BlockSpec((1, seq, hd), lambda bi, hi: (bi, 0, hi)),
        out_shape=jax.ShapeDtypeStruct((b, seq, h * hd), F32),
        scratch_shapes=[pltpu.VMEM((seq, 2 * hd), BF16), pltpu.VMEM((hd, seq), BF16),
                        pltpu.VMEM((nbp, hd), F32)],
        compiler_params=_params("parallel", "parallel"),
        name="attn_prompt",
    )(q, k, v)


def _page_means_kernel(pt_ref, *refs, n):
    _page_block_means(refs[:n], refs[n])


def _page_means(cache_k, page_table):
    db, n_pages = page_table.shape
    _, h, _, hd = cache_k.shape
    n = min(PAGES_PER_MEANS_STEP, n_pages)

    def page(j):
        return pl.BlockSpec((1, h, PAGE_SIZE, hd), lambda b, g, pt: (pt[b, g * n + j], 0, 0, 0))

    return pl.pallas_call(
        functools.partial(_page_means_kernel, n=n),
        grid_spec=pltpu.PrefetchScalarGridSpec(
            num_scalar_prefetch=1,
            grid=(db, n_pages // n),
            in_specs=[page(j) for j in range(n)],
            out_specs=pl.BlockSpec((1, n // 2, h * hd), lambda b, g, pt: (b, g, 0)),
        ),
        out_shape=jax.ShapeDtypeStruct((db, n_pages // 2, h * hd), F32),
        compiler_params=_params("parallel", "arbitrary"),
        name="page_means",
    )(page_table, *([cache_k] * n))


def _select_kernel(q_ref, means_ref, pt_ref, sel_ref):
    n_blocks = means_ref.shape[1]
    outs = []
    for i in range(q_ref.shape[0]):
        means = jnp.concatenate(
            [means_ref[i], jnp.zeros((LANES - n_blocks, means_ref.shape[2]), F32)], axis=0)
        gate = _dot3(q_ref[i], means, _dot_nt)
        lane = lax.broadcasted_iota(jnp.int32, gate.shape, 1)
        lane_f = lane.astype(F32)
        pages = pt_ref[i]
        g = jnp.where(lane < n_blocks, gate, NEG)
        out = jnp.zeros(gate.shape, F32)
        for j in range(MOBA_TOP_K):
            m = jnp.max(g, axis=1, keepdims=True)
            idx = jnp.min(jnp.where(g == m, lane_f, float(LANES)), axis=1, keepdims=True)
            for p in range(PAGES_PER_BLOCK):
                phys = jnp.sum(jnp.where(lane_f == idx * PAGES_PER_BLOCK + p, pages, 0.0),
                               axis=1, keepdims=True)
                out = jnp.where(lane == j * PAGES_PER_BLOCK + p, phys, out)
            g = jnp.where(lane_f == idx, NEG, g)
        outs.append(out.astype(jnp.int32))
    sel_ref[...] = jnp.stack(outs, axis=0)


def _select(q_bd, means, pages_f):
    db, r, w = q_bd.shape
    nb = means.shape[1]
    assert MOBA_TOP_K <= nb and nb * PAGES_PER_BLOCK <= LANES
    g = math.gcd(db, 8)
    return pl.pallas_call(
        _select_kernel,
        grid=(db // g,),
        in_specs=[pl.BlockSpec((g, r, w), lambda b: (b, 0, 0)),
                  pl.BlockSpec((g, nb, w), lambda b: (b, 0, 0)),
                  pl.BlockSpec((g, 1, LANES), lambda b: (b, 0, 0))],
        out_specs=pl.BlockSpec((g, r, LANES), lambda b: (b, 0, 0)),
        out_shape=jax.ShapeDtypeStruct((db, r, LANES), jnp.int32),
        compiler_params=_params("parallel"),
        name="select_blocks",
    )(q_bd, means, pages_f)


def _attn_paged_kernel(sel_ref, q_ref, kn_ref, vn_ref, ck_ref, cv_ref, o_ref, kbuf, vbuf, sems,
                       *, n_slots, sq, scale):
    b = pl.program_id(0)
    n_heads = q_ref.shape[1]
    assert n_heads % 2 == 0

    def page_copies(seq, h, page_of_slot):
        out = []
        for s in range(n_slots):
            row = page_of_slot(seq, h * n_slots + s)
            out.append(pltpu.make_async_copy(ck_ref.at[row], kbuf.at[h % 2, s], sems.at[0, h % 2]))
            out.append(pltpu.make_async_copy(cv_ref.at[row], vbuf.at[h % 2, s], sems.at[1, h % 2]))
        return out

    def start_all(copies):
        for i, c in enumerate(copies):
            c.start(priority=i % 2)

    @pl.when(b == 0)
    def _():
        start_all(page_copies(b, 0, lambda seq, i: sel_ref[seq, i]))

    for h in range(n_heads):
        if h + 1 < n_heads:
            start_all(page_copies(b, h + 1, lambda seq, i: sel_ref[seq, i]))
        else:
            @pl.when(b + 1 < pl.num_programs(0))
            def _():
                start_all(page_copies(b + 1, 0, lambda seq, i: sel_ref[seq, i]))
        for c in page_copies(0, h, lambda seq, i: 0):
            c.wait()
        o_ref[0, h] = _paged_head(q_ref[0, h], kn_ref[0, h], vn_ref[0, h], kbuf[h % 2],
                                  vbuf[h % 2], sq, scale)


def _page_rows(sel, n_heads, n_slots):
    head_of_slot = jnp.repeat(jnp.arange(n_heads, dtype=jnp.int32), n_slots)
    return sel * n_heads + head_of_slot[None, :]


def _paged_heads_step(sel_ref, q_ref, kn_ref, vn_ref, ck_ref, cv_ref, o_ref, kbuf, vbuf, sems,
                      *, n_slots, sq, scale, n_heads):
    i = pl.program_id(0)
    hps = q_ref.shape[1]
    spq = n_heads // hps

    def page_copies(step, page_of):
        seq, h0 = step // spq, (step % spq) * hps
        out = []
        for j in range(hps):
            for s in range(n_slots):
                row = page_of(seq, (h0 + j) * n_slots + s)
                out.append(pltpu.make_async_copy(ck_ref.at[row], kbuf.at[j, s], sems.at[0]))
                out.append(pltpu.make_async_copy(cv_ref.at[row], vbuf.at[j, s], sems.at[1]))
        return out

    def start_all(copies):
        for n, c in enumerate(copies):
            c.start(priority=n % 2)

    @pl.when(i == 0)
    def _():
        start_all(page_copies(i, lambda seq, n: sel_ref[seq, n]))

    for c in page_copies(0, lambda seq, n: 0):
        c.wait()
    for j in range(hps):
        o_ref[0, j] = _paged_head(q_ref[0, j], kn_ref[0, j], vn_ref[0, j], kbuf[j], vbuf[j],
                                  sq, scale)

    @pl.when(i + 1 < pl.num_programs(0))
    def _():
        start_all(page_copies(i + 1, lambda seq, n: sel_ref[seq, n]))


def _paged_head(q, kn, vn, k_pages, v_pages, sq, scale):
    n_slots, _, hd = k_pages.shape
    k_all = k_pages.reshape(n_slots * PAGE_SIZE, hd).astype(BF16)
    v_all = v_pages.reshape(n_slots * PAGE_SIZE, hd).astype(BF16)
    s = _dot_nt((q * scale).astype(BF16), k_all)
    keys_per_q = MOBA_TOP_K * MOBA_BLOCK
    row = lax.broadcasted_iota(jnp.int32, s.shape, 0)
    col = lax.broadcasted_iota(jnp.int32, s.shape, 1)
    s = jnp.where((col >= row * keys_per_q) & (col < (row + 1) * keys_per_q), s, NEG)
    row1 = lax.broadcasted_iota(jnp.int32, (q.shape[0], 1), 0)
    s_new = [jnp.where(row1 >= j, jnp.sum(q * kn[j:j + 1, :], axis=1, keepdims=True) * scale, NEG)
             for j in range(sq)]
    m = functools.reduce(jnp.maximum, s_new + [jnp.max(s, axis=1, keepdims=True)])
    p = jnp.exp(s - m)
    p_new = [jnp.exp(sj - m) for sj in s_new]
    l = functools.reduce(lambda x, y: x + y, p_new + [jnp.sum(p, axis=1, keepdims=True)])
    acc = functools.reduce(lambda x, y: x + y,
                           [p_new[j] * vn[j:j + 1, :] for j in range(sq)]
                           + [_dot(p.astype(BF16), v_all)])
    return acc / l


def _attn_paged(q, k_new, v_new, cache_k, cache_v, sel, sq):
    db, h, rows, hd = q.shape
    n_slots = sq * MOBA_TOP_K * PAGES_PER_BLOCK
    per_seq = pl.BlockSpec((1, h, rows, hd), lambda b, sel: (b, 0, 0, 0))
    hbm = pl.BlockSpec(memory_space=pl.ANY)
    sel_rows = _page_rows(sel, h, n_slots)
    return pl.pallas_call(
        functools.partial(_attn_paged_kernel, n_slots=n_slots, sq=sq, scale=hd ** -0.5),
        grid_spec=pltpu.PrefetchScalarGridSpec(
            num_scalar_prefetch=1,
            grid=(db,),
            in_specs=[per_seq, per_seq, per_seq, hbm, hbm],
            out_specs=per_seq,
            scratch_shapes=[pltpu.VMEM((2, n_slots, PAGE_SIZE, hd), F32),
                            pltpu.VMEM((2, n_slots, PAGE_SIZE, hd), F32),
                            pltpu.SemaphoreType.DMA((2, 2))],
        ),
        out_shape=jax.ShapeDtypeStruct((db, h, rows, hd), F32),
        compiler_params=_params("arbitrary"),
        name="attn_paged",
    )(sel_rows, q, k_new, v_new, cache_k.reshape(-1, PAGE_SIZE, hd),
      cache_v.reshape(-1, PAGE_SIZE, hd))


def _rope_tables(pos, hd):
    half = hd // 8
    inv = ROPE_THETA ** (-jnp.arange(half, dtype=F32) * 2.0 / (2 * half))
    ang = pos.astype(F32)[:, None] * inv[None, :]
    cos, sin = jnp.cos(ang), jnp.sin(ang)
    n = pos.shape[0]
    ones = jnp.ones((n, hd - 2 * half), F32)
    zeros = jnp.zeros((n, hd - 2 * half), F32)
    return (jnp.concatenate([cos, cos, ones], axis=1),
            jnp.concatenate([-sin, sin, zeros], axis=1))


def _block_diag_gate(w_gate):
    n, k, _ = w_gate.shape
    per = GATE_GROUP // k
    eye = jnp.eye(per, dtype=w_gate.dtype)

    def build(w):
        w = w.reshape(n // per, per, k, k)
        return jnp.einsum("gmkj,mn->gmknj", w, eye).reshape(n // per, GATE_GROUP, GATE_GROUP).astype(BF16)

    return build(w_gate[:, :, :k]), build(w_gate[:, :, k:])


def kernel(x_prompt, x_sample, c_prompt, c_sample, state_conv, state_rglru, cache_k, cache_v, page_table, g_norm, w_ada, b_ada, w_rec_in, conv_w, conv_b, w_gate, b_gate, lam, w_rec_out, g_kv, w_ada_kv, b_ada_kv, w_kv, g_k, w_q, g_q, w_o, w_up, w_down):
    bp, sp, d = x_prompt.shape
    db, sq, _ = x_sample.shape
    depth = g_norm.shape[0]
    n_a = w_rec_in.shape[0]
    dr = w_rec_in.shape[2] // 2
    hd = g_k.shape[0]
    n_heads = w_q.shape[2] // hd
    n_pages = page_table.shape[1]
    d_ff = w_up.shape[2]
    assert hd == LANES and n_heads == N_HEADS and sq <= SUBLANES and n_pages % PAGES_PER_BLOCK == 0

    c_all = jnp.concatenate([c_prompt, c_sample], axis=0)
    pad = (-c_all.shape[0]) % SUBLANES
    c_all = jnp.pad(c_all, ((0, pad), (0, 0)))
    mods = _adaln(c_all, w_ada.reshape(depth * 2, d, 3 * d), b_ada.reshape(depth * 2, 1, 3 * d))
    mods_kv = _adaln(c_all, w_ada_kv[None], b_ada_kv[None, None])

    def group_mods(m, prompt):
        if prompt:
            return m[:, :bp].reshape(m.shape[0] * bp, 1, m.shape[2])
        return m[:, bp:bp + db]

    assert d == dr
    n_b = depth - n_a
    pack = jnp.concatenate(
        [g_norm,
         jnp.pad(jnp.concatenate([conv_b[:, None], lam[:, None], b_gate, conv_w], axis=1),
                 ((0, n_b), (0, 0), (0, 0))),
         jnp.zeros((depth, PACK_ROWS - ROW_CONV_W - CONV_WIDTH, d), F32)], axis=1)
    pack_kv = jnp.pad(g_kv[None, None], ((0, 0), (0, PACK_ROWS - 1), (0, 0)))
    head_gains = jnp.concatenate([g_k[None], g_q], axis=0)[:, None]
    head_gains = (head_gains, head_gains[:, :, _rope_partner(hd)])

    w_in16 = [w_rec_in[l].astype(BF16) for l in range(n_a)]
    w_out16 = [w_rec_out[l].astype(BF16) for l in range(n_a)]
    w_kv16 = _with_partner_columns(w_kv.astype(BF16), n_heads, hd)
    w_q16 = [_with_partner_columns(w_q[j].astype(BF16), n_heads, hd) for j in range(depth - n_a)]
    w_o16 = [w_o[j].astype(BF16) for j in range(depth - n_a)]
    w_up16, w_down16 = w_up.astype(BF16), w_down.astype(BF16)
    gates = [_block_diag_gate(w_gate[l]) for l in range(n_a)]
    wr_all, wi_all = (jnp.stack([g[i] for g in gates]) for i in range(2))
    conv0_t = state_conv.transpose(0, 2, 1, 3)
    pages_f = jnp.pad(page_table.astype(F32), ((0, 0), (0, LANES - n_pages)))[:, None, :]

    prompt_tm, scan_tt, host_tm = 512, 256, 256
    scan_steps = bp * (sp // scan_tt)
    gps = n_pages // PAGES_PER_MEANS_STEP
    fuse_means = (n_pages % PAGES_PER_MEANS_STEP == 0 and gps > 0 and scan_steps % gps == 0
                  and n_a * scan_steps >= db * gps)
    host_steps = bp * sp // host_tm
    host_attn = (sp % host_tm == 0 and host_steps % db == 0
                 and n_heads % (host_steps // db) == 0)
    means_parts = []

    def setup(x3, prompt):
        nb, seq, _ = x3.shape
        t = nb * seq
        tm = prompt_tm if prompt else t
        x = x3.reshape(t, d) if prompt else x3.transpose(1, 0, 2).reshape(t, d)
        st = dict(prompt=prompt, nb=nb, seq=seq, t=t, x=x, tm=tm,
                  tps=seq // tm if prompt else 1, conv_new=[], h_new=[],
                  mods=group_mods(mods, prompt), mods_kv=group_mods(mods_kv, prompt))
        if prompt:
            pos = jnp.arange(seq, dtype=jnp.int32)
            st.update(tables=_rope_tables(pos, hd), tab_nb=nb, tab_seq=seq)
        else:
            pos = n_pages * PAGE_SIZE + jnp.arange(seq, dtype=jnp.int32)
            st.update(tables=tuple(jnp.repeat(tb, nb, axis=0) for tb in _rope_tables(pos, hd)),
                      tab_nb=1, tab_seq=t)
        return st

    def pad_rows(a, nb, seq):
        a = a.reshape(n_heads, seq, nb, hd).transpose(2, 0, 1, 3)
        return jnp.pad(a, ((0, 0), (0, 0), (0, SUBLANES - seq), (0, 0)))

    def unpad_rows(o, nb, seq):
        return o[:, :, :seq].transpose(2, 0, 1, 3).reshape(nb * seq, n_heads * hd)

    def mixer(st, l):
        prompt, nb, seq, t, tm, tps, x = (st[k] for k in ("prompt", "nb", "seq", "t", "tm", "tps", "x"))
        st.update(paged=None)
        if l == n_a:
            k, v = _head_proj(x, st["mods_kv"], 0, nb, pack_kv, 0, w_kv16, head_gains, 0,
                              st["tables"], st["tab_nb"], st["tab_seq"], tm, n_heads, n_heads, hd)
            st.update(k=k, v=v)
            if not prompt:
                st.update(means=(jnp.concatenate(means_parts, axis=0) if fuse_means
                                 else _page_means(cache_k, page_table)),
                          k8=pad_rows(k, nb, seq), v8=pad_rows(v, nb, seq))
        if l < n_a:
            if prompt:
                side = None
                first_seq = l * scan_steps // gps if fuse_means else db
                if first_seq < db:
                    side = (cache_k, page_table, first_seq, min(scan_steps // gps, db - first_seq))
                z, tail, h_last, *part = _scan_prompt(x, st["mods"], 2 * l, pack, l, w_in16[l],
                                                      wr_all, wi_all, nb, seq, scan_tt, side)
                means_parts.extend(part)
                st["conv_new"].append(tail[:, SUBLANES - (CONV_WIDTH - 1):])
                st["h_new"].append(h_last[:, 0])
            else:
                ga, xb = _rec_in(x, st["mods"], 2 * l, nb, pack, l, w_in16[l], tm, tps)
                xb_t = xb.reshape(seq, nb, dr)
                z_t, h_last = _scan_steps(xb_t, ga.reshape(seq, nb, dr), conv0_t, state_rglru,
                                          pack, wr_all, wi_all, l)
                z = z_t.reshape(t, dr)
                hist = jnp.concatenate([state_conv[l], xb_t.transpose(1, 0, 2)], axis=1)
                st["conv_new"].append(hist[:, -(CONV_WIDTH - 1):])
                st["h_new"].append(h_last)
            st.update(z=z, w_mix=w_out16[l])
        else:
            j = l - n_a
            (q,) = _head_proj(x, st["mods"], 2 * l, nb, pack, l, w_q16[j], head_gains, 1 + j,
                              st["tables"], st["tab_nb"], st["tab_seq"], tm, n_heads, 0, hd)
            if prompt:
                st.update(z=_attn_prompt(q, st["k"], st["v"]).reshape(t, n_heads * hd))
            else:
                q8 = pad_rows(q, nb, seq)
                q_bd = jnp.einsum("bhqd,hg->bhqgd", q8, jnp.eye(n_heads, dtype=F32))
                sel = _select(q_bd.reshape(nb, n_heads * SUBLANES, n_heads * hd), st["means"],
                              pages_f)
                sel = sel.reshape(nb, n_heads, SUBLANES, LANES)
                sel = sel[:, :, :seq, :MOBA_TOP_K * PAGES_PER_BLOCK].reshape(nb, -1)
                st.update(z=None, paged=(q8, st["k8"], st["v8"], cache_k, cache_v, sel, seq))
            st.update(w_mix=w_o16[j])

    def mlp(st, l, hosted=None):
        tm = host_tm if hosted is not None else st["tm"]
        out = _mix_mlp(st["x"], st["z"], st["mods"], st["nb"], st["w_mix"], pack,
                       w_up16, w_down16, l, tm, st["seq"] // tm if st["prompt"] else 1, hosted)
        if hosted is None:
            st["x"] = out
            return None
        st["x"] = out[0]
        return out[1]

    grp_p, grp_s = setup(x_prompt, True), setup(x_sample, False)
    for l in range(depth):
        mixer(grp_p, l)
        mixer(grp_s, l)
        paged = grp_s["paged"]
        if paged is not None and host_attn:
            o = mlp(grp_p, l, hosted=paged)
        else:
            mlp(grp_p, l)
            o = _attn_paged(*paged) if paged is not None else None
        if o is not None:
            grp_s["z"] = unpad_rows(o, grp_s["nb"], grp_s["seq"])
        mlp(grp_s, l)

    def finish(st):
        nb, seq = st["nb"], st["seq"]
        k, v, y = st["k"], st["v"], st["x"]
        if st["prompt"]:
            y = y.reshape(nb, seq, d)
        else:
            k = k.reshape(n_heads, seq, nb, hd).transpose(2, 0, 1, 3)
            v = v.reshape(n_heads, seq, nb, hd).transpose(2, 0, 1, 3)
            y = y.reshape(seq, nb, d).transpose(1, 0, 2)
        return y, jnp.stack(st["conv_new"]), jnp.stack(st["h_new"]), k, v

    y_p, conv_p, h_p, k_p, v_p = finish(grp_p)
    y_s, conv_s, h_s, k_s, v_s = finish(grp_s)
    return (y_p, y_s, conv_p, h_p, conv_s, h_s, k_p, v_p, k_s, v_s)
```

```python
import functools
import math

import jax
import jax.numpy as jnp
from jax import lax
from jax.experimental import pallas as pl
from jax.experimental.pallas import tpu as pltpu

F32 = jnp.float32
BF16 = jnp.bfloat16

NORM_EPS = 1e-6
LRU_C = 8.0
CONV_WIDTH = 4
PAGE_SIZE = 128
MOBA_BLOCK = 256
MOBA_TOP_K = 3
PAGES_PER_BLOCK = MOBA_BLOCK // PAGE_SIZE
N_HEADS = 8
ROPE_THETA = 500000.0
NEG = -1e30
LOG2_E = math.log2(math.e)
LANES = 128
SUBLANES = 8
GATE_GROUP = 256
VMEM_LIMIT = 56 * 1024 * 1024


def _params(*sem):
    return pltpu.CompilerParams(dimension_semantics=sem, vmem_limit_bytes=VMEM_LIMIT)


def _resident(shape):
    nd = len(shape)
    return pl.BlockSpec(shape, lambda *_: (0,) * nd, pipeline_mode=pl.Buffered(1))


def _resident_layer(shape, layer):
    nd = len(shape)
    return pl.BlockSpec((1,) + tuple(shape[1:]), lambda *_: (layer,) + (0,) * (nd - 1),
                        pipeline_mode=pl.Buffered(1))


def _dot(a, b):
    return jnp.dot(a, b, preferred_element_type=F32)


def _dot_nt(a, b):
    return lax.dot_general(a, b, (((1,), (1,)), ((), ())), preferred_element_type=F32)


def _split(a):
    hi = a.astype(BF16)
    return hi, (a - hi.astype(F32)).astype(BF16)


def _dot3(a, b, dot=_dot):
    ah, al = _split(a)
    bh, bl = _split(b)
    return dot(ah, bh) + (dot(ah, bl) + dot(al, bh))


def _sigmoid(x):
    return 0.5 + 0.5 * jnp.tanh(0.5 * x)


def _gelu_tanh(x):
    return 0.5 * x * (1.0 + jnp.tanh(math.sqrt(2.0 / math.pi) * (x + 0.044715 * (x * x * x))))


def _softplus(x):
    return jnp.maximum(x, 0.0) + jnp.log1p(jnp.exp(-jnp.abs(x)))


def _modulate(x, g, mod, d):
    ms = jnp.mean(x * x, axis=-1, keepdims=True)
    y = x * lax.rsqrt(ms + NORM_EPS) * g
    return y * (1.0 + mod[:, d:2 * d]) + mod[:, :d]


def _adaln_kernel(c_ref, w_ref, b_ref, o_ref):
    c = c_ref[...]
    a = c * _sigmoid(c)
    r = a.shape[0]
    a_hi = a.astype(BF16).astype(F32)
    w_hi, w_lo = _split(w_ref[0])
    stacked = _dot(jnp.concatenate([a_hi, a - a_hi], axis=0).astype(BF16), w_hi)
    o_ref[0] = stacked[:r] + (stacked[r:] + _dot(a_hi.astype(BF16), w_lo)) + b_ref[0]


def _adaln(c, w, b):
    r, d = c.shape
    m, _, n = w.shape
    tn = 1024
    return pl.pallas_call(
        _adaln_kernel,
        grid=(m, n // tn),
        in_specs=[
            pl.BlockSpec((r, d), lambda i, j: (0, 0)),
            pl.BlockSpec((1, d, tn), lambda i, j: (i, 0, j)),
            pl.BlockSpec((1, 1, tn), lambda i, j: (i, 0, j)),
        ],
        out_specs=pl.BlockSpec((1, r, tn), lambda i, j: (i, 0, j)),
        out_shape=jax.ShapeDtypeStruct((m, r, n), F32),
        compiler_params=_params("parallel", "parallel"),
        name="adaln",
    )(c, w, b)


def _mod_spec(mod, idx, nb, tm, tiles_per_seq):
    _, rows, w = mod.shape
    if rows == 1:
        return pl.BlockSpec((1, 1, w), lambda i, *_: (idx * nb + i // tiles_per_seq, 0, 0))
    assert tm % rows == 0
    return pl.BlockSpec((1, rows, w), lambda i, *_: (idx, 0, 0))


def _mod_rows(mod_ref, tm):
    mod = mod_ref[0]
    if mod.shape[0] in (1, tm):
        return mod
    return jnp.concatenate([mod] * (tm // mod.shape[0]), axis=0)


ROW_G1, ROW_G2, ROW_CONV_B, ROW_LAM, ROW_GATE_B, ROW_CONV_W, PACK_ROWS = 0, 1, 2, 3, 4, 6, 16


def _row(pk_ref, r, n=1):
    return pk_ref[0, r:r + n, :]


def _rec_in_kernel(x_ref, mod_ref, pk_ref, w_ref, ga_ref, xb_ref, *, d, dr):
    h = _modulate(x_ref[...], _row(pk_ref, ROW_G1), _mod_rows(mod_ref, x_ref.shape[0]),
                  d).astype(BF16)
    y = _dot(h, w_ref[...])
    ga_ref[...] = _gelu_tanh(y[:, :dr])
    xb_ref[...] = y[:, dr:]


def _rec_in(x, mod, mod_idx, nb, pk, layer, w, tm, tiles_per_seq):
    t, d = x.shape
    dr = w.shape[1] // 2
    row = pl.BlockSpec((tm, dr), lambda i: (i, 0))
    return pl.pallas_call(
        functools.partial(_rec_in_kernel, d=d, dr=dr),
        grid=(t // tm,),
        in_specs=[
            pl.BlockSpec((tm, d), lambda i: (i, 0)),
            _mod_spec(mod, mod_idx, nb, tm, tiles_per_seq),
            _resident_layer(pk.shape, layer),
            _resident(w.shape),
        ],
        out_specs=[row, row],
        out_shape=[jax.ShapeDtypeStruct((t, dr), F32)] * 2,
        compiler_params=_params("parallel"),
        name="rec_in",
    )(x, mod, pk, w)


def _rope_partner(hd):
    half = hd // 8
    i = jnp.arange(hd)
    return jnp.where(i < half, i + half, jnp.where(i < 2 * half, i - half, i))


def _with_partner_columns(w, n_rope, hd):
    d = w.shape[0]
    wr = w[:, :n_rope * hd].reshape(d, n_rope, hd)[:, :, _rope_partner(hd)]
    return jnp.concatenate([w, wr.reshape(d, n_rope * hd)], axis=1)


def _head_proj_kernel(x_ref, mod_ref, pk_ref, w_ref, gh_ref, ghp_ref, cos_ref, sin_ref, *out_refs,
                      d, hd, n_rope, n_plain):
    h_ref = out_refs[-1]
    h_ref[...] = _modulate(x_ref[...], _row(pk_ref, ROW_G1), _mod_rows(mod_ref, x_ref.shape[0]),
                           d).astype(BF16)
    cg = cos_ref[...] * gh_ref[0]
    sg = sin_ref[...] * ghp_ref[0]
    partner0 = n_rope + n_plain
    roped, plain = [], []
    for i0 in range(0, n_rope + n_plain, 2):
        y = _dot(h_ref[...], w_ref[:, i0 * hd:(i0 + 2) * hd])
        if i0 < n_rope:
            yp = _dot(h_ref[...], w_ref[:, (partner0 + i0) * hd:(partner0 + i0 + 2) * hd])
        for i in (i0, i0 + 1):
            cols = slice((i - i0) * hd, (i - i0 + 1) * hd)
            if i >= n_rope:
                plain.append(y[:, cols])
                continue
            yh = y[:, cols]
            rs = lax.rsqrt(jnp.mean(yh * yh, axis=-1, keepdims=True) + NORM_EPS)
            roped.append(rs * (yh * cg + yp[:, cols] * sg))
    out_refs[0][0] = jnp.stack(roped, axis=0)
    if plain:
        out_refs[1][0] = jnp.stack(plain, axis=0)


def _head_proj(x, mod, mod_idx, mod_nb, pk, layer, w, gains, gain_idx, tables, nb, seq, tm,
               n_rope, n_plain, hd):
    t, d = x.shape
    assert n_rope % 2 == 0 and n_plain % 2 == 0
    tps = seq // tm
    tab = pl.BlockSpec((tm, hd), lambda i: (i % tps, 0))
    out_specs = [pl.BlockSpec((1, n_rope, tm, hd), lambda i: (i // tps, 0, i % tps, 0))]
    out_shape = [jax.ShapeDtypeStruct((nb, n_rope, seq, hd), F32)]
    if n_plain:
        out_specs.append(pl.BlockSpec((1, n_plain, tm, hd), lambda i: (i // tps, 0, i % tps, 0)))
        out_shape.append(jax.ShapeDtypeStruct((nb, n_plain, seq, hd), F32))
    return pl.pallas_call(
        functools.partial(_head_proj_kernel, d=d, hd=hd, n_rope=n_rope, n_plain=n_plain),
        grid=(t // tm,),
        in_specs=[
            pl.BlockSpec((tm, d), lambda i: (i, 0)),
            _mod_spec(mod, mod_idx, mod_nb, tm, tps),
            _resident_layer(pk.shape, layer),
            _resident(w.shape),
            _resident_layer(gains[0].shape, gain_idx),
            _resident_layer(gains[1].shape, gain_idx),
            tab, tab,
        ],
        out_specs=out_specs,
        out_shape=out_shape,
        scratch_shapes=[pltpu.VMEM((tm, d), BF16)],
        compiler_params=_params("parallel"),
        name="head_proj",
    )(x, mod, pk, w, *gains, *tables)


FF_CHUNK = 1024


PAGES_PER_MEANS_STEP = 16


def _page_block_means(k_refs, o_ref):
    n = len(k_refs)
    n_heads, _, hd = k_refs[0].shape[1:]
    for h in range(n_heads):
        sums = [jnp.sum(k_refs[j][0, h], axis=0, keepdims=True) for j in range(n)]
        o_ref[0, :, h * hd:(h + 1) * hd] = jnp.concatenate(
            [(sums[2 * i] + sums[2 * i + 1]) * (1.0 / MOBA_BLOCK) for i in range(n // 2)], axis=0)


def _mix_mlp_kernel(*refs, d, attn):
    if attn:
        sel_ref, refs = refs[0], refs[1:]
        _paged_heads_step(sel_ref, *refs[8:13], refs[14], *refs[18:21], **attn)
        refs = refs[:8] + (refs[13],) + refs[15:18]
    x_ref, z_ref, mod1_ref, wo_ref, mod2_ref, pk_ref, wu_ref, wd_ref, o_ref = refs[:9]
    x1_ref, h_ref, acc_ref = refs[-3:]
    tm = x_ref.shape[0]
    x1 = x_ref[...] + _mod_rows(mod1_ref, tm)[:, 2 * d:] * _dot(z_ref[...].astype(BF16),
                                                                wo_ref[...])
    x1_ref[...] = x1
    mod2 = _mod_rows(mod2_ref, tm)
    h_ref[...] = _modulate(x1, _row(pk_ref, ROW_G2), mod2, d).astype(BF16)
    for c in range(wu_ref.shape[2] // FF_CHUNK):
        cols = slice(c * FF_CHUNK, (c + 1) * FF_CHUNK)
        u = _dot(h_ref[...], wu_ref[0, :, cols])
        a = jnp.square(jnp.maximum(u, 0.0)).astype(BF16)
        part = _dot(a, wd_ref[0, cols, :])
        if c == 0:
            acc_ref[...] = part
        else:
            acc_ref[...] += part
    o_ref[...] = x1_ref[...] + mod2[:, 2 * d:] * acc_ref[...]


def _mix_mlp(x, z, mods, mod_nb, wo, pk, wu, wd, layer, tm, tiles_per_seq, hosted=None):
    t, d = x.shape
    assert wu.shape[2] % FF_CHUNK == 0
    steps = t // tm
    row = pl.BlockSpec((tm, d), lambda i, *_: (i, 0))
    in_specs = [
        row,
        pl.BlockSpec((tm, z.shape[1]), lambda i, *_: (i, 0)),
        _mod_spec(mods, 2 * layer, mod_nb, tm, tiles_per_seq),
        _resident(wo.shape),
        _mod_spec(mods, 2 * layer + 1, mod_nb, tm, tiles_per_seq),
        _resident_layer(pk.shape, layer),
        _resident_layer(wu.shape, layer),
        _resident_layer(wd.shape, layer),
    ]
    scratch = [pltpu.VMEM((tm, d), F32), pltpu.VMEM((tm, d), BF16), pltpu.VMEM((tm, d), F32)]
    out_specs, out_shape = row, jax.ShapeDtypeStruct((t, d), F32)
    args = (x, z, mods, wo, mods, pk, wu, wd)
    if hosted is None:
        return pl.pallas_call(
            functools.partial(_mix_mlp_kernel, d=d, attn=None),
            grid=(steps,), in_specs=in_specs, out_specs=out_specs, out_shape=out_shape,
            scratch_shapes=scratch, compiler_params=_params("parallel"), name="mix_mlp",
        )(*args)
    q, k_new, v_new, cache_k, cache_v, sel, sq = hosted
    db, h, rows, hd = q.shape
    assert steps % db == 0 and h % (steps // db) == 0
    spq = steps // db
    hps = h // spq
    n_slots = sq * MOBA_TOP_K * PAGES_PER_BLOCK
    heads = pl.BlockSpec((1, hps, rows, hd), lambda i, sel: (i // spq, i % spq, 0, 0))
    hbm = pl.BlockSpec(memory_space=pl.ANY)
    return pl.pallas_call(
        functools.partial(_mix_mlp_kernel, d=d,
                          attn=dict(n_slots=n_slots, sq=sq, scale=hd ** -0.5, n_heads=h)),
        grid_spec=pltpu.PrefetchScalarGridSpec(
            num_scalar_prefetch=1,
            grid=(steps,),
            in_specs=in_specs + [heads, heads, heads, hbm, hbm],
            out_specs=[out_specs, heads],
            scratch_shapes=scratch + [pltpu.VMEM((hps, n_slots, PAGE_SIZE, hd), F32),
                                      pltpu.VMEM((hps, n_slots, PAGE_SIZE, hd), F32),
                                      pltpu.SemaphoreType.DMA((2,))],
        ),
        out_shape=[out_shape, jax.ShapeDtypeStruct((db, h, rows, hd), F32)],
        compiler_params=_params("arbitrary"),
        name="mix_mlp_attn",
    )(_page_rows(sel, h, n_slots), *args, q, k_new, v_new,
      cache_k.reshape(-1, PAGE_SIZE, hd), cache_v.reshape(-1, PAGE_SIZE, hd))


def _lru_inputs(xc, wr_ref, wi_ref, pk_ref):
    xh = xc.astype(BF16)
    gw = wr_ref.shape[2]
    r = jnp.concatenate([_dot(xh[:, j * gw:(j + 1) * gw], wr_ref[0, j])
                         for j in range(wr_ref.shape[1])], axis=1) + _row(pk_ref, ROW_GATE_B)
    i = jnp.concatenate([_dot(xh[:, j * gw:(j + 1) * gw], wi_ref[0, j])
                         for j in range(wi_ref.shape[1])], axis=1) + _row(pk_ref, ROW_GATE_B + 1)
    log_a = -LRU_C * _sigmoid(r) * _softplus(-_row(pk_ref, ROW_LAM))
    t = jnp.tanh(0.5 * log_a)
    rc = 1.0 / (1.0 - t)
    a = (1.0 + t) * rc
    u = (2.0 * rc) * jnp.sqrt(-t) * (_sigmoid(i) * xc)
    return a, u


def _scan_prompt_kernel(*refs, tt, n_side):
    if n_side:
        refs = refs[1:]
        _page_block_means(refs[6:6 + n_side], refs[9 + n_side])
    x_ref, mod_ref, pk_ref, win_ref, wr_ref, wi_ref = refs[:6]
    z_ref, tail_out_ref, hl_ref = refs[6 + n_side:9 + n_side]
    tail_ref, hc_ref, a_ref, u_ref, ga_ref = refs[-5:]
    ti = pl.program_id(1)
    d, dr = x_ref.shape[1], z_ref.shape[1]

    @pl.when(ti == 0)
    def _():
        tail_ref[...] = jnp.zeros(tail_ref.shape, F32)
        hc_ref[...] = jnp.zeros(hc_ref.shape, F32)

    y = _dot(_modulate(x_ref[...], _row(pk_ref, ROW_G1), mod_ref[0], d).astype(BF16),
             win_ref[...])
    ga_ref[...] = _gelu_tanh(y[:, :dr])
    x = y[:, dr:]
    tail = tail_ref[...]
    row8 = lax.broadcasted_iota(jnp.int32, (SUBLANES, dr), 0)
    xc = _row(pk_ref, ROW_CONV_B) + _row(pk_ref, ROW_CONV_W + CONV_WIDTH - 1) * x
    for k in range(1, CONV_WIDTH):
        xs = pltpu.roll(x, k, 0)
        first = jnp.where(row8 < k, pltpu.roll(tail, k, 0), xs[:SUBLANES])
        xs = jnp.concatenate([first, xs[SUBLANES:]], axis=0)
        xc = xc + _row(pk_ref, ROW_CONV_W + CONV_WIDTH - 1 - k) * xs
    tail_ref[...] = x[tt - SUBLANES:]
    tail_out_ref[0] = x[tt - SUBLANES:]

    a, u = _lru_inputs(xc, wr_ref, wi_ref, pk_ref)
    a_ref[...] = a
    u_ref[...] = u

    def group(gi, carry):
        r0 = pl.multiple_of(gi * SUBLANES, SUBLANES)
        a8 = a_ref[pl.ds(r0, SUBLANES), :]
        u8 = u_ref[pl.ds(r0, SUBLANES), :]
        for s in (1, 2, 4):
            keep = row8 >= s
            u8 = jnp.where(keep, a8 * pltpu.roll(u8, s, 0) + u8, u8)
            a8 = jnp.where(keep, a8 * pltpu.roll(a8, s, 0), a8)
        h8 = a8 * carry + u8
        z_ref[pl.ds(r0, SUBLANES), :] = h8 * ga_ref[pl.ds(r0, SUBLANES), :]
        return jnp.broadcast_to(h8[SUBLANES - 1:SUBLANES, :], (SUBLANES, dr))

    carry = lax.fori_loop(0, tt // SUBLANES, group,
                          jnp.broadcast_to(hc_ref[...], (SUBLANES, dr)), unroll=4)
    hc_ref[...] = carry[0:1, :]
    hl_ref[0] = carry[0:1, :]


def _scan_prompt(x, mods, mod_idx, pk, layer, w_in, wr, wi, nb, seq, tt, side=None):
    t, d = x.shape
    dr = w_in.shape[1] // 2
    nt = seq // tt
    row = pl.BlockSpec((tt, dr), lambda b, i, *_: (b * nt + i, 0))
    per_seq8 = pl.BlockSpec((1, SUBLANES, dr), lambda b, i, *_: (b, 0, 0))
    per_seq1 = pl.BlockSpec((1, 1, dr), lambda b, i, *_: (b, 0, 0))
    in_specs = [pl.BlockSpec((tt, d), lambda b, i, *_: (b * nt + i, 0)),
                pl.BlockSpec((1, 1, mods.shape[2]), lambda b, i, *_: (mod_idx * nb + b, 0, 0)),
                _resident_layer(pk.shape, layer), _resident(w_in.shape),
                _resident_layer(wr.shape, layer), _resident_layer(wi.shape, layer)]
    out_specs = [row, per_seq8, per_seq1]
    out_shape = [jax.ShapeDtypeStruct((t, dr), F32),
                 jax.ShapeDtypeStruct((nb, SUBLANES, dr), F32),
                 jax.ShapeDtypeStruct((nb, 1, dr), F32)]
    scratch = [pltpu.VMEM((SUBLANES, dr), F32), pltpu.VMEM((1, dr), F32),
               pltpu.VMEM((tt, dr), F32), pltpu.VMEM((tt, dr), F32), pltpu.VMEM((tt, dr), F32)]
    args = (x, mods, pk, w_in, wr, wi)
    if side is None:
        return pl.pallas_call(
            functools.partial(_scan_prompt_kernel, tt=tt, n_side=0),
            grid=(nb, nt), in_specs=in_specs, out_specs=out_specs, out_shape=out_shape,
            scratch_shapes=scratch, compiler_params=_params("parallel", "arbitrary"),
            name="scan_prompt",
        )(*args)
    cache_k, page_table, first_seq, n_seqs = side
    n = PAGES_PER_MEANS_STEP
    n_pages = page_table.shape[1]
    _, h, _, hd = cache_k.shape
    gps = n_pages // n
    last = n_seqs * gps - 1

    def group(b, i):
        return jnp.minimum(b * nt + i, last)

    def page(j):
        return pl.BlockSpec(
            (1, h, PAGE_SIZE, hd),
            lambda b, i, pt: (pt[first_seq + group(b, i) // gps, (group(b, i) % gps) * n + j],
                              0, 0, 0))

    return pl.pallas_call(
        functools.partial(_scan_prompt_kernel, tt=tt, n_side=n),
        grid_spec=pltpu.PrefetchScalarGridSpec(
            num_scalar_prefetch=1,
            grid=(nb, nt),
            in_specs=in_specs + [page(j) for j in range(n)],
            out_specs=out_specs + [pl.BlockSpec(
                (1, n // PAGES_PER_BLOCK, h * hd),
                lambda b, i, pt: (group(b, i) // gps, group(b, i) % gps, 0))],
            scratch_shapes=scratch,
        ),
        out_shape=out_shape + [
            jax.ShapeDtypeStruct((n_seqs, n_pages // PAGES_PER_BLOCK, h * hd), F32)],
        compiler_params=_params("arbitrary", "arbitrary"),
        name="scan_prompt_means",
    )(page_table, *args, *([cache_k] * n))


def _scan_steps_kernel(xb_ref, ga_ref, st_ref, h0_ref, pk_ref, wr_ref, wi_ref, z_ref, hl_ref,
                       *, steps):
    rows = [st_ref[0, j] for j in range(CONV_WIDTH - 1)] + [xb_ref[t] for t in range(steps)]
    h = h0_ref[0]
    for t in range(steps):
        xc = _row(pk_ref, ROW_CONV_B)
        for j in range(CONV_WIDTH):
            xc = xc + _row(pk_ref, ROW_CONV_W + j) * rows[t + j]
        a, u = _lru_inputs(xc, wr_ref, wi_ref, pk_ref)
        h = a * h + u
        z_ref[t] = h * ga_ref[t]
    hl_ref[...] = h


def _scan_steps(xb, ga, st, h0, pk, wr, wi, layer):
    steps, nb, dr = xb.shape
    args = (xb, ga, st, h0, pk, wr, wi)
    return pl.pallas_call(
        functools.partial(_scan_steps_kernel, steps=steps),
        grid=(1,),
        in_specs=[_resident(xb.shape), _resident(ga.shape)]
        + [_resident_layer(a.shape, layer) for a in args[2:]],
        out_specs=[pl.BlockSpec((steps, nb, dr), lambda i: (0, 0, 0)),
                   pl.BlockSpec((nb, dr), lambda i: (0, 0))],
        out_shape=[jax.ShapeDtypeStruct((steps, nb, dr), F32),
                   jax.ShapeDtypeStruct((nb, dr), F32)],
        compiler_params=_params("arbitrary"),
        name="scan_steps",
    )(*args)


ATTN_KEY_CHUNK = 128


def _block_bias_t(gate_t, cur):
    row = lax.broadcasted_iota(jnp.int32, gate_t.shape, 0)
    row_f = row.astype(F32)
    g = jnp.where(row < cur, gate_t, NEG)
    bias = jnp.where(row == cur, 0.0, NEG)
    for _ in range(min(MOBA_TOP_K, cur)):
        m = jnp.max(g, axis=0, keepdims=True)
        idx = jnp.min(jnp.where(g == m, row_f, float(LANES)), axis=0, keepdims=True)
        pick = row_f == idx
        bias = jnp.where(pick, 0.0, bias)
        g = jnp.where(pick, NEG, g)
    return bias


def _attn_prompt_kernel(q_ref, k_ref, v_ref, o_ref, kaug_ref, vt_ref, means_ref, *, seq, hd, scale):
    bq = MOBA_BLOCK
    nblk = seq // bq
    nbp = means_ref.shape[0]

    lane = lax.broadcasted_iota(jnp.int32, (bq, hd), 1)
    means_ref[...] = jnp.zeros(means_ref.shape, F32)
    for n in range(nblk):
        kb = k_ref[0, 0, n * bq:(n + 1) * bq, :]
        means_ref[n:n + 1, :] = jnp.sum(kb, axis=0, keepdims=True) * (1.0 / MOBA_BLOCK)
        onehot = jnp.where(lane == n, 1.0, 0.0).astype(BF16)
        kaug_ref[n * bq:(n + 1) * bq, :] = jnp.concatenate([kb.astype(BF16), onehot], axis=1)
        vt_ref[:, n * bq:(n + 1) * bq] = v_ref[0, 0, n * bq:(n + 1) * bq, :].T.astype(BF16)

    kc = ATTN_KEY_CHUNK
    key_idx = lax.broadcasted_iota(jnp.int32, (kc, bq), 0)
    query_idx = lax.broadcasted_iota(jnp.int32, (kc, bq), 1)

    outs = []
    for c in range(nblk):
        qt = q_ref[0, 0, c * bq:(c + 1) * bq, :].T
        gate_t = _dot3(means_ref[...], qt)
        qt_aug = jnp.concatenate(
            [qt * (scale * LOG2_E), _block_bias_t(gate_t, c), jnp.zeros((hd - nbp, bq), F32)],
            axis=0).astype(BF16)
        m = l = acc = None
        own = list(range(c * bq // kc, (c + 1) * bq // kc))
        for j in own + list(range(c * bq // kc)):
            s = _dot(kaug_ref[j * kc:(j + 1) * kc, :], qt_aug)
            if j in own:
                s = jnp.where(key_idx + (j * kc - c * bq) <= query_idx, s, NEG)
            mn = jnp.max(s, axis=0, keepdims=True)
            m_new = mn if m is None else jnp.maximum(m, mn)
            p = jnp.exp2(s - m_new)
            ln = jnp.sum(p, axis=0, keepdims=True)
            on = _dot(vt_ref[:, j * kc:(j + 1) * kc], p.astype(BF16))
            if m is None:
                l, acc = ln, on
            else:
                wa = jnp.exp2(m - m_new)
                l, acc = l * wa + ln, acc * wa + on
            m = m_new
        outs.append((acc * (1.0 / l)).T)
    o_ref[0] = jnp.concatenate(outs, axis=0)


def _attn_prompt(q, k, v):
    b, h, seq, hd = q.shape
    nblk = seq // MOBA_BLOCK
    assert seq % MOBA_BLOCK == 0 and nblk <= hd and hd == LANES
    nbp = -(-nblk // SUBLANES) * SUBLANES
    full = pl.BlockSpec((1, 1, seq, hd), lambda bi, hi: (bi, hi, 0, 0))
    return pl.pallas_call(
        functools.partial(_attn_prompt_kernel, seq=seq, hd=hd, scale=hd ** -0.5),
        grid=(b, h),
        in_specs=[full, full, full],
        out_specs=pl.BlockSpec((1, seq, hd), lambda bi, hi: (bi, 0, hi)),
        out_shape=jax.ShapeDtypeStruct((b, seq, h * hd), F32),
        scratch_shapes=[pltpu.VMEM((seq, 2 * hd), BF16), pltpu.VMEM((hd, seq), BF16),
                        pltpu.VMEM((nbp, hd), F32)],
        compiler_params=_params("parallel", "parallel"),
        name="attn_prompt",
    )(q, k, v)


def _page_means_kernel(pt_ref, *refs, n):
    _page_block_means(refs[:n], refs[n])


def _page_means(cache_k, page_table):
    db, n_pages = page_table.shape
    _, h, _, hd = cache_k.shape
    n = min(PAGES_PER_MEANS_STEP, n_pages)

    def page(j):
        return pl.BlockSpec((1, h, PAGE_SIZE, hd), lambda b, g, pt: (pt[b, g * n + j], 0, 0, 0))

    return pl.pallas_call(
        functools.partial(_page_means_kernel, n=n),
        grid_spec=pltpu.PrefetchScalarGridSpec(
            num_scalar_prefetch=1,
            grid=(db, n_pages // n),
            in_specs=[page(j) for j in range(n)],
            out_specs=pl.BlockSpec((1, n // 2, h * hd), lambda b, g, pt: (b, g, 0)),
        ),
        out_shape=jax.ShapeDtypeStruct((db, n_pages // 2, h * hd), F32),
        compiler_params=_params("parallel", "arbitrary"),
        name="page_means",
    )(page_table, *([cache_k] * n))


def _select_kernel(q_ref, means_ref, pt_ref, sel_ref):
    n_blocks = means_ref.shape[1]
    outs = []
    for i in range(q_ref.shape[0]):
        means = jnp.concatenate(
            [means_ref[i], jnp.zeros((LANES - n_blocks, means_ref.shape[2]), F32)], axis=0)
        gate = _dot3(q_ref[i], means, _dot_nt)
        lane = lax.broadcasted_iota(jnp.int32, gate.shape, 1)
        lane_f = lane.astype(F32)
        pages = pt_ref[i]
        g = jnp.where(lane < n_blocks, gate, NEG)
        out = jnp.zeros(gate.shape, F32)
        for j in range(MOBA_TOP_K):
            m = jnp.max(g, axis=1, keepdims=True)
            idx = jnp.min(jnp.where(g == m, lane_f, float(LANES)), axis=1, keepdims=True)
            for p in range(PAGES_PER_BLOCK):
                phys = jnp.sum(jnp.where(lane_f == idx * PAGES_PER_BLOCK + p, pages, 0.0),
                               axis=1, keepdims=True)
                out = jnp.where(lane == j * PAGES_PER_BLOCK + p, phys, out)
            g = jnp.where(lane_f == idx, NEG, g)
        outs.append(out.astype(jnp.int32))
    sel_ref[...] = jnp.stack(outs, axis=0)


def _select(q_bd, means, pages_f):
    db, r, w = q_bd.shape
    nb = means.shape[1]
    assert MOBA_TOP_K <= nb and nb * PAGES_PER_BLOCK <= LANES
    g = math.gcd(db, 8)
    return pl.pallas_call(
        _select_kernel,
        grid=(db // g,),
        in_specs=[pl.BlockSpec((g, r, w), lambda b: (b, 0, 0)),
                  pl.BlockSpec((g, nb, w), lambda b: (b, 0, 0)),
                  pl.BlockSpec((g, 1, LANES), lambda b: (b, 0, 0))],
        out_specs=pl.BlockSpec((g, r, LANES), lambda b: (b, 0, 0)),
        out_shape=jax.ShapeDtypeStruct((db, r, LANES), jnp.int32),
        compiler_params=_params("parallel"),
        name="select_blocks",
    )(q_bd, means, pages_f)


def _attn_paged_kernel(sel_ref, q_ref, kn_ref, vn_ref, ck_ref, cv_ref, o_ref, kbuf, vbuf, sems,
                       *, n_slots, sq, scale):
    b = pl.program_id(0)
    n_heads = q_ref.shape[1]
    assert n_heads % 2 == 0

    def page_copies(seq, h, page_of_slot):
        out = []
        for s in range(n_slots):
            row = page_of_slot(seq, h * n_slots + s)
            out.append(pltpu.make_async_copy(ck_ref.at[row], kbuf.at[h % 2, s], sems.at[0, h % 2]))
            out.append(pltpu.make_async_copy(cv_ref.at[row], vbuf.at[h % 2, s], sems.at[1, h % 2]))
        return out

    def start_all(copies):
        for c in copies:
            c.start()

    @pl.when(b == 0)
    def _():
        start_all(page_copies(b, 0, lambda seq, i: sel_ref[seq, i]))

    for h in range(n_heads):
        if h + 1 < n_heads:
            start_all(page_copies(b, h + 1, lambda seq, i: sel_ref[seq, i]))
        else:
            @pl.when(b + 1 < pl.num_programs(0))
            def _():
                start_all(page_copies(b + 1, 0, lambda seq, i: sel_ref[seq, i]))
        for c in page_copies(0, h, lambda seq, i: 0):
            c.wait()
        o_ref[0, h] = _paged_head(q_ref[0, h], kn_ref[0, h], vn_ref[0, h], kbuf[h % 2],
                                  vbuf[h % 2], sq, scale)


def _page_rows(sel, n_heads, n_slots):
    head_of_slot = jnp.repeat(jnp.arange(n_heads, dtype=jnp.int32), n_slots)
    return sel * n_heads + head_of_slot[None, :]


def _paged_heads_step(sel_ref, q_ref, kn_ref, vn_ref, ck_ref, cv_ref, o_ref, kbuf, vbuf, sems,
                      *, n_slots, sq, scale, n_heads):
    i = pl.program_id(0)
    hps = q_ref.shape[1]
    spq = n_heads // hps

    def page_copies(step, page_of):
        seq, h0 = step // spq, (step % spq) * hps
        out = []
        for j in range(hps):
            for s in range(n_slots):
                row = page_of(seq, (h0 + j) * n_slots + s)
                out.append(pltpu.make_async_copy(ck_ref.at[row], kbuf.at[j, s], sems.at[0]))
                out.append(pltpu.make_async_copy(cv_ref.at[row], vbuf.at[j, s], sems.at[1]))
        return out

    def start_all(copies):
        for c in copies:
            c.start()

    @pl.when(i == 0)
    def _():
        start_all(page_copies(i, lambda seq, n: sel_ref[seq, n]))

    for c in page_copies(0, lambda seq, n: 0):
        c.wait()
    for j in range(hps):
        o_ref[0, j] = _paged_head(q_ref[0, j], kn_ref[0, j], vn_ref[0, j], kbuf[j], vbuf[j],
                                  sq, scale)

    @pl.when(i + 1 < pl.num_programs(0))
    def _():
        start_all(page_copies(i + 1, lambda seq, n: sel_ref[seq, n]))


def _paged_head(q, kn, vn, k_pages, v_pages, sq, scale):
    n_slots, _, hd = k_pages.shape
    k_all = k_pages.reshape(n_slots * PAGE_SIZE, hd).astype(BF16)
    v_all = v_pages.reshape(n_slots * PAGE_SIZE, hd).astype(BF16)
    s = _dot_nt((q * scale).astype(BF16), k_all)
    keys_per_q = MOBA_TOP_K * MOBA_BLOCK
    row = lax.broadcasted_iota(jnp.int32, s.shape, 0)
    col = lax.broadcasted_iota(jnp.int32, s.shape, 1)
    s = jnp.where((col >= row * keys_per_q) & (col < (row + 1) * keys_per_q), s, NEG)
    row1 = lax.broadcasted_iota(jnp.int32, (q.shape[0], 1), 0)
    s_new = [jnp.where(row1 >= j, jnp.sum(q * kn[j:j + 1, :], axis=1, keepdims=True) * scale, NEG)
             for j in range(sq)]
    m = functools.reduce(jnp.maximum, s_new + [jnp.max(s, axis=1, keepdims=True)])
    p = jnp.exp(s - m)
    p_new = [jnp.exp(sj - m) for sj in s_new]
    l = functools.reduce(lambda x, y: x + y, p_new + [jnp.sum(p, axis=1, keepdims=True)])
    acc = functools.reduce(lambda x, y: x + y,
                           [p_new[j] * vn[j:j + 1, :] for j in range(sq)]
                           + [_dot(p.astype(BF16), v_all)])
    return acc / l


def _attn_paged(q, k_new, v_new, cache_k, cache_v, sel, sq):
    db, h, rows, hd = q.shape
    n_slots = sq * MOBA_TOP_K * PAGES_PER_BLOCK
    per_seq = pl.BlockSpec((1, h, rows, hd), lambda b, sel: (b, 0, 0, 0))
    hbm = pl.BlockSpec(memory_space=pl.ANY)
    sel_rows = _page_rows(sel, h, n_slots)
    return pl.pallas_call(
        functools.partial(_attn_paged_kernel, n_slots=n_slots, sq=sq, scale=hd ** -0.5),
        grid_spec=pltpu.PrefetchScalarGridSpec(
            num_scalar_prefetch=1,
            grid=(db,),
            in_specs=[per_seq, per_seq, per_seq, hbm, hbm],
            out_specs=per_seq,
            scratch_shapes=[pltpu.VMEM((2, n_slots, PAGE_SIZE, hd), F32),
                            pltpu.VMEM((2, n_slots, PAGE_SIZE, hd), F32),
                            pltpu.SemaphoreType.DMA((2, 2))],
        ),
        out_shape=jax.ShapeDtypeStruct((db, h, rows, hd), F32),
        compiler_params=_params("arbitrary"),
        name="attn_paged",
    )(sel_rows, q, k_new, v_new, cache_k.reshape(-1, PAGE_SIZE, hd),
      cache_v.reshape(-1, PAGE_SIZE, hd))


def _rope_tables(pos, hd):
    half = hd // 8
    inv = ROPE_THETA ** (-jnp.arange(half, dtype=F32) * 2.0 / (2 * half))
    ang = pos.astype(F32)[:, None] * inv[None, :]
    cos, sin = jnp.cos(ang), jnp.sin(ang)
    n = pos.shape[0]
    ones = jnp.ones((n, hd - 2 * half), F32)
    zeros = jnp.zeros((n, hd - 2 * half), F32)
    return (jnp.concatenate([cos, cos, ones], axis=1),
            jnp.concatenate([-sin, sin, zeros], axis=1))


def _block_diag_gate(w_gate):
    n, k, _ = w_gate.shape
    per = GATE_GROUP // k
    eye = jnp.eye(per, dtype=w_gate.dtype)

    def build(w):
        w = w.reshape(n // per, per, k, k)
        return jnp.einsum("gmkj,mn->gmknj", w, eye).reshape(n // per, GATE_GROUP, GATE_GROUP).astype(BF16)

    return build(w_gate[:, :, :k]), build(w_gate[:, :, k:])


def kernel(x_prompt, x_sample, c_prompt, c_sample, state_conv, state_rglru, cache_k, cache_v, page_table, g_norm, w_ada, b_ada, w_rec_in, conv_w, conv_b, w_gate, b_gate, lam, w_rec_out, g_kv, w_ada_kv, b_ada_kv, w_kv, g_k, w_q, g_q, w_o, w_up, w_down):
    bp, sp, d = x_prompt.shape
    db, sq, _ = x_sample.shape
    depth = g_norm.shape[0]
    n_a = w_rec_in.shape[0]
    dr = w_rec_in.shape[2] // 2
    hd = g_k.shape[0]
    n_heads = w_q.shape[2] // hd
    n_pages = page_table.shape[1]
    assert hd == LANES and n_heads == N_HEADS and sq <= SUBLANES and n_pages % PAGES_PER_BLOCK == 0

    c_all = jnp.concatenate([c_prompt, c_sample], axis=0)
    pad = (-c_all.shape[0]) % SUBLANES
    c_all = jnp.pad(c_all, ((0, pad), (0, 0)))
    mods = _adaln(c_all, w_ada.reshape(depth * 2, d, 3 * d), b_ada.reshape(depth * 2, 1, 3 * d))
    mods_kv = _adaln(c_all, w_ada_kv[None], b_ada_kv[None, None])

    def group_mods(m, prompt):
        if prompt:
            return m[:, :bp].reshape(m.shape[0] * bp, 1, m.shape[2])
        return m[:, bp:bp + db]

    assert d == dr
    n_b = depth - n_a
    pack = jnp.concatenate(
        [g_norm,
         jnp.pad(jnp.concatenate([conv_b[:, None], lam[:, None], b_gate, conv_w], axis=1),
                 ((0, n_b), (0, 0), (0, 0))),
         jnp.zeros((depth, PACK_ROWS - ROW_CONV_W - CONV_WIDTH, d), F32)], axis=1)
    pack_kv = jnp.pad(g_kv[None, None], ((0, 0), (0, PACK_ROWS - 1), (0, 0)))
    head_gains = jnp.concatenate([g_k[None], g_q], axis=0)[:, None]
    head_gains = (head_gains, head_gains[:, :, _rope_partner(hd)])

    w_in16 = [w_rec_in[l].astype(BF16) for l in range(n_a)]
    w_out16 = [w_rec_out[l].astype(BF16) for l in range(n_a)]
    w_kv16 = _with_partner_columns(w_kv.astype(BF16), n_heads, hd)
    w_q16 = [_with_partner_columns(w_q[j].astype(BF16), n_heads, hd) for j in range(depth - n_a)]
    w_o16 = [w_o[j].astype(BF16) for j in range(depth - n_a)]
    w_up16, w_down16 = w_up.astype(BF16), w_down.astype(BF16)
    gates = [_block_diag_gate(w_gate[l]) for l in range(n_a)]
    wr_all, wi_all = (jnp.stack([g[i] for g in gates]) for i in range(2))
    conv0_t = state_conv.transpose(0, 2, 1, 3)
    pages_f = jnp.pad(page_table.astype(F32), ((0, 0), (0, LANES - n_pages)))[:, None, :]

    prompt_tm, scan_tt, host_tm = 512, 256, 256
    scan_steps = bp * (sp // scan_tt)
    gps = n_pages // PAGES_PER_MEANS_STEP
    fuse_means = (n_pages % PAGES_PER_MEANS_STEP == 0 and gps > 0 and scan_steps % gps == 0
                  and n_a * scan_steps >= db * gps)
    host_steps = bp * sp // host_tm
    host_attn = (sp % host_tm == 0 and host_steps % db == 0
                 and n_heads % (host_steps // db) == 0)
    means_parts = []

    def setup(x3, prompt):
        nb, seq, _ = x3.shape
        t = nb * seq
        tm = prompt_tm if prompt else t
        x = x3.reshape(t, d) if prompt else x3.transpose(1, 0, 2).reshape(t, d)
        st = dict(prompt=prompt, nb=nb, seq=seq, t=t, x=x, tm=tm,
                  tps=seq // tm if prompt else 1, conv_new=[], h_new=[],
                  mods=group_mods(mods, prompt), mods_kv=group_mods(mods_kv, prompt))
        if prompt:
            pos = jnp.arange(seq, dtype=jnp.int32)
            st.update(tables=_rope_tables(pos, hd), tab_nb=nb, tab_seq=seq)
        else:
            pos = n_pages * PAGE_SIZE + jnp.arange(seq, dtype=jnp.int32)
            st.update(tables=tuple(jnp.repeat(tb, nb, axis=0) for tb in _rope_tables(pos, hd)),
                      tab_nb=1, tab_seq=t)
        return st

    def pad_rows(a, nb, seq):
        a = a.reshape(n_heads, seq, nb, hd).transpose(2, 0, 1, 3)
        return jnp.pad(a, ((0, 0), (0, 0), (0, SUBLANES - seq), (0, 0)))

    def unpad_rows(o, nb, seq):
        return o[:, :, :seq].transpose(2, 0, 1, 3).reshape(nb * seq, n_heads * hd)

    def mixer(st, l):
        prompt, nb, seq, t, tm, tps, x = (st[k] for k in ("prompt", "nb", "seq", "t", "tm", "tps", "x"))
        st.update(paged=None)
        if l == n_a:
            k, v = _head_proj(x, st["mods_kv"], 0, nb, pack_kv, 0, w_kv16, head_gains, 0,
                              st["tables"], st["tab_nb"], st["tab_seq"], tm, n_heads, n_heads, hd)
            st.update(k=k, v=v)
            if not prompt:
                st.update(means=(jnp.concatenate(means_parts, axis=0) if fuse_means
                                 else _page_means(cache_k, page_table)),
                          k8=pad_rows(k, nb, seq), v8=pad_rows(v, nb, seq))
        if l < n_a:
            if prompt:
                side = None
                first_seq = l * scan_steps // gps if fuse_means else db
                if first_seq < db:
                    side = (cache_k, page_table, first_seq, min(scan_steps // gps, db - first_seq))
                z, tail, h_last, *part = _scan_prompt(x, st["mods"], 2 * l, pack, l, w_in16[l],
                                                      wr_all, wi_all, nb, seq, scan_tt, side)
                means_parts.extend(part)
                st["conv_new"].append(tail[:, SUBLANES - (CONV_WIDTH - 1):])
                st["h_new"].append(h_last[:, 0])
            else:
                ga, xb = _rec_in(x, st["mods"], 2 * l, nb, pack, l, w_in16[l], tm, tps)
                xb_t = xb.reshape(seq, nb, dr)
                z_t, h_last = _scan_steps(xb_t, ga.reshape(seq, nb, dr), conv0_t, state_rglru,
                                          pack, wr_all, wi_all, l)
                z = z_t.reshape(t, dr)
                hist = jnp.concatenate([state_conv[l], xb_t.transpose(1, 0, 2)], axis=1)
                st["conv_new"].append(hist[:, -(CONV_WIDTH - 1):])
                st["h_new"].append(h_last)
            st.update(z=z, w_mix=w_out16[l])
        else:
            j = l - n_a
            (q,) = _head_proj(x, st["mods"], 2 * l, nb, pack, l, w_q16[j], head_gains, 1 + j,
                              st["tables"], st["tab_nb"], st["tab_seq"], tm, n_heads, 0, hd)
            if prompt:
                st.update(z=_attn_prompt(q, st["k"], st["v"]).reshape(t, n_heads * hd))
            else:
                q8 = pad_rows(q, nb, seq)
                q_bd = jnp.einsum("bhqd,hg->bhqgd", q8, jnp.eye(n_heads, dtype=F32))
                sel = _select(q_bd.reshape(nb, n_heads * SUBLANES, n_heads * hd), st["means"],
                              pages_f)
                sel = sel.reshape(nb, n_heads, SUBLANES, LANES)
                sel = sel[:, :, :seq, :MOBA_TOP_K * PAGES_PER_BLOCK].reshape(nb, -1)
                st.update(z=None, paged=(q8, st["k8"], st["v8"], cache_k, cache_v, sel, seq))
            st.update(w_mix=w_o16[j])

    def mlp(st, l, hosted=None):
        tm = host_tm if hosted is not None else st["tm"]
        out = _mix_mlp(st["x"], st["z"], st["mods"], st["nb"], st["w_mix"], pack,
                       w_up16, w_down16, l, tm, st["seq"] // tm if st["prompt"] else 1, hosted)
        if hosted is None:
            st["x"] = out
            return None
        st["x"] = out[0]
        return out[1]

    grp_p, grp_s = setup(x_prompt, True), setup(x_sample, False)
    for l in range(depth):
        mixer(grp_p, l)
        mixer(grp_s, l)
        paged = grp_s["paged"]
        if paged is not None and host_attn:
            o = mlp(grp_p, l, hosted=paged)
        else:
            mlp(grp_p, l)
            o = _attn_paged(*paged) if paged is not None else None
        if o is not None:
            grp_s["z"] = unpad_rows(o, grp_s["nb"], grp_s["seq"])
        mlp(grp_s, l)

    def finish(st):
        nb, seq = st["nb"], st["seq"]
        k, v, y = st["k"], st["v"], st["x"]
        if st["prompt"]:
            y = y.reshape(nb, seq, d)
        else:
            k = k.reshape(n_heads, seq, nb, hd).transpose(2, 0, 1, 3)
            v = v.reshape(n_heads, seq, nb, hd).transpose(2, 0, 1, 3)
            y = y.reshape(seq, nb, d).transpose(1, 0, 2)
        return y, jnp.stack(st["conv_new"]), jnp.stack(st["h_new"]), k, v

    y_p, conv_p, h_p, k_p, v_p = finish(grp_p)
    y_s, conv_s, h_s, k_s, v_s = finish(grp_s)
    return (y_p, y_s, conv_p, h_p, conv_s, h_s, k_p, v_p, k_s, v_s)
```
